```python
import math
import jax, jax.numpy as jnp
from jax import lax
import numpy as np

D_MODEL = 1024
BATCH = 8
SEQ = 2048
DEPTH = 2
DEC_BATCH = 32
DEC_SEQ = 64
PAST_LEN = 4096

CHUNK = 64
Q_BLOCK = 128
RET_HEADS = 4
RET_DK = D_MODEL // 16
RET_DV = 2 * RET_DK
DIFF_HEADS = 4
DIFF_DH = D_MODEL // 16
DIFF_DV = 2 * DIFF_DH
D_FF = ((8 * D_MODEL // 3 + 255) // 256) * 256
ROPE_THETA = 10000.0
EPS = 1e-6
N_SUB = 3
RET_QK_W = RET_HEADS * RET_DK
RET_V_W = RET_HEADS * RET_DV
DIFF_QK_W = DIFF_HEADS * 2 * DIFF_DH
DIFF_V_W = DIFF_HEADS * DIFF_DV
IN_SIZES = (RET_QK_W, RET_QK_W, RET_V_W, RET_V_W, DIFF_QK_W, DIFF_QK_W, DIFF_V_W)
D_IN = sum(IN_SIZES)
MIX_W = RET_V_W + DIFF_V_W

kernel_name = "hybrid_retention_diffattn_streaming_step"


def _rms_norm(x, g=None):
    xf = x.astype(jnp.float32)
    y = xf * lax.rsqrt(jnp.mean(xf * xf, axis=-1, keepdims=True) + EPS)
    if g is not None:
        y = y * g.astype(jnp.float32)
    return y.astype(x.dtype)


def _swiglu(h, w_up, w_down):
    a, b = jnp.split(h @ w_up, 2, axis=-1)
    return (jax.nn.silu(a) * b) @ w_down


def _rotary(x, pos, inv_freq, interleaved):
    ang = pos.astype(jnp.float32)[:, None] * inv_freq[None, :]
    shp = (pos.shape[0],) + (1,) * (x.ndim - 3) + (inv_freq.shape[0],)
    cos = jnp.cos(ang).reshape(shp).astype(x.dtype)
    sin = jnp.sin(ang).reshape(shp).astype(x.dtype)
    if interleaved:
        x1, x2 = x[..., 0::2], x[..., 1::2]
        return jnp.stack([x1 * cos - x2 * sin, x1 * sin + x2 * cos], axis=-1).reshape(x.shape)
    x1, x2 = jnp.split(x, 2, axis=-1)
    return jnp.concatenate([x1 * cos - x2 * sin, x2 * cos + x1 * sin], axis=-1)


def _retention(q, k, v, s0):
    B, L, H, Dk = q.shape
    Dv = v.shape[-1]
    C = min(CHUNK, L)
    N = L // C
    log_g = jnp.log1p(-jnp.exp2(-5.0 - jnp.arange(H, dtype=jnp.float32)))
    idx = jnp.arange(C, dtype=jnp.float32)
    rel = idx[:, None] - idx[None, :]
    dmask = jnp.where(rel >= 0, jnp.exp(log_g[:, None, None] * jnp.maximum(rel, 0.0)), 0.0)
    q_dec = jnp.exp(log_g[None, :] * (idx[:, None] + 1.0))
    k_dec = jnp.exp(log_g[None, :] * (C - 1.0 - idx[:, None]))
    c_dec = jnp.exp(log_g * C)

    def chunks(t):
        return t.astype(jnp.float32).reshape(B, N, C, H, t.shape[-1]).swapaxes(0, 1)

    def step(S, inp):
        qi, ki, vi = inp
        att = jnp.einsum('bihd,bjhd->bhij', qi, ki) * dmask
        o = (jnp.einsum('bhij,bjhe->bihe', att, vi)
             + jnp.einsum('bihd,bhde->bihe', qi * q_dec[:, :, None], S))
        S = c_dec[:, None, None] * S + jnp.einsum('bjhd,bjhe->bhde', ki * k_dec[:, :, None], vi)
        return S, o

    S, o = lax.scan(step, s0.astype(jnp.float32), (chunks(q), chunks(k), chunks(v)))
    o = o.swapaxes(0, 1).reshape(B, L, H, Dv)
    return o.astype(v.dtype), S.astype(s0.dtype)


def _diff_attention(q, k, v, lam, q_pos, k_pos):
    B, Lq, H, _, Dh = q.shape
    Dv = v.shape[-1]
    blk = min(Q_BLOCK, Lq)
    nb = Lq // blk
    scale = Dh ** -0.5
    qb = q.reshape(B, nb, blk, H, 2, Dh).swapaxes(0, 1)
    qcb = (q_pos // CHUNK).reshape(nb, blk)
    kc = k_pos // CHUNK

    def one_block(args):
        qi, qc = args
        s = jnp.einsum('bqhcd,bkhcd->bhcqk', qi, k, preferred_element_type=jnp.float32) * scale
        s = jnp.where(qc[:, None] >= kc[None, :], s, -jnp.inf)
        p = jax.nn.softmax(s, axis=-1)
        p = p[:, :, 0] - lam * p[:, :, 1]
        return jnp.einsum('bhqk,bkhd->bqhd', p.astype(v.dtype), v)

    o = lax.map(one_block, (qb, qcb))
    return o.swapaxes(0, 1).reshape(B, Lq, H, Dv)


def _token_mix(h, pos, lambda_init, w_in, w_out, g_qk, w_lambda, g_sub, kv_past, s0):
    B, L, _ = h.shape
    split_points = np.cumsum(IN_SIZES)[:-1].tolist()
    rq, rk, rv, rg, dq, dk, dv = jnp.split(h @ w_in, split_points, axis=-1)
    ret_freq = 1.0 / (ROPE_THETA ** jnp.linspace(0.0, 1.0, RET_DK // 2, dtype=jnp.float32))
    rq = _rotary(rq.reshape(B, L, RET_HEADS, RET_DK), pos, ret_freq, True)
    rk = _rotary(rk.reshape(B, L, RET_HEADS, RET_DK), pos, ret_freq, True) * (RET_DK ** -0.5)
    rv = rv.reshape(B, L, RET_HEADS, RET_DV)
    o_r, s_new = _retention(rq, rk, rv, s0)
    o_r = _rms_norm(o_r).reshape(B, L, RET_V_W) * jax.nn.silu(rg)
    rope_freq = 1.0 / (ROPE_THETA ** (jnp.arange(0, DIFF_DH, 2, dtype=jnp.float32) / DIFF_DH))
    dq = _rotary(_rms_norm(dq.reshape(B, L, DIFF_HEADS, 2, DIFF_DH), g_qk[0]), pos, rope_freq, False)
    dk = _rotary(_rms_norm(dk.reshape(B, L, DIFF_HEADS, 2, DIFF_DH), g_qk[1]), pos, rope_freq, False)
    dv = dv.reshape(B, L, DIFF_HEADS, DIFF_DV)
    wl = w_lambda.astype(jnp.float32)
    lam = jnp.exp(jnp.sum(wl[0] * wl[1])) - jnp.exp(jnp.sum(wl[2] * wl[3])) + lambda_init
    if kv_past is None:
        k_all, v_all, k_pos = dk, dv, pos
    else:
        k_all = jnp.concatenate([kv_past[0].astype(dk.dtype), dk], axis=1)
        v_all = jnp.concatenate([kv_past[1].astype(dv.dtype), dv], axis=1)
        k_pos = jnp.arange(k_all.shape[1])
    o_d = _diff_attention(dq, k_all, v_all, lam, pos, k_pos)
    o_d = _rms_norm(o_d, g_sub) * (1.0 - lambda_init)
    mix = jnp.concatenate([o_r, o_d.reshape(B, L, DIFF_V_W)], axis=-1) @ w_out
    return mix, dk, dv, s_new


def _layer(x, c, pos, l, w_ada, b_ada, g_norm, w_ff_up, w_ff_down, w_in, w_out, g_qk, w_lambda, g_sub,
           kv_past, s0):
    B = x.shape[0]
    mod = (jax.nn.silu(c) @ w_ada[l] + b_ada[l]).reshape(B, N_SUB, 3, D_MODEL)[:, :, :, None, :]

    def pre(t, i):
        return _rms_norm(t, g_norm[l, i]) * (1.0 + mod[:, i, 1]) + mod[:, i, 0]

    lambda_init = 0.8 - 0.6 * math.exp(-0.3 * l)
    x = x + 0.5 * mod[:, 0, 2] * _swiglu(pre(x, 0), w_ff_up[l, 0], w_ff_down[l, 0])
    mix, k_new, v_new, s_new = _token_mix(pre(x, 1), pos, lambda_init, w_in[l], w_out[l], g_qk[l],
                                          w_lambda[l], g_sub[l], kv_past, s0)
    x = x + mod[:, 1, 2] * mix
    x = x + 0.5 * mod[:, 2, 2] * _swiglu(pre(x, 2), w_ff_up[l, 1], w_ff_down[l, 1])
    return x, k_new, v_new, s_new


def setup_inputs(seed: int = 0) -> dict:
    key = jax.random.key(seed)
    ks = jax.random.split(key, 20)
    f32 = jnp.float32
    nrm = lambda k, shp, s: jax.random.normal(k, shp, f32) * s
    return {
        "x_prompt": nrm(ks[0], (BATCH, SEQ, D_MODEL), 1.0),
        "x_sample": nrm(ks[1], (DEC_BATCH, DEC_SEQ, D_MODEL), 1.0),
        "cache_k": nrm(ks[2], (DEPTH, DEC_BATCH, PAST_LEN, DIFF_HEADS, 2, DIFF_DH), 1.0),
        "cache_v": nrm(ks[3], (DEPTH, DEC_BATCH, PAST_LEN, DIFF_HEADS, DIFF_DV), 1.0),
        "state_ret": nrm(ks[4], (DEPTH, DEC_BATCH, RET_HEADS, RET_DK, RET_DV), 0.5),
        "c_prompt": nrm(ks[5], (BATCH, D_MODEL), 1.0),
        "c_sample": nrm(ks[6], (DEC_BATCH, D_MODEL), 1.0),
        "w_ada": nrm(ks[7], (DEPTH, D_MODEL, N_SUB * 3 * D_MODEL), 0.5 * D_MODEL ** -0.5),
        "b_ada": nrm(ks[8], (DEPTH, N_SUB * 3 * D_MODEL), 0.01),
        "g_norm": 1.0 + nrm(ks[9], (DEPTH, N_SUB, D_MODEL), 0.01),
        "w_ff_up": nrm(ks[10], (DEPTH, 2, D_MODEL, 2 * D_FF), D_MODEL ** -0.5),
        "w_ff_down": nrm(ks[11], (DEPTH, 2, D_FF, D_MODEL), D_FF ** -0.5),
        "w_in": nrm(ks[12], (DEPTH, D_MODEL, D_IN), D_MODEL ** -0.5),
        "w_out": nrm(ks[13], (DEPTH, MIX_W, D_MODEL), MIX_W ** -0.5),
        "g_qk": 1.0 + nrm(ks[14], (DEPTH, 2, DIFF_DH), 0.01),
        "w_lambda": nrm(ks[15], (DEPTH, 4, DIFF_DH), 0.1),
        "g_sub": 1.0 + nrm(ks[16], (DEPTH, DIFF_DV), 0.01),
    }


def reference(x_prompt, x_sample, cache_k, cache_v, state_ret, c_prompt, c_sample, w_ada, b_ada, g_norm,
              w_ff_up, w_ff_down, w_in, w_out, g_qk, w_lambda, g_sub):
    past = cache_k.shape[2]
    pos_p = jnp.arange(x_prompt.shape[1])
    pos_s = past + jnp.arange(x_sample.shape[1])
    yp, ys = x_prompt, x_sample
    kp, vp, sp, ksl, vsl, ssl = [], [], [], [], [], []
    for l in range(DEPTH):
        s0 = jnp.zeros((x_prompt.shape[0], RET_HEADS, RET_DK, RET_DV), jnp.float32)
        yp, k_new, v_new, s_new = _layer(yp, c_prompt, pos_p, l, w_ada, b_ada, g_norm, w_ff_up, w_ff_down,
                                         w_in, w_out, g_qk, w_lambda, g_sub, None, s0)
        kp.append(k_new); vp.append(v_new); sp.append(s_new)
        ys, k_new, v_new, s_new = _layer(ys, c_sample, pos_s, l, w_ada, b_ada, g_norm, w_ff_up, w_ff_down,
                                         w_in, w_out, g_qk, w_lambda, g_sub, (cache_k[l], cache_v[l]),
                                         state_ret[l])
        ksl.append(k_new); vsl.append(v_new); ssl.append(s_new)
    return (yp, ys, jnp.stack(kp), jnp.stack(vp), jnp.stack(sp), jnp.stack(ksl), jnp.stack(vsl), jnp.stack(ssl))
```

```python
import functools
import math

import numpy as np
import jax
import jax.numpy as jnp
from jax import lax
from jax.experimental import pallas as pl
from jax.experimental.pallas import tpu as pltpu

F32 = jnp.float32
BF16 = jnp.bfloat16

CHUNK = 64
RET_HEADS = 4
DIFF_HEADS = 4
ROPE_THETA = 10000.0
EPS = 1e-6
N_SUB = 3
LANES = 128
TOKEN_TILE = 512
FF_CHUNK = 256
ATTN_TQ = 512
ATTN_TK = 512
CACHE_TK = 1024
VMEM_LIMIT = 56 * 1024 * 1024


def _cparams(n_axes):
    return pltpu.CompilerParams(dimension_semantics=("arbitrary",) * n_axes,
                                vmem_limit_bytes=VMEM_LIMIT)


def _resident(block_shape, index_map):
    return pl.BlockSpec(block_shape, index_map, pipeline_mode=pl.Buffered(1))


def _rms(x):
    return x * lax.rsqrt(jnp.mean(x * x, axis=-1, keepdims=True) + EPS)


def _dot(a, b):
    return jnp.dot(a, b, preferred_element_type=F32)


def _dot_nt(a, b):
    return lax.dot_general(a, b, (((1,), (1,)), ((), ())), preferred_element_type=F32)


def _ada_kernel(c_ref, w_ref, b_ref, o_ref):
    c = c_ref[...]
    a = c * jax.nn.sigmoid(c)
    o_ref[...] = jnp.dot(a, w_ref[...], precision=lax.Precision.HIGHEST,
                         preferred_element_type=F32) + b_ref[...]


def _ada_mods(c_all, w_ada, b_ada):
    depth, d, n_out = w_ada.shape
    n_seq = c_all.shape[0]
    n_blk = n_out // d
    return pl.pallas_call(
        _ada_kernel,
        grid=(depth, n_blk),
        in_specs=[
            pl.BlockSpec((n_seq, d), lambda l, j: (0, 0)),
            pl.BlockSpec((None, d, d), lambda l, j: (l, 0, j)),
            pl.BlockSpec((None, None, 1, d), lambda l, j: (l, j, 0, 0)),
        ],
        out_specs=pl.BlockSpec((None, None, n_seq, d), lambda l, j: (l, j, 0, 0)),
        out_shape=jax.ShapeDtypeStruct((depth, n_blk, n_seq, d), F32),
        compiler_params=_cparams(2),
        name="ada_mods",
    )(c_all, w_ada, b_ada.reshape(depth, n_blk, 1, d))


def _pre(x3, g, shift, scale):
    return _rms(x3) * g * (1.0 + scale) + shift


def _ffn_apply(x3, g, shift, scale, gate, wup_ref, wdn_ref):
    n_seq, lt, d = x3.shape
    d_ff = wdn_ref.shape[0]
    hb = _pre(x3, g, shift, scale).reshape(n_seq * lt, d).astype(BF16)
    acc = jnp.zeros((n_seq * lt, d), F32)
    for c in range(d_ff // FF_CHUNK):
        lo = c * FF_CHUNK
        a = _dot(hb, wup_ref[:, lo:lo + FF_CHUNK])
        b = _dot(hb, wup_ref[:, d_ff + lo:d_ff + lo + FF_CHUNK])
        act = (a * jax.nn.sigmoid(a) * b).astype(BF16)
        acc = acc + _dot(act, wdn_ref[lo:lo + FF_CHUNK, :])
    return x3 + (0.5 * gate) * acc.reshape(n_seq, lt, d)


def _ffn_kernel(x_ref, shift_ref, scale_ref, gate_ref, g_ref, wup_ref, wdn_ref, o_ref):
    o_ref[...] = _ffn_apply(x_ref[...], g_ref[...], shift_ref[...], scale_ref[...], gate_ref[...],
                            wup_ref, wdn_ref)


def _token_tiles(n_b, seq):
    lt = min(seq, TOKEN_TILE)
    n_seq = max(1, min(n_b, TOKEN_TILE // lt))
    assert seq % lt == 0 and n_b % n_seq == 0 and lt % 8 == 0
    return n_seq, lt


def _mod_spec(idx, n_seq, d):
    return pl.BlockSpec((None, n_seq, 1, d), lambda b, i, _k=idx: (_k, b, 0, 0))


def _ffn1(x, mods, g_norm_l, wup, wdn):
    n_b, seq, d = x.shape
    n_seq, lt = _token_tiles(n_b, seq)
    x_spec = pl.BlockSpec((n_seq, lt, d), lambda b, i: (b, i, 0))
    return pl.pallas_call(
        _ffn_kernel,
        grid=(n_b // n_seq, seq // lt),
        in_specs=[x_spec, _mod_spec(0, n_seq, d), _mod_spec(1, n_seq, d), _mod_spec(2, n_seq, d),
                  pl.BlockSpec((None, 1, d), lambda b, i: (0, 0, 0)),
                  _resident(wup.shape, lambda b, i: (0, 0)),
                  _resident(wdn.shape, lambda b, i: (0, 0))],
        out_specs=x_spec,
        out_shape=jax.ShapeDtypeStruct(x.shape, F32),
        compiler_params=_cparams(2),
        name="ffn1",
    )(x, mods, mods, mods, g_norm_l, wup, wdn)


def _swap_pairs(x):
    w = x.shape[-1]
    lane = lax.broadcasted_iota(jnp.int32, (1, w), 1)
    return jnp.where((lane & 1) == 0, pltpu.roll(x, w - 1, 1), pltpu.roll(x, 1, 1))


def _swap_halves(x, half):
    w = x.shape[-1]
    lane = lax.broadcasted_iota(jnp.int32, (1, w), 1)
    return jnp.where((lane & (2 * half - 1)) < half, pltpu.roll(x, w - half, 1), pltpu.roll(x, half, 1))


def _mix_in_kernel(x_ref, shift_ref, scale_ref, g_ref, win_ref, gqk_ref, gmat_ref,
                   cosr_ref, sinr_ref, cosd_ref, sind_ref, dmask_ref, qdec_ref, kdec_ref, cdec_ref, s0_ref,
                   dk_ref, dv_ref, qa_ref, ka_ref, va_ref, or_ref, snew_ref, s_scr, *, rc, sizes):
    ret_qk, ret_v, diff_qk, diff_v = sizes
    n_seq, lt, d = x_ref.shape
    m = n_seq * lt
    dk_h = ret_qk // RET_HEADS
    dv_h = ret_v // RET_HEADS
    hb = _pre(x_ref[...], g_ref[...], shift_ref[...], scale_ref[...]).reshape(m, d).astype(BF16)

    def proj(lo, width):
        return _dot(hb, win_ref[:, lo:lo + width])

    off = 0
    rq = proj(off, ret_qk); off += ret_qk
    rk = proj(off, ret_qk); off += ret_qk
    rv = proj(off, ret_v); off += ret_v
    rg = proj(off, ret_v); off += ret_v
    dq = proj(off, diff_qk); off += diff_qk
    dk = proj(off, diff_qk); off += diff_qk
    dv = proj(off, diff_v)

    def rope(x, partner, cos_ref, sin_ref):
        w = x.shape[-1]
        y = x.reshape(n_seq, lt, w) * cos_ref[...] + partner.reshape(n_seq, lt, w) * sin_ref[...]
        return y.reshape(m, w)

    rq = rope(rq, _swap_pairs(rq), cosr_ref, sinr_ref)
    rk = rope(rk, _swap_pairs(rk), cosr_ref, sinr_ref) * (dk_h ** -0.5)

    sub = diff_qk // (2 * DIFF_HEADS)
    qk = jnp.concatenate([dq, dk], axis=0)
    ss = qk * qk
    hi = ss.astype(BF16)
    lo = (ss - hi.astype(F32)).astype(BF16)
    gmean = (_dot(hi, gmat_ref[...]) + _dot(lo, gmat_ref[...])) * (1.0 / sub)
    qk = qk * lax.rsqrt(gmean + EPS)
    dq = qk[:m] * gqk_ref[0]
    dk = qk[m:] * gqk_ref[1]
    dq = rope(dq, _swap_halves(dq, sub // 2), cosd_ref, sind_ref)
    dk = rope(dk, _swap_halves(dk, sub // 2), cosd_ref, sind_ref)

    dk_ref[...] = dk.reshape(n_seq, lt, diff_qk)
    dv_ref[...] = dv.reshape(n_seq, lt, diff_v)
    qa_ref[...] = (dq * (sub ** -0.5)).astype(BF16).reshape(n_seq, lt, diff_qk)
    ka_ref[...] = dk.astype(BF16).reshape(n_seq, lt, diff_qk)
    va_ref[...] = dv.astype(BF16).reshape(n_seq, lt, diff_v)

    @pl.when(pl.program_id(1) == 0)
    def _():
        s_scr[...] = jnp.zeros(s_scr.shape, F32)
        for s in range(n_seq):
            for h in range(RET_HEADS):
                s_scr[s, h * dk_h:(h + 1) * dk_h, h * dv_h:(h + 1) * dv_h] = s0_ref[s, h]

    lane_q = lax.broadcasted_iota(jnp.int32, (1, ret_qk), 1)
    gate = rg * jax.nn.sigmoid(rg)
    rv_b = rv.astype(BF16)
    for s in range(n_seq):
        for c in range(lt // rc):
            r0 = s * lt + c * rc
            q = rq[r0:r0 + rc]
            k = rk[r0:r0 + rc]
            v = rv_b[r0:r0 + rc]
            q_heads = jnp.concatenate(
                [jnp.where((lane_q >= h * dk_h) & (lane_q < (h + 1) * dk_h), q, 0.0) for h in range(RET_HEADS)],
                axis=0).astype(BF16)
            att = (_dot_nt(q_heads, k.astype(BF16)) * dmask_ref[...]).astype(BF16)
            inter = _dot((q * qdec_ref[...]).astype(BF16), s_scr[s].astype(BF16))
            kd_t = jnp.transpose(k * kdec_ref[...]).astype(BF16)
            for h in range(RET_HEADS):
                vh = v[:, h * dv_h:(h + 1) * dv_h]
                o = _dot(att[h * rc:(h + 1) * rc], vh) + inter[:, h * dv_h:(h + 1) * dv_h]
                o = _rms(o) * gate[r0:r0 + rc, h * dv_h:(h + 1) * dv_h]
                or_ref[s, c * rc:(c + 1) * rc, h * dv_h:(h + 1) * dv_h] = o.astype(BF16)
                upd = _dot(kd_t[h * dk_h:(h + 1) * dk_h], vh)
                blk = (slice(h * dk_h, (h + 1) * dk_h), slice(h * dv_h, (h + 1) * dv_h))
                s_scr[(s,) + blk] = cdec_ref[h] * s_scr[(s,) + blk] + upd
    for s in range(n_seq):
        for h in range(RET_HEADS):
            snew_ref[s, h] = s_scr[s, h * dk_h:(h + 1) * dk_h, h * dv_h:(h + 1) * dv_h]


def _retention_tables(rc, dk_h, dv_h):
    n_h = RET_HEADS
    log_g = jnp.log1p(-jnp.exp2(-5.0 - jnp.arange(n_h, dtype=F32)))
    idx = jnp.arange(rc, dtype=F32)
    rel = idx[:, None] - idx[None, :]
    dmask = jnp.where(rel >= 0, jnp.exp(log_g[:, None, None] * jnp.maximum(rel, 0.0)), 0.0)
    q_dec = jnp.exp(log_g[None, :] * (idx[:, None] + 1.0))
    k_dec = jnp.exp(log_g[None, :] * (rc - 1.0 - idx[:, None]))
    c_dec = jnp.exp(log_g * rc)
    return (dmask.reshape(n_h * rc, rc), jnp.repeat(q_dec, dk_h, axis=1), jnp.repeat(k_dec, dk_h, axis=1),
            jnp.broadcast_to(c_dec[:, None, None], (n_h, dk_h, dv_h)))


def _rope_tables(pos, dk_h, sub):
    posf = pos.astype(F32)[:, None]
    ret_freq = 1.0 / (ROPE_THETA ** jnp.linspace(0.0, 1.0, dk_h // 2, dtype=F32))
    ang = posf * ret_freq[None, :]
    sign = jnp.tile(jnp.array([-1.0, 1.0], F32), dk_h // 2)
    cos_r = jnp.tile(jnp.repeat(jnp.cos(ang), 2, axis=1), (1, RET_HEADS))
    sin_r = jnp.tile(jnp.repeat(jnp.sin(ang), 2, axis=1) * sign, (1, RET_HEADS))
    rope_freq = 1.0 / (ROPE_THETA ** (jnp.arange(0, sub, 2, dtype=F32) / sub))
    ang = posf * rope_freq[None, :]
    sign = jnp.concatenate([-jnp.ones((sub // 2,), F32), jnp.ones((sub // 2,), F32)])
    cos_d = jnp.tile(jnp.tile(jnp.cos(ang), (1, 2)), (1, 2 * DIFF_HEADS))
    sin_d = jnp.tile(jnp.tile(jnp.sin(ang), (1, 2)) * sign, (1, 2 * DIFF_HEADS))
    return cos_r, sin_r, cos_d, sin_d


def _mix_in(x, mods, g_norm_l, win, gqk_l, gmat, rope, ret_tabs, s0, sizes, rc):
    n_b, seq, d = x.shape
    ret_qk, ret_v, diff_qk, diff_v = sizes
    n_seq, lt = _token_tiles(n_b, seq)
    assert lt % rc == 0
    dk_h, dv_h = ret_qk // RET_HEADS, ret_v // RET_HEADS

    def tok(width):
        return pl.BlockSpec((n_seq, lt, width), lambda b, i: (b, i, 0))

    def full(a):
        return _resident(a.shape, lambda b, i, _n=a.ndim: (0,) * _n)

    def pos_tab(a):
        return pl.BlockSpec((lt, a.shape[1]), lambda b, i: (i, 0))

    state_spec = pl.BlockSpec((n_seq, RET_HEADS, dk_h, dv_h), lambda b, i: (b, 0, 0, 0))
    outs = pl.pallas_call(
        functools.partial(_mix_in_kernel, rc=rc, sizes=sizes),
        grid=(n_b // n_seq, seq // lt),
        in_specs=[tok(d), _mod_spec(3, n_seq, d), _mod_spec(4, n_seq, d),
                  pl.BlockSpec((None, 1, d), lambda b, i: (1, 0, 0)),
                  full(win), full(gqk_l), full(gmat),
                  pos_tab(rope[0]), pos_tab(rope[1]), pos_tab(rope[2]), pos_tab(rope[3]),
                  full(ret_tabs[0]), full(ret_tabs[1]), full(ret_tabs[2]), full(ret_tabs[3]),
                  state_spec],
        out_specs=[tok(diff_qk), tok(diff_v), tok(diff_qk), tok(diff_qk), tok(diff_v), tok(ret_v), state_spec],
        out_shape=[jax.ShapeDtypeStruct((n_b, seq, diff_qk), F32),
                   jax.ShapeDtypeStruct((n_b, seq, diff_v), F32),
                   jax.ShapeDtypeStruct((n_b, seq, diff_qk), BF16),
                   jax.ShapeDtypeStruct((n_b, seq, diff_qk), BF16),
                   jax.ShapeDtypeStruct((n_b, seq, diff_v), BF16),
                   jax.ShapeDtypeStruct((n_b, seq, ret_v), BF16),
                   jax.ShapeDtypeStruct(s0.shape, F32)],
        scratch_shapes=[pltpu.VMEM((n_seq, ret_qk, ret_v), F32)],
        compiler_params=_cparams(2),
        name="mix_in",
    )(x, mods, mods, g_norm_l, win, gqk_l, gmat, *rope, *ret_tabs, s0)
    return outs


def _stack_subheads(q):
    w = q.shape[-1]
    lane = lax.broadcasted_iota(jnp.int32, (1, w), 1)
    zero = jnp.zeros_like(q)
    return jnp.concatenate([jnp.where(lane < w // 2, q, zero), jnp.where(lane >= w // 2, q, zero)], axis=0)


def _softmax_step(s, v, m_ref, l_ref, acc_ref):
    m_prev = m_ref[...]
    m_new = jnp.maximum(m_prev, jnp.max(s, axis=-1, keepdims=True))
    alpha = jnp.exp(m_prev - m_new)
    p = jnp.exp(s - m_new)
    l_ref[...] = alpha * l_ref[...] + jnp.sum(p, axis=-1, keepdims=True)
    acc_ref[...] = alpha * acc_ref[...] + _dot(p.astype(BF16), v)
    m_ref[...] = m_new


def _lambda(wl_ref, lam_init):
    wl = wl_ref[...]
    a = jnp.sum(wl[0:1] * wl[1:2], axis=-1, keepdims=True)
    b = jnp.sum(wl[2:3] * wl[3:4], axis=-1, keepdims=True)
    return jnp.exp(a) - jnp.exp(b) + lam_init


def _diff_combine(acc, l, lam, gsub, lam_init):
    rows = acc.shape[0] // 2
    o = acc[:rows] / l[:rows] - lam * (acc[rows:] / l[rows:])
    return _rms(o) * gsub * (1.0 - lam_init)


def _attn_prompt_kernel(wl_ref, gsub_ref, q_ref, k_ref, v_ref, o_ref, m_ref, l_ref, acc_ref, *, tk, lam_init):
    tq = q_ref.shape[0]
    i = pl.program_id(2)
    qs = _stack_subheads(q_ref[...])
    m_ref[...] = jnp.full(m_ref.shape, -jnp.inf, F32)
    l_ref[...] = jnp.zeros(l_ref.shape, F32)
    acc_ref[...] = jnp.zeros(acc_ref.shape, F32)
    per_q = tq // tk

    def full_tile(j, carry):
        r0 = pl.multiple_of(j * tk, tk)
        _softmax_step(_dot_nt(qs, k_ref[pl.ds(r0, tk), :]), v_ref[pl.ds(r0, tk), :], m_ref, l_ref, acc_ref)
        return carry

    lax.fori_loop(0, i * per_q, full_tile, 0)
    row = lax.broadcasted_iota(jnp.int32, (2 * tq, tk), 0)
    q_chunk = jnp.where(row >= tq, row - tq, row) // CHUNK
    k_chunk = lax.broadcasted_iota(jnp.int32, (2 * tq, tk), 1) // CHUNK
    for r in range(per_q):
        r0 = pl.multiple_of((i * per_q + r) * tk, tk)
        s = _dot_nt(qs, k_ref[pl.ds(r0, tk), :])
        s = jnp.where(q_chunk >= k_chunk + r * (tk // CHUNK), s, -jnp.inf)
        _softmax_step(s, v_ref[pl.ds(r0, tk), :], m_ref, l_ref, acc_ref)
    o = _diff_combine(acc_ref[...], l_ref[...], _lambda(wl_ref, lam_init), gsub_ref[...], lam_init)
    o_ref[...] = o.astype(BF16)


def _attn_prompt(q, k, v, wl_l, gsub_l, lam_init):
    n_b, seq, width = q.shape
    hw = width // DIFF_HEADS
    tq = min(seq, ATTN_TQ)
    tk = min(tq, ATTN_TK)
    assert seq % tq == 0 and tq % tk == 0 and tk % CHUNK == 0
    kv_spec = pl.BlockSpec((None, seq, hw), lambda b, h, i: (b, 0, h))
    q_spec = pl.BlockSpec((None, tq, hw), lambda b, h, i: (b, i, h))
    return pl.pallas_call(
        functools.partial(_attn_prompt_kernel, tk=tk, lam_init=lam_init),
        grid=(n_b, DIFF_HEADS, seq // tq),
        in_specs=[pl.BlockSpec(wl_l.shape, lambda b, h, i: (0, 0)),
                  pl.BlockSpec(gsub_l.shape, lambda b, h, i: (0, 0)),
                  q_spec, kv_spec, kv_spec],
        out_specs=q_spec,
        out_shape=jax.ShapeDtypeStruct((n_b, seq, v.shape[-1]), BF16),
        scratch_shapes=[pltpu.VMEM((2 * tq, 1), F32), pltpu.VMEM((2 * tq, 1), F32),
                        pltpu.VMEM((2 * tq, v.shape[-1] // DIFF_HEADS), F32)],
        compiler_params=_cparams(3),
        name="attn_prompt",
    )(wl_l, gsub_l, q, k, v)


def _attn_sample_kernel(wl_ref, gsub_ref, q_ref, kc_ref, vc_ref, kn_ref, vn_ref, o_ref,
                        m_ref, l_ref, acc_ref, *, lam_init):
    j = pl.program_id(1)
    hw = q_ref.shape[-1] // DIFF_HEADS
    hv = vc_ref.shape[-1] // DIFF_HEADS

    @pl.when(j == 0)
    def _():
        m_ref[...] = jnp.full(m_ref.shape, -jnp.inf, F32)
        l_ref[...] = jnp.zeros(l_ref.shape, F32)
        acc_ref[...] = jnp.zeros(acc_ref.shape, F32)

    def head_q(h):
        return _stack_subheads(q_ref[:, h * hw:(h + 1) * hw])

    for h in range(DIFF_HEADS):
        k = kc_ref[:, h * hw:(h + 1) * hw].astype(BF16)
        v = vc_ref[:, h * hv:(h + 1) * hv].astype(BF16)
        _softmax_step(_dot_nt(head_q(h), k), v, m_ref.at[h], l_ref.at[h], acc_ref.at[h])

    @pl.when(j == pl.num_programs(1) - 1)
    def _():
        lam = _lambda(wl_ref, lam_init)
        for h in range(DIFF_HEADS):
            _softmax_step(_dot_nt(head_q(h), kn_ref[:, h * hw:(h + 1) * hw]), vn_ref[:, h * hv:(h + 1) * hv],
                          m_ref.at[h], l_ref.at[h], acc_ref.at[h])
            o = _diff_combine(acc_ref[h], l_ref[h], lam, gsub_ref[...], lam_init)
            o_ref[:, h * hv:(h + 1) * hv] = o.astype(BF16)


def _attn_sample(q, k_new, v_new, cache_k_l, cache_v_l, wl_l, gsub_l, lam_init):
    n_b, seq, width = q.shape
    past = cache_k_l.shape[1]
    assert past % CHUNK == 0 and seq <= CHUNK
    tk = min(past, CACHE_TK)
    assert past % tk == 0
    vw = v_new.shape[-1]
    new_q = pl.BlockSpec((None, seq, width), lambda b, j: (b, 0, 0))
    new_v = pl.BlockSpec((None, seq, vw), lambda b, j: (b, 0, 0))
    return pl.pallas_call(
        functools.partial(_attn_sample_kernel, lam_init=lam_init),
        grid=(n_b, past // tk),
        in_specs=[pl.BlockSpec(wl_l.shape, lambda b, j: (0, 0)),
                  pl.BlockSpec(gsub_l.shape, lambda b, j: (0, 0)),
                  new_q,
                  pl.BlockSpec((None, tk, width), lambda b, j: (b, j, 0)),
                  pl.BlockSpec((None, tk, vw), lambda b, j: (b, j, 0)),
                  new_q, new_v],
        out_specs=new_v,
        out_shape=jax.ShapeDtypeStruct((n_b, seq, vw), BF16),
        scratch_shapes=[pltpu.VMEM((DIFF_HEADS, 2 * seq, 1), F32), pltpu.VMEM((DIFF_HEADS, 2 * seq, 1), F32),
                        pltpu.VMEM((DIFF_HEADS, 2 * seq, vw // DIFF_HEADS), F32)],
        compiler_params=_cparams(2),
        name="attn_sample",
    )(wl_l, gsub_l, q, cache_k_l, cache_v_l, k_new, v_new)


def _mix_out_kernel(x_ref, or_ref, od_ref, gate1_ref, shift_ref, scale_ref, gate2_ref, g_ref,
                    wout_ref, wup_ref, wdn_ref, o_ref):
    n_seq, lt, d = x_ref.shape
    m = n_seq * lt
    rw = or_ref.shape[-1]
    mix = (_dot(or_ref[...].reshape(m, rw), wout_ref[:rw, :])
           + _dot(od_ref[...].reshape(m, od_ref.shape[-1]), wout_ref[rw:, :]))
    x3 = x_ref[...] + gate1_ref[...] * mix.reshape(n_seq, lt, d)
    o_ref[...] = _ffn_apply(x3, g_ref[...], shift_ref[...], scale_ref[...], gate2_ref[...], wup_ref, wdn_ref)


def _mix_out(x, o_r, o_d, mods, g_norm_l, wout, wup, wdn):
    n_b, seq, d = x.shape
    n_seq, lt = _token_tiles(n_b, seq)

    def tok(width):
        return pl.BlockSpec((n_seq, lt, width), lambda b, i: (b, i, 0))

    return pl.pallas_call(
        _mix_out_kernel,
        grid=(n_b // n_seq, seq // lt),
        in_specs=[tok(d), tok(o_r.shape[-1]), tok(o_d.shape[-1]),
                  _mod_spec(5, n_seq, d), _mod_spec(6, n_seq, d), _mod_spec(7, n_seq, d), _mod_spec(8, n_seq, d),
                  pl.BlockSpec((None, 1, d), lambda b, i: (2, 0, 0)),
                  _resident(wout.shape, lambda b, i: (0, 0)),
                  _resident(wup.shape, lambda b, i: (0, 0)),
                  _resident(wdn.shape, lambda b, i: (0, 0))],
        out_specs=tok(d),
        out_shape=jax.ShapeDtypeStruct(x.shape, F32),
        compiler_params=_cparams(2),
        name="mix_out",
    )(x, o_r, o_d, mods, mods, mods, mods, g_norm_l, wout, wup, wdn)


def kernel(x_prompt, x_sample, cache_k, cache_v, state_ret, c_prompt, c_sample, w_ada, b_ada, g_norm,
           w_ff_up, w_ff_down, w_in, w_out, g_qk, w_lambda, g_sub):
    depth = w_in.shape[0]
    n_p, seq_p, d = x_prompt.shape
    n_s, seq_s, _ = x_sample.shape
    past = cache_k.shape[2]
    dk_h, dv_h = state_ret.shape[-2], state_ret.shape[-1]
    sub = cache_k.shape[-1]
    ret_qk, ret_v = RET_HEADS * dk_h, RET_HEADS * dv_h
    diff_qk, diff_v = DIFF_HEADS * 2 * sub, DIFF_HEADS * cache_v.shape[-1]
    sizes = (ret_qk, ret_v, diff_qk, diff_v)

    mods = _ada_mods(jnp.concatenate([c_prompt, c_sample], axis=0), w_ada, b_ada)
    mods = mods[:, :, :, None, :]
    mods_p, mods_s = mods[:, :, :n_p], mods[:, :, n_p:]

    rope_p = _rope_tables(jnp.arange(seq_p), dk_h, sub)
    rope_s = _rope_tables(past + jnp.arange(seq_s), dk_h, sub)
    rc_p, rc_s = min(CHUNK, seq_p), min(CHUNK, seq_s)
    tabs_p = _retention_tables(rc_p, dk_h, dv_h)
    tabs_s = _retention_tables(rc_s, dk_h, dv_h)
    group = np.arange(diff_qk) // sub
    gmat = jnp.asarray(group[:, None] == group[None, :], BF16)
    s0_p = jnp.zeros((n_p, RET_HEADS, dk_h, dv_h), F32)
    cache_k2 = cache_k.reshape(depth, n_s, past, diff_qk)
    cache_v2 = cache_v.reshape(depth, n_s, past, diff_v)

    yp, ys = x_prompt, x_sample
    kp, vp, sp, ksl, vsl, ssl = [], [], [], [], [], []
    for l in range(depth):
        lam_init = 0.8 - 0.6 * math.exp(-0.3 * l)
        wup = [w_ff_up[l, i].astype(BF16) for i in range(2)]
        wdn = [w_ff_down[l, i].astype(BF16) for i in range(2)]
        win = w_in[l].astype(BF16)
        wout = w_out[l].astype(BF16)
        g_norm_l = g_norm[l][:, None, :]
        gqk_l = jnp.tile(g_qk[l], (1, diff_qk // sub))[:, None, :]
        wl_l = w_lambda[l].astype(F32)
        gsub_l = g_sub[l][None, :]

        yp = _ffn1(yp, mods_p[l], g_norm_l, wup[0], wdn[0])
        dk, dv, qa, ka, va, o_r, s_new = _mix_in(yp, mods_p[l], g_norm_l, win, gqk_l, gmat, rope_p, tabs_p,
                                                  s0_p, sizes, rc_p)
        o_d = _attn_prompt(qa, ka, va, wl_l, gsub_l, lam_init)
        yp = _mix_out(yp, o_r, o_d, mods_p[l], g_norm_l, wout, wup[1], wdn[1])
        kp.append(dk); vp.append(dv); sp.append(s_new)

        ys = _ffn1(ys, mods_s[l], g_norm_l, wup[0], wdn[0])
        dk, dv, qa, ka, va, o_r, s_new = _mix_in(ys, mods_s[l], g_norm_l, win, gqk_l, gmat, rope_s, tabs_s,
                                                  state_ret[l], sizes, rc_s)
        o_d = _attn_sample(qa, ka, va, cache_k2[l], cache_v2[l], wl_l, gsub_l, lam_init)
        ys = _mix_out(ys, o_r, o_d, mods_s[l], g_norm_l, wout, wup[1], wdn[1])
        ksl.append(dk); vsl.append(dv); ssl.append(s_new)

    def heads(t, n_b, seq, last):
        return jnp.stack(t).reshape((depth, n_b, seq, DIFF_HEADS) + last)

    return (yp, ys,
            heads(kp, n_p, seq_p, (2, sub)), heads(vp, n_p, seq_p, (diff_v // DIFF_HEADS,)), jnp.stack(sp),
            heads(ksl, n_s, seq_s, (2, sub)), heads(vsl, n_s, seq_s, (diff_v // DIFF_HEADS,)), jnp.stack(ssl))
```

```python
import functools
import math

import numpy as np
import jax
import jax.numpy as jnp
from jax import lax
from jax.experimental import pallas as pl
from jax.experimental.pallas import tpu as pltpu

F32 = jnp.float32
BF16 = jnp.bfloat16

CHUNK = 64
RET_HEADS = 4
DIFF_HEADS = 4
ROPE_THETA = 10000.0
EPS = 1e-6
TOKEN_TILE = 512
FF_CHUNK = 256
CACHE_TK = 2048
VMEM_LIMIT = 56 * 1024 * 1024


def _cparams(n_axes):
    return pltpu.CompilerParams(dimension_semantics=("arbitrary",) * n_axes,
                                vmem_limit_bytes=VMEM_LIMIT)


def _resident(block_shape, index_map):
    return pl.BlockSpec(block_shape, index_map, pipeline_mode=pl.Buffered(1))


def _rms(x):
    return x * lax.rsqrt(jnp.mean(x * x, axis=-1, keepdims=True) + EPS)


def _dot(a, b):
    return jnp.dot(a, b, preferred_element_type=F32)


def _dot_nt(a, b):
    return lax.dot_general(a, b, (((1,), (1,)), ((), ())), preferred_element_type=F32)


def _split_bf16(x):
    hi = x.astype(BF16)
    return hi, (x - hi.astype(F32)).astype(BF16)


def _ada_kernel(c_ref, w_ref, b_ref, o_ref):
    n = c_ref.shape[0]
    c = c_ref[...]
    a_hi, a_lo = _split_bf16(c * jax.nn.sigmoid(c))
    w_hi, w_lo = _split_bf16(w_ref[...])
    a_both = jnp.concatenate([a_hi.astype(F32), a_lo.astype(F32)], axis=0).astype(BF16)
    first = _dot(a_both, w_hi)
    o_ref[...] = first[:n] + first[n:] + _dot(a_hi, w_lo) + b_ref[...]


def _ada_mods(c_all, w_ada, b_ada):
    depth, d, n_out = w_ada.shape
    n_seq = c_all.shape[0]
    n_blk = n_out // d
    return pl.pallas_call(
        _ada_kernel,
        grid=(depth, n_blk),
        in_specs=[
            pl.BlockSpec((n_seq, d), lambda l, j: (0, 0)),
            pl.BlockSpec((None, d, d), lambda l, j: (l, 0, j)),
            pl.BlockSpec((None, None, 1, d), lambda l, j: (l, j, 0, 0)),
        ],
        out_specs=pl.BlockSpec((None, None, n_seq, d), lambda l, j: (l, j, 0, 0)),
        out_shape=jax.ShapeDtypeStruct((depth, n_blk, n_seq, d), F32),
        compiler_params=_cparams(2),
        name="ada_mods",
    )(c_all, w_ada, b_ada.reshape(depth, n_blk, 1, d))


def _pre(x3, g, shift, scale):
    return _rms(x3) * g * (1.0 + scale) + shift


def _ffn_apply(x3, g, shift, scale, gate, wup_ref, wdn_ref):
    n_seq, lt, d = x3.shape
    d_ff = wdn_ref.shape[0]
    hb = _pre(x3, g, shift, scale).reshape(n_seq * lt, d).astype(BF16)
    acc = jnp.zeros((n_seq * lt, d), F32)
    for c in range(d_ff // FF_CHUNK):
        lo = c * FF_CHUNK
        a = _dot(hb, wup_ref[:, lo:lo + FF_CHUNK])
        b = _dot(hb, wup_ref[:, d_ff + lo:d_ff + lo + FF_CHUNK])
        act = (a * jax.nn.sigmoid(a) * b).astype(BF16)
        acc = acc + _dot(act, wdn_ref[lo:lo + FF_CHUNK, :])
    return x3 + (0.5 * gate) * acc.reshape(n_seq, lt, d)


def _ffn_kernel(x_ref, shift_ref, scale_ref, gate_ref, g_ref, wup_ref, wdn_ref, o_ref):
    o_ref[...] = _ffn_apply(x_ref[...], g_ref[...], shift_ref[...], scale_ref[...], gate_ref[...],
                            wup_ref, wdn_ref)


def _token_tiles(n_b, seq):
    lt = min(seq, TOKEN_TILE)
    n_seq = max(1, min(n_b, TOKEN_TILE // lt))
    assert seq % lt == 0 and n_b % n_seq == 0 and lt % 8 == 0
    return n_seq, lt


def _mod_spec(layer, idx, n_seq, d):
    return pl.BlockSpec((None, None, n_seq, 1, d), lambda b, i: (layer, idx, b, 0, 0))


def _layer_weight(w, *lead):
    n_lead = len(lead)
    return _resident((None,) * n_lead + w.shape[n_lead:], lambda b, i: lead + (0,) * (w.ndim - n_lead))


def _ffn1(x, mods, g_norm, wup, wdn, layer):
    n_b, seq, d = x.shape
    n_seq, lt = _token_tiles(n_b, seq)
    x_spec = pl.BlockSpec((n_seq, lt, d), lambda b, i: (b, i, 0))
    return pl.pallas_call(
        _ffn_kernel,
        grid=(n_b // n_seq, seq // lt),
        in_specs=[x_spec, _mod_spec(layer, 0, n_seq, d), _mod_spec(layer, 1, n_seq, d),
                  _mod_spec(layer, 2, n_seq, d),
                  pl.BlockSpec((None, None, 1, d), lambda b, i: (layer, 0, 0, 0)),
                  _layer_weight(wup, layer, 0), _layer_weight(wdn, layer, 0)],
        out_specs=x_spec,
        out_shape=jax.ShapeDtypeStruct(x.shape, F32),
        compiler_params=_cparams(2),
        name="ffn1",
    )(x, mods, mods, mods, g_norm, wup, wdn)


def _swap_pairs(x):
    w = x.shape[-1]
    lane = lax.broadcasted_iota(jnp.int32, (1, w), 1)
    return jnp.where((lane & 1) == 0, pltpu.roll(x, w - 1, 1), pltpu.roll(x, 1, 1))


def _swap_halves(x, half):
    w = x.shape[-1]
    lane = lax.broadcasted_iota(jnp.int32, (1, w), 1)
    return jnp.where((lane & (2 * half - 1)) < half, pltpu.roll(x, w - half, 1), pltpu.roll(x, half, 1))


def _mix_in_kernel(x_ref, shift_ref, scale_ref, g_ref, win_ref, gqk_ref, gmat_ref,
                   cosr_ref, sinr_ref, cosd_ref, sind_ref, dmask_ref, qdec_ref, kdec_ref, cdec_ref, s0_ref,
                   dk_ref, dv_ref, qa_ref, ka_ref, va_ref, or_ref, snew_ref, s_scr, *, rc, sizes, prompt):
    ret_qk, ret_v, diff_qk, diff_v = sizes
    n_seq, lt, d = x_ref.shape
    m = n_seq * lt
    dk_h = ret_qk // RET_HEADS
    dv_h = ret_v // RET_HEADS
    hv = diff_v // DIFF_HEADS
    hb = _pre(x_ref[...], g_ref[...], shift_ref[...], scale_ref[...]).reshape(m, d).astype(BF16)

    def proj(lo, width):
        return _dot(hb, win_ref[:, lo:lo + width])

    off = 0
    rq = proj(off, ret_qk); off += ret_qk
    rk = proj(off, ret_qk); off += ret_qk
    rv = proj(off, ret_v); off += ret_v
    rg = proj(off, ret_v); off += ret_v
    dq = proj(off, diff_qk); off += diff_qk
    dk = proj(off, diff_qk); off += diff_qk
    dv = proj(off, diff_v)

    def rope(x, partner, cos_ref, sin_ref):
        w = x.shape[-1]
        y = x.reshape(n_seq, lt, w) * cos_ref[...] + partner.reshape(n_seq, lt, w) * sin_ref[...]
        return y.reshape(m, w)

    rq = rope(rq, _swap_pairs(rq), cosr_ref, sinr_ref)
    rk = rope(rk, _swap_pairs(rk), cosr_ref, sinr_ref) * (dk_h ** -0.5)

    sub = diff_qk // (2 * DIFF_HEADS)
    qk = jnp.concatenate([dq, dk], axis=0)
    hi, lo = _split_bf16(qk * qk)
    gmean = (_dot(hi, gmat_ref[...]) + _dot(lo, gmat_ref[...])) * (1.0 / sub)
    qk = qk * lax.rsqrt(gmean + EPS)
    dq = qk[:m] * gqk_ref[0]
    dk = qk[m:] * gqk_ref[1]
    dq = rope(dq, _swap_halves(dq, sub // 2), cosd_ref, sind_ref) * (sub ** -0.5)
    dk = rope(dk, _swap_halves(dk, sub // 2), cosd_ref, sind_ref)

    for s in range(n_seq):
        rows = slice(s * lt, (s + 1) * lt)
        for h in range(DIFF_HEADS):
            dv_ref[s, pl.ds(h, lt, stride=DIFF_HEADS), :] = dv[rows, h * hv:(h + 1) * hv]
        if prompt:
            dk_t = jnp.transpose(dk[rows])
            dk_ref[s] = dk_t
            ka_ref[s] = dk[rows].astype(BF16)
            qa_ref[s] = jnp.transpose(dq[rows]).astype(BF16)
            va_ref[s] = jnp.transpose(dv[rows]).astype(BF16)
        else:
            dk_ref[s] = dk[rows]
            ka_ref[s] = dk[rows].astype(BF16)
            qa_ref[s] = dq[rows].astype(BF16)
            va_ref[s] = dv[rows].astype(BF16)

    @pl.when(pl.program_id(1) == 0)
    def _():
        s_scr[...] = jnp.zeros(s_scr.shape, F32)
        for s in range(n_seq):
            for h in range(RET_HEADS):
                s_scr[s, h * dk_h:(h + 1) * dk_h, h * dv_h:(h + 1) * dv_h] = s0_ref[s, h]

    lane_q = lax.broadcasted_iota(jnp.int32, (1, ret_qk), 1)
    gate = rg * jax.nn.sigmoid(rg)
    rv_b = rv.astype(BF16)
    for s in range(n_seq):
        for c in range(lt // rc):
            r0 = s * lt + c * rc
            q = rq[r0:r0 + rc]
            k = rk[r0:r0 + rc]
            v = rv_b[r0:r0 + rc]
            q_heads = jnp.concatenate(
                [jnp.where((lane_q >= h * dk_h) & (lane_q < (h + 1) * dk_h), q, 0.0) for h in range(RET_HEADS)],
                axis=0).astype(BF16)
            att = (_dot_nt(q_heads, k.astype(BF16)) * dmask_ref[...]).astype(BF16)
            inter = _dot((q * qdec_ref[...]).astype(BF16), s_scr[s].astype(BF16))
            kd_t = jnp.transpose(k * kdec_ref[...]).astype(BF16)
            for h in range(RET_HEADS):
                vh = v[:, h * dv_h:(h + 1) * dv_h]
                o = _dot(att[h * rc:(h + 1) * rc], vh) + inter[:, h * dv_h:(h + 1) * dv_h]
                o = _rms(o) * gate[r0:r0 + rc, h * dv_h:(h + 1) * dv_h]
                or_ref[s, c * rc:(c + 1) * rc, h * dv_h:(h + 1) * dv_h] = o.astype(BF16)
                upd = _dot(kd_t[h * dk_h:(h + 1) * dk_h], vh)
                blk = (slice(h * dk_h, (h + 1) * dk_h), slice(h * dv_h, (h + 1) * dv_h))
                s_scr[(s,) + blk] = cdec_ref[h] * s_scr[(s,) + blk] + upd
    for s in range(n_seq):
        for h in range(RET_HEADS):
            snew_ref[s, h] = s_scr[s, h * dk_h:(h + 1) * dk_h, h * dv_h:(h + 1) * dv_h]


def _retention_tables(rc, dk_h, dv_h):
    n_h = RET_HEADS
    log_g = jnp.log1p(-jnp.exp2(-5.0 - jnp.arange(n_h, dtype=F32)))
    idx = jnp.arange(rc, dtype=F32)
    rel = idx[:, None] - idx[None, :]
    dmask = jnp.where(rel >= 0, jnp.exp(log_g[:, None, None] * jnp.maximum(rel, 0.0)), 0.0)
    q_dec = jnp.exp(log_g[None, :] * (idx[:, None] + 1.0))
    k_dec = jnp.exp(log_g[None, :] * (rc - 1.0 - idx[:, None]))
    c_dec = jnp.exp(log_g * rc)
    return (dmask.reshape(n_h * rc, rc), jnp.repeat(q_dec, dk_h, axis=1), jnp.repeat(k_dec, dk_h, axis=1),
            jnp.broadcast_to(c_dec[:, None, None], (n_h, dk_h, dv_h)))


def _rope_tables(pos, dk_h, sub):
    posf = pos.astype(F32)[:, None]
    ret_freq = 1.0 / (ROPE_THETA ** jnp.linspace(0.0, 1.0, dk_h // 2, dtype=F32))
    ang = posf * ret_freq[None, :]
    sign = jnp.tile(jnp.array([-1.0, 1.0], F32), dk_h // 2)
    cos_r = jnp.tile(jnp.repeat(jnp.cos(ang), 2, axis=1), (1, RET_HEADS))
    sin_r = jnp.tile(jnp.repeat(jnp.sin(ang), 2, axis=1) * sign, (1, RET_HEADS))
    rope_freq = 1.0 / (ROPE_THETA ** (jnp.arange(0, sub, 2, dtype=F32) / sub))
    ang = posf * rope_freq[None, :]
    sign = jnp.concatenate([-jnp.ones((sub // 2,), F32), jnp.ones((sub // 2,), F32)])
    cos_d = jnp.tile(jnp.tile(jnp.cos(ang), (1, 2)), (1, 2 * DIFF_HEADS))
    sin_d = jnp.tile(jnp.tile(jnp.sin(ang), (1, 2)) * sign, (1, 2 * DIFF_HEADS))
    return cos_r, sin_r, cos_d, sin_d


def _mix_in(x, mods, g_norm, win, gqk, gmat, rope, ret_tabs, s0, s0_layer, sizes, rc, layer, prompt):
    n_b, seq, d = x.shape
    ret_qk, ret_v, diff_qk, diff_v = sizes
    n_seq, lt = _token_tiles(n_b, seq)
    assert lt % rc == 0
    dk_h, dv_h = ret_qk // RET_HEADS, ret_v // RET_HEADS
    hv = diff_v // DIFF_HEADS

    def tok(width):
        return pl.BlockSpec((n_seq, lt, width), lambda b, i: (b, i, 0))

    def feat(width):
        return pl.BlockSpec((n_seq, width, lt), lambda b, i: (b, 0, i))

    def full(a):
        return _resident(a.shape, lambda b, i, _n=a.ndim: (0,) * _n)

    def pos_tab(a):
        return pl.BlockSpec((lt, a.shape[1]), lambda b, i: (i, 0))

    state_blk = (n_seq, RET_HEADS, dk_h, dv_h)
    if s0_layer is None:
        s0_spec = pl.BlockSpec(state_blk, lambda b, i: (b, 0, 0, 0))
    else:
        s0_spec = pl.BlockSpec((None,) + state_blk, lambda b, i: (s0_layer, b, 0, 0, 0))
    dv_spec = pl.BlockSpec((n_seq, lt * DIFF_HEADS, hv), lambda b, i: (b, i, 0))
    dv_shape = jax.ShapeDtypeStruct((n_b, seq * DIFF_HEADS, hv), F32)
    if prompt:
        att_specs = [feat(diff_qk), dv_spec, feat(diff_qk), tok(diff_qk),
                     pl.BlockSpec((n_seq, None, diff_v, lt), lambda b, i: (b, i, 0, 0))]
        att_shapes = [jax.ShapeDtypeStruct((n_b, diff_qk, seq), F32), dv_shape,
                      jax.ShapeDtypeStruct((n_b, diff_qk, seq), BF16),
                      jax.ShapeDtypeStruct((n_b, seq, diff_qk), BF16),
                      jax.ShapeDtypeStruct((n_b, seq // lt, diff_v, lt), BF16)]
    else:
        att_specs = [tok(diff_qk), dv_spec, tok(diff_qk), tok(diff_qk), tok(diff_v)]
        att_shapes = [jax.ShapeDtypeStruct((n_b, seq, diff_qk), F32), dv_shape,
                      jax.ShapeDtypeStruct((n_b, seq, diff_qk), BF16),
                      jax.ShapeDtypeStruct((n_b, seq, diff_qk), BF16),
                      jax.ShapeDtypeStruct((n_b, seq, diff_v), BF16)]
    return pl.pallas_call(
        functools.partial(_mix_in_kernel, rc=rc, sizes=sizes, prompt=prompt),
        grid=(n_b // n_seq, seq // lt),
        in_specs=[tok(d), _mod_spec(layer, 3, n_seq, d), _mod_spec(layer, 4, n_seq, d),
                  pl.BlockSpec((None, None, 1, d), lambda b, i: (layer, 1, 0, 0)),
                  _layer_weight(win, layer),
                  pl.BlockSpec((None, 2, 1, diff_qk), lambda b, i: (layer, 0, 0, 0)),
                  full(gmat),
                  pos_tab(rope[0]), pos_tab(rope[1]), pos_tab(rope[2]), pos_tab(rope[3]),
                  full(ret_tabs[0]), full(ret_tabs[1]), full(ret_tabs[2]), full(ret_tabs[3]),
                  s0_spec],
        out_specs=att_specs + [tok(ret_v), pl.BlockSpec(state_blk, lambda b, i: (b, 0, 0, 0))],
        out_shape=att_shapes + [jax.ShapeDtypeStruct((n_b, seq, ret_v), BF16),
                                jax.ShapeDtypeStruct((n_b,) + state_blk[1:], F32)],
        scratch_shapes=[pltpu.VMEM((n_seq, ret_qk, ret_v), F32)],
        compiler_params=_cparams(2),
        name="mix_in",
    )(x, mods, mods, g_norm, win, gqk, gmat, *rope, *ret_tabs, s0)


def _lambda(wl_ref, lam_init):
    wl = wl_ref[...]
    a = jnp.sum(wl[0:1] * wl[1:2], axis=-1, keepdims=True)
    b = jnp.sum(wl[2:3] * wl[3:4], axis=-1, keepdims=True)
    return jnp.exp(a) - jnp.exp(b) + lam_init


def _softmax_step_t(s_t, v_t, m_ref, l_ref, acc_ref):
    m_prev = m_ref[...]
    m_new = jnp.maximum(m_prev, jnp.max(s_t, axis=0, keepdims=True))
    alpha = jnp.exp(m_prev - m_new)
    p = jnp.exp(s_t - m_new)
    l_ref[...] = alpha * l_ref[...] + jnp.sum(p, axis=0, keepdims=True)
    acc_ref[...] = alpha * acc_ref[...] + _dot(v_t, p.astype(BF16))
    m_ref[...] = m_new


def _attn_prompt_kernel(wl_ref, gsub_ref, qt_ref, k_ref, vt_ref, o_ref, m_ref, l_ref, acc_ref, *, lam_init):
    hw, tq = qt_ref.shape
    tk = vt_ref.shape[-1]
    i = pl.program_id(2)
    qt = qt_ref[...]
    feat = lax.broadcasted_iota(jnp.int32, (hw, 1), 0)
    zero = jnp.zeros_like(qt)
    qs_t = jnp.concatenate([jnp.where(feat < hw // 2, qt, zero), jnp.where(feat >= hw // 2, qt, zero)], axis=1)
    m_ref[...] = jnp.full(m_ref.shape, -jnp.inf, F32)
    l_ref[...] = jnp.zeros(l_ref.shape, F32)
    acc_ref[...] = jnp.zeros(acc_ref.shape, F32)

    def full_tile(j, carry):
        r0 = pl.multiple_of(j * tk, tk)
        _softmax_step_t(_dot(k_ref[pl.ds(r0, tk), :], qs_t), vt_ref[j], m_ref, l_ref, acc_ref)
        return carry

    lax.fori_loop(0, i, full_tile, 0)
    k_chunk = lax.broadcasted_iota(jnp.int32, (tk, 2 * tq), 0) // CHUNK
    col = lax.broadcasted_iota(jnp.int32, (tk, 2 * tq), 1)
    q_chunk = jnp.where(col >= tq, col - tq, col) // CHUNK
    r0 = pl.multiple_of(i * tk, tk)
    s_t = jnp.where(k_chunk <= q_chunk, _dot(k_ref[pl.ds(r0, tk), :], qs_t), -jnp.inf)
    _softmax_step_t(s_t, vt_ref[i], m_ref, l_ref, acc_ref)

    acc = acc_ref[...]
    l = l_ref[...]
    o_t = acc[:, :tq] / l[:, :tq] - _lambda(wl_ref, lam_init) * (acc[:, tq:] / l[:, tq:])
    o_t = o_t * lax.rsqrt(jnp.mean(o_t * o_t, axis=0, keepdims=True) + EPS) * gsub_ref[...] * (1.0 - lam_init)
    o_ref[...] = jnp.transpose(o_t).astype(BF16)


def _attn_prompt(qt, k, vt, wl, gsub_col, lam_init, layer):
    n_b, width, seq = qt.shape
    n_kt, diff_v, tk = vt.shape[1:]
    hw, hv = width // DIFF_HEADS, diff_v // DIFF_HEADS
    tq = tk
    assert seq == n_kt * tk and tk % CHUNK == 0
    return pl.pallas_call(
        functools.partial(_attn_prompt_kernel, lam_init=lam_init),
        grid=(n_b, DIFF_HEADS, seq // tq),
        in_specs=[pl.BlockSpec((None,) + wl.shape[1:], lambda b, h, i: (layer, 0, 0)),
                  pl.BlockSpec((None,) + gsub_col.shape[1:], lambda b, h, i: (layer, 0, 0)),
                  pl.BlockSpec((None, hw, tq), lambda b, h, i: (b, h, i)),
                  pl.BlockSpec((None, seq, hw), lambda b, h, i: (b, 0, h)),
                  pl.BlockSpec((None, n_kt, hv, tk), lambda b, h, i: (b, 0, h, 0))],
        out_specs=pl.BlockSpec((None, tq, hv), lambda b, h, i: (b, i, h)),
        out_shape=jax.ShapeDtypeStruct((n_b, seq, diff_v), BF16),
        scratch_shapes=[pltpu.VMEM((1, 2 * tq), F32), pltpu.VMEM((1, 2 * tq), F32),
                        pltpu.VMEM((hv, 2 * tq), F32)],
        compiler_params=_cparams(3),
        name="attn_prompt",
    )(wl, gsub_col, qt, k, vt)


def _stack_subheads(q):
    w = q.shape[-1]
    lane = lax.broadcasted_iota(jnp.int32, (1, w), 1)
    zero = jnp.zeros_like(q)
    return jnp.concatenate([jnp.where(lane < w // 2, q, zero), jnp.where(lane >= w // 2, q, zero)], axis=0)


def _softmax_step(s, v, m_ref, l_ref, acc_ref):
    m_prev = m_ref[...]
    m_new = jnp.maximum(m_prev, jnp.max(s, axis=-1, keepdims=True))
    alpha = jnp.exp(m_prev - m_new)
    p = jnp.exp(s - m_new)
    l_ref[...] = alpha * l_ref[...] + jnp.sum(p, axis=-1, keepdims=True)
    acc_ref[...] = alpha * acc_ref[...] + _dot(p.astype(BF16), v)
    m_ref[...] = m_new


def _attn_sample_kernel(wl_ref, gsub_ref, q_ref, kct_ref, vc_ref, kn_ref, vn_ref, o_ref,
                        m_ref, l_ref, acc_ref, *, lam_init):
    j = pl.program_id(1)
    hw = q_ref.shape[-1] // DIFF_HEADS
    hv = vc_ref.shape[-1]
    tk = kct_ref.shape[-1]

    @pl.when(j == 0)
    def _():
        m_ref[...] = jnp.full(m_ref.shape, -jnp.inf, F32)
        l_ref[...] = jnp.zeros(l_ref.shape, F32)
        acc_ref[...] = jnp.zeros(acc_ref.shape, F32)

    def head_q(h):
        return _stack_subheads(q_ref[:, h * hw:(h + 1) * hw])

    for h in range(DIFF_HEADS):
        k_t = kct_ref[h * hw:(h + 1) * hw, :].astype(BF16)
        v = vc_ref[pl.ds(h, tk, stride=DIFF_HEADS), :].astype(BF16)
        _softmax_step(_dot(head_q(h), k_t), v, m_ref.at[h], l_ref.at[h], acc_ref.at[h])

    @pl.when(j == pl.num_programs(1) - 1)
    def _():
        lam = _lambda(wl_ref, lam_init)
        for h in range(DIFF_HEADS):
            _softmax_step(_dot_nt(head_q(h), kn_ref[:, h * hw:(h + 1) * hw]), vn_ref[:, h * hv:(h + 1) * hv],
                          m_ref.at[h], l_ref.at[h], acc_ref.at[h])
            acc = acc_ref[h]
            l = l_ref[h]
            rows = acc.shape[0] // 2
            o = acc[:rows] / l[:rows] - lam * (acc[rows:] / l[rows:])
            o_ref[:, h * hv:(h + 1) * hv] = (_rms(o) * gsub_ref[...] * (1.0 - lam_init)).astype(BF16)


def _attn_sample(q, k_new, v_new, cache_kt, cache_vi, wl, gsub_row, lam_init, layer):
    n_b, seq, width = q.shape
    past = cache_kt.shape[-1]
    hv = cache_vi.shape[-1]
    vw = v_new.shape[-1]
    assert past % CHUNK == 0 and seq <= CHUNK
    tk = min(past, CACHE_TK)
    assert past % tk == 0
    return pl.pallas_call(
        functools.partial(_attn_sample_kernel, lam_init=lam_init),
        grid=(n_b, past // tk),
        in_specs=[pl.BlockSpec((None,) + wl.shape[1:], lambda b, j: (layer, 0, 0)),
                  pl.BlockSpec((None,) + gsub_row.shape[1:], lambda b, j: (layer, 0, 0)),
                  pl.BlockSpec((None, seq, width), lambda b, j: (b, 0, 0)),
                  pl.BlockSpec((None, None, width, tk), lambda b, j: (layer, b, 0, j)),
                  pl.BlockSpec((None, None, tk * DIFF_HEADS, hv), lambda b, j: (layer, b, j, 0)),
                  pl.BlockSpec((None, seq, width), lambda b, j: (b, 0, 0)),
                  pl.BlockSpec((None, seq, vw), lambda b, j: (b, 0, 0))],
        out_specs=pl.BlockSpec((None, seq, vw), lambda b, j: (b, 0, 0)),
        out_shape=jax.ShapeDtypeStruct((n_b, seq, vw), BF16),
        scratch_shapes=[pltpu.VMEM((DIFF_HEADS, 2 * seq, 1), F32), pltpu.VMEM((DIFF_HEADS, 2 * seq, 1), F32),
                        pltpu.VMEM((DIFF_HEADS, 2 * seq, hv), F32)],
        compiler_params=_cparams(2),
        name="attn_sample",
    )(wl, gsub_row, q, cache_kt, cache_vi, k_new, v_new)


def _mix_out_kernel(x_ref, or_ref, od_ref, gate1_ref, shift_ref, scale_ref, gate2_ref, g_ref,
                    wout_ref, wup_ref, wdn_ref, o_ref):
    n_seq, lt, d = x_ref.shape
    m = n_seq * lt
    rw = or_ref.shape[-1]
    mix = (_dot(or_ref[...].reshape(m, rw), wout_ref[:rw, :])
           + _dot(od_ref[...].reshape(m, od_ref.shape[-1]), wout_ref[rw:, :]))
    x3 = x_ref[...] + gate1_ref[...] * mix.reshape(n_seq, lt, d)
    o_ref[...] = _ffn_apply(x3, g_ref[...], shift_ref[...], scale_ref[...], gate2_ref[...], wup_ref, wdn_ref)


def _mix_out(x, o_r, o_d, mods, g_norm, wout, wup, wdn, layer):
    n_b, seq, d = x.shape
    n_seq, lt = _token_tiles(n_b, seq)

    def tok(width):
        return pl.BlockSpec((n_seq, lt, width), lambda b, i: (b, i, 0))

    return pl.pallas_call(
        _mix_out_kernel,
        grid=(n_b // n_seq, seq // lt),
        in_specs=[tok(d), tok(o_r.shape[-1]), tok(o_d.shape[-1]),
                  _mod_spec(layer, 5, n_seq, d), _mod_spec(layer, 6, n_seq, d),
                  _mod_spec(layer, 7, n_seq, d), _mod_spec(layer, 8, n_seq, d),
                  pl.BlockSpec((None, None, 1, d), lambda b, i: (layer, 2, 0, 0)),
                  _layer_weight(wout, layer), _layer_weight(wup, layer, 1), _layer_weight(wdn, layer, 1)],
        out_specs=tok(d),
        out_shape=jax.ShapeDtypeStruct(x.shape, F32),
        compiler_params=_cparams(2),
        name="mix_out",
    )(x, o_r, o_d, mods, mods, mods, mods, g_norm, wout, wup, wdn)


def kernel(x_prompt, x_sample, cache_k, cache_v, state_ret, c_prompt, c_sample, w_ada, b_ada, g_norm,
           w_ff_up, w_ff_down, w_in, w_out, g_qk, w_lambda, g_sub):
    depth = w_in.shape[0]
    n_p, seq_p, d = x_prompt.shape
    n_s, seq_s, _ = x_sample.shape
    past = cache_k.shape[2]
    dk_h, dv_h = state_ret.shape[-2], state_ret.shape[-1]
    sub = cache_k.shape[-1]
    hv = cache_v.shape[-1]
    ret_qk, ret_v = RET_HEADS * dk_h, RET_HEADS * dv_h
    diff_qk, diff_v = DIFF_HEADS * 2 * sub, DIFF_HEADS * hv
    sizes = (ret_qk, ret_v, diff_qk, diff_v)

    mods = _ada_mods(jnp.concatenate([c_prompt, c_sample], axis=0), w_ada, b_ada)
    mods = mods[:, :, :, None, :]
    mods_p, mods_s = mods[:, :, :n_p], mods[:, :, n_p:]

    rope_p = _rope_tables(jnp.arange(seq_p), dk_h, sub)
    rope_s = _rope_tables(past + jnp.arange(seq_s), dk_h, sub)
    rc_p, rc_s = min(CHUNK, seq_p), min(CHUNK, seq_s)
    tabs_p = _retention_tables(rc_p, dk_h, dv_h)
    tabs_s = _retention_tables(rc_s, dk_h, dv_h)
    group = np.arange(diff_qk) // sub
    gmat = jnp.asarray(group[:, None] == group[None, :], BF16)
    s0_p = jnp.zeros((n_p, RET_HEADS, dk_h, dv_h), F32)
    cache_kt = jnp.transpose(cache_k, (0, 1, 3, 4, 5, 2)).reshape(depth, n_s, diff_qk, past)
    cache_vi = cache_v.reshape(depth, n_s, past * DIFF_HEADS, hv)

    wup = w_ff_up.astype(BF16)
    wdn = w_ff_down.astype(BF16)
    win = w_in.astype(BF16)
    wout = w_out.astype(BF16)
    g_norm4 = g_norm[:, :, None, :]
    gqk = jnp.tile(g_qk, (1, 1, diff_qk // sub))[:, :, None, :]
    wl = w_lambda.astype(F32)
    gsub_row = g_sub[:, None, :]
    gsub_col = g_sub[:, :, None]

    yp, ys = x_prompt, x_sample
    kp, vp, sp, ksl, vsl, ssl = [], [], [], [], [], []
    for l in range(depth):
        lam_init = 0.8 - 0.6 * math.exp(-0.3 * l)

        yp = _ffn1(yp, mods_p, g_norm4, wup, wdn, l)
        dk_t, dv_i, q_t, k_a, v_t, o_r, s_new = _mix_in(yp, mods_p, g_norm4, win, gqk, gmat, rope_p, tabs_p,
                                                         s0_p, None, sizes, rc_p, l, True)
        o_d = _attn_prompt(q_t, k_a, v_t, wl, gsub_col, lam_init, l)
        yp = _mix_out(yp, o_r, o_d, mods_p, g_norm4, wout, wup, wdn, l)
        kp.append(dk_t); vp.append(dv_i); sp.append(s_new)

        ys = _ffn1(ys, mods_s, g_norm4, wup, wdn, l)
        dk, dv_i, q_a, k_a, v_a, o_r, s_new = _mix_in(ys, mods_s, g_norm4, win, gqk, gmat, rope_s, tabs_s,
                                                       state_ret, l, sizes, rc_s, l, False)
        o_d = _attn_sample(q_a, k_a, v_a, cache_kt, cache_vi, wl, gsub_row, lam_init, l)
        ys = _mix_out(ys, o_r, o_d, mods_s, g_norm4, wout, wup, wdn, l)
        ksl.append(dk); vsl.append(dv_i); ssl.append(s_new)

    k_prompt = jnp.transpose(jnp.stack(kp).reshape(depth, n_p, DIFF_HEADS, 2, sub, seq_p), (0, 1, 5, 2, 3, 4))
    v_prompt = jnp.stack(vp).reshape(depth, n_p, seq_p, DIFF_HEADS, hv)
    k_sample = jnp.stack(ksl).reshape(depth, n_s, seq_s, DIFF_HEADS, 2, sub)
    v_sample = jnp.stack(vsl).reshape(depth, n_s, seq_s, DIFF_HEADS, hv)
    return (yp, ys, k_prompt, v_prompt, jnp.stack(sp), k_sample, v_sample, jnp.stack(ssl))
```

```python
import functools
import math

import numpy as np
import jax
import jax.numpy as jnp
from jax import lax
from jax.experimental import pallas as pl
from jax.experimental.pallas import tpu as pltpu

F32 = jnp.float32
BF16 = jnp.bfloat16

CHUNK = 64
RET_HEADS = 4
DIFF_HEADS = 4
ROPE_THETA = 10000.0
EPS = 1e-6
TOKEN_TILE = 512
FF_CHUNK = 256
RET_CHUNK = 256
ATTN_GROUP = 256
ATTN_LOOKAHEAD = 3
CACHE_TK = 2048
VMEM_LIMIT = 56 * 1024 * 1024


def _cparams(n_axes):
    return pltpu.CompilerParams(dimension_semantics=("arbitrary",) * n_axes,
                                vmem_limit_bytes=VMEM_LIMIT)


def _resident(block_shape, index_map):
    return pl.BlockSpec(block_shape, index_map, pipeline_mode=pl.Buffered(1))


def _rms(x):
    return x * lax.rsqrt(jnp.mean(x * x, axis=-1, keepdims=True) + EPS)


def _dot(a, b):
    return jnp.dot(a, b, preferred_element_type=F32)


def _dot_nt(a, b):
    return lax.dot_general(a, b, (((1,), (1,)), ((), ())), preferred_element_type=F32)


def _split_bf16(x):
    hi = x.astype(BF16)
    return hi, (x - hi.astype(F32)).astype(BF16)


def _ada_kernel(c_ref, w_ref, b_ref, o_ref):
    n = c_ref.shape[0]
    c = c_ref[...]
    a_hi, a_lo = _split_bf16(c * jax.nn.sigmoid(c))
    w_hi, w_lo = _split_bf16(w_ref[...])
    a_both = jnp.concatenate([a_hi.astype(F32), a_lo.astype(F32)], axis=0).astype(BF16)
    first = _dot(a_both, w_hi)
    o_ref[...] = first[:n] + first[n:] + _dot(a_hi, w_lo) + b_ref[...]


def _ada_mods(c_all, w_ada, b_ada):
    depth, d, n_out = w_ada.shape
    n_seq = c_all.shape[0]
    n_blk = n_out // d
    return pl.pallas_call(
        _ada_kernel,
        grid=(depth, n_blk),
        in_specs=[
            pl.BlockSpec((n_seq, d), lambda l, j: (0, 0)),
            pl.BlockSpec((None, d, d), lambda l, j: (l, 0, j)),
            pl.BlockSpec((None, None, 1, d), lambda l, j: (l, j, 0, 0)),
        ],
        out_specs=pl.BlockSpec((None, None, n_seq, d), lambda l, j: (l, j, 0, 0)),
        out_shape=jax.ShapeDtypeStruct((depth, n_blk, n_seq, d), F32),
        compiler_params=_cparams(2),
        name="ada_mods",
    )(c_all, w_ada, b_ada.reshape(depth, n_blk, 1, d))


def _pre(x3, g, shift, scale):
    return _rms(x3) * g * (1.0 + scale) + shift


def _ffn_apply(x3, g, shift, scale, gate, wup_ref, wdn_ref):
    n_seq, lt, d = x3.shape
    d_ff = wdn_ref.shape[0]
    hb = _pre(x3, g, shift, scale).reshape(n_seq * lt, d).astype(BF16)
    acc = jnp.zeros((n_seq * lt, d), F32)
    for c in range(d_ff // FF_CHUNK):
        lo = c * FF_CHUNK
        a = _dot(hb, wup_ref[:, lo:lo + FF_CHUNK])
        b = _dot(hb, wup_ref[:, d_ff + lo:d_ff + lo + FF_CHUNK])
        act = (a * jax.nn.sigmoid(a) * b).astype(BF16)
        acc = acc + _dot(act, wdn_ref[lo:lo + FF_CHUNK, :])
    return x3 + (0.5 * gate) * acc.reshape(n_seq, lt, d)


def _ffn_kernel(x_ref, shift_ref, scale_ref, gate_ref, g_ref, wup_ref, wdn_ref, o_ref):
    o_ref[...] = _ffn_apply(x_ref[...], g_ref[...], shift_ref[...], scale_ref[...], gate_ref[...],
                            wup_ref, wdn_ref)


def _token_tiles(n_b, seq):
    lt = min(seq, TOKEN_TILE)
    n_seq = max(1, min(n_b, TOKEN_TILE // lt))
    assert seq % lt == 0 and n_b % n_seq == 0 and lt % 8 == 0
    return n_seq, lt


def _mod_spec(layer, idx, n_seq, d):
    return pl.BlockSpec((None, None, n_seq, 1, d), lambda b, i: (layer, idx, b, 0, 0))


def _layer_weight(w, *lead):
    n_lead = len(lead)
    return _resident((None,) * n_lead + w.shape[n_lead:], lambda b, i: lead + (0,) * (w.ndim - n_lead))


def _ffn1(x, mods, g_norm, wup, wdn, layer):
    n_b, seq, d = x.shape
    n_seq, lt = _token_tiles(n_b, seq)
    x_spec = pl.BlockSpec((n_seq, lt, d), lambda b, i: (b, i, 0))
    return pl.pallas_call(
        _ffn_kernel,
        grid=(n_b // n_seq, seq // lt),
        in_specs=[x_spec, _mod_spec(layer, 0, n_seq, d), _mod_spec(layer, 1, n_seq, d),
                  _mod_spec(layer, 2, n_seq, d),
                  pl.BlockSpec((None, None, 1, d), lambda b, i: (layer, 0, 0, 0)),
                  _layer_weight(wup, layer, 0), _layer_weight(wdn, layer, 0)],
        out_specs=x_spec,
        out_shape=jax.ShapeDtypeStruct(x.shape, F32),
        compiler_params=_cparams(2),
        name="ffn1",
    )(x, mods, mods, mods, g_norm, wup, wdn)


def _swap_pairs(x):
    w = x.shape[-1]
    lane = lax.broadcasted_iota(jnp.int32, (1, w), 1)
    return jnp.where((lane & 1) == 0, pltpu.roll(x, w - 1, 1), pltpu.roll(x, 1, 1))


def _swap_halves(x, half):
    w = x.shape[-1]
    lane = lax.broadcasted_iota(jnp.int32, (1, w), 1)
    return jnp.where((lane & (2 * half - 1)) < half, pltpu.roll(x, w - half, 1), pltpu.roll(x, half, 1))


def _mix_in_kernel(x_ref, shift_ref, scale_ref, g_ref, win_ref, gqk_ref, gmat_ref,
                   cosr_ref, sinr_ref, cosd_ref, sind_ref, dmask_ref, qdec_ref, kdec_ref, cdec_ref, s0_ref,
                   dk_ref, dv_ref, qa_ref, ka_ref, va_ref, or_ref, snew_ref, s_scr, *, rc, sizes, prompt):
    ret_qk, ret_v, diff_qk, diff_v = sizes
    n_seq, lt, d = x_ref.shape
    m = n_seq * lt
    dk_h = ret_qk // RET_HEADS
    dv_h = ret_v // RET_HEADS
    hv = diff_v // DIFF_HEADS
    hb = _pre(x_ref[...], g_ref[...], shift_ref[...], scale_ref[...]).reshape(m, d).astype(BF16)

    widths = (ret_qk, ret_qk, ret_v, ret_v, diff_qk, diff_qk, diff_v)

    def proj(n):
        lo = sum(widths[:n])
        return _dot(hb, win_ref[:, lo:lo + widths[n]])

    def rope(x, partner, cos_ref, sin_ref):
        w = x.shape[-1]
        y = x.reshape(n_seq, lt, w) * cos_ref[...] + partner.reshape(n_seq, lt, w) * sin_ref[...]
        return y.reshape(m, w)

    rq, rk = proj(0), proj(1)
    dq, dk = proj(4), proj(5)
    rv = proj(2)

    rq = rope(rq, _swap_pairs(rq), cosr_ref, sinr_ref)
    rk = rope(rk, _swap_pairs(rk), cosr_ref, sinr_ref) * (dk_h ** -0.5)

    sub = diff_qk // (2 * DIFF_HEADS)
    qk = jnp.concatenate([dq, dk], axis=0)
    gmean = _dot((qk * qk).astype(BF16), gmat_ref[...]) * (1.0 / sub)
    rg, dv = proj(3), proj(6)
    qk = qk * lax.rsqrt(gmean + EPS)
    dq = qk[:m] * gqk_ref[0]
    dk = qk[m:] * gqk_ref[1]
    dq = rope(dq, _swap_halves(dq, sub // 2), cosd_ref, sind_ref) * (sub ** -0.5)
    dk = rope(dk, _swap_halves(dk, sub // 2), cosd_ref, sind_ref)

    for s in range(n_seq):
        rows = slice(s * lt, (s + 1) * lt)
        for h in range(DIFF_HEADS):
            dv_ref[s, pl.ds(h, lt, stride=DIFF_HEADS), :] = dv[rows, h * hv:(h + 1) * hv]
        if prompt:
            dk_t = jnp.transpose(dk[rows])
            dk_ref[s] = dk_t
            ka_ref[s] = dk[rows].astype(BF16)
            qa_ref[s] = jnp.transpose(dq[rows]).astype(BF16)
            va_ref[s] = jnp.transpose(dv[rows]).astype(BF16)
        else:
            dk_ref[s] = dk[rows]
            ka_ref[s] = dk[rows].astype(BF16)
            qa_ref[s] = dq[rows].astype(BF16)
            va_ref[s] = dv[rows].astype(BF16)

    @pl.when(pl.program_id(1) == 0)
    def _():
        s_scr[...] = s0_ref[...]

    lane_q = lax.broadcasted_iota(jnp.int32, (1, ret_qk), 1)
    gate = rg * jax.nn.sigmoid(rg)
    rv_b = rv.astype(BF16)
    zero_blk = jnp.zeros((dk_h, dv_h), BF16)
    chunks = [(s, c, s * lt + c * rc) for s in range(n_seq) for c in range(lt // rc)]

    def head_cols(h, width):
        return slice(h * width, (h + 1) * width)

    att, upd = {}, {}
    for s, c, r0 in chunks:
        q = rq[r0:r0 + rc]
        q_heads = jnp.concatenate(
            [jnp.where((lane_q >= h * dk_h) & (lane_q < (h + 1) * dk_h), q, 0.0) for h in range(RET_HEADS)],
            axis=0).astype(BF16)
        att[s, c] = (_dot_nt(q_heads, rk[r0:r0 + rc].astype(BF16)) * dmask_ref[...]).astype(BF16)
    for s, c, r0 in chunks:
        kd_t = jnp.transpose(rk[r0:r0 + rc] * kdec_ref[...]).astype(BF16)
        upd[s, c] = [_dot(kd_t[head_cols(h, dk_h)], rv_b[r0:r0 + rc, head_cols(h, dv_h)])
                     for h in range(RET_HEADS)]
    for s in range(n_seq):
        state = [s_scr[s, h] for h in range(RET_HEADS)]
        for c in range(lt // rc):
            r0 = s * lt + c * rc
            s_diag = jnp.concatenate(
                [jnp.concatenate([state[h].astype(BF16) if g == h else zero_blk for g in range(RET_HEADS)], axis=1)
                 for h in range(RET_HEADS)], axis=0)
            inter = _dot((rq[r0:r0 + rc] * qdec_ref[...]).astype(BF16), s_diag)
            for h in range(RET_HEADS):
                cols = head_cols(h, dv_h)
                o = _dot(att[s, c][h * rc:(h + 1) * rc], rv_b[r0:r0 + rc, cols]) + inter[:, cols]
                or_ref[s, c * rc:(c + 1) * rc, cols] = (_rms(o) * gate[r0:r0 + rc, cols]).astype(BF16)
                state[h] = cdec_ref[h] * state[h] + upd[s, c][h]
        for h in range(RET_HEADS):
            s_scr[s, h] = state[h]
            snew_ref[s, h] = state[h]


def _retention_tables(rc, dk_h, dv_h):
    n_h = RET_HEADS
    log_g = jnp.log1p(-jnp.exp2(-5.0 - jnp.arange(n_h, dtype=F32)))
    idx = jnp.arange(rc, dtype=F32)
    rel = idx[:, None] - idx[None, :]
    dmask = jnp.where(rel >= 0, jnp.exp(log_g[:, None, None] * jnp.maximum(rel, 0.0)), 0.0)
    q_dec = jnp.exp(log_g[None, :] * (idx[:, None] + 1.0))
    k_dec = jnp.exp(log_g[None, :] * (rc - 1.0 - idx[:, None]))
    c_dec = jnp.exp(log_g * rc)
    return (dmask.reshape(n_h * rc, rc), jnp.repeat(q_dec, dk_h, axis=1), jnp.repeat(k_dec, dk_h, axis=1),
            jnp.broadcast_to(c_dec[:, None, None], (n_h, dk_h, dv_h)))


def _rope_tables(pos, dk_h, sub):
    posf = pos.astype(F32)[:, None]
    ret_freq = 1.0 / (ROPE_THETA ** jnp.linspace(0.0, 1.0, dk_h // 2, dtype=F32))
    ang = posf * ret_freq[None, :]
    sign = jnp.tile(jnp.array([-1.0, 1.0], F32), dk_h // 2)
    cos_r = jnp.tile(jnp.repeat(jnp.cos(ang), 2, axis=1), (1, RET_HEADS))
    sin_r = jnp.tile(jnp.repeat(jnp.sin(ang), 2, axis=1) * sign, (1, RET_HEADS))
    rope_freq = 1.0 / (ROPE_THETA ** (jnp.arange(0, sub, 2, dtype=F32) / sub))
    ang = posf * rope_freq[None, :]
    sign = jnp.concatenate([-jnp.ones((sub // 2,), F32), jnp.ones((sub // 2,), F32)])
    cos_d = jnp.tile(jnp.tile(jnp.cos(ang), (1, 2)), (1, 2 * DIFF_HEADS))
    sin_d = jnp.tile(jnp.tile(jnp.sin(ang), (1, 2)) * sign, (1, 2 * DIFF_HEADS))
    return cos_r, sin_r, cos_d, sin_d


def _mix_in(x, mods, g_norm, win, gqk, gmat, rope, ret_tabs, s0, s0_layer, sizes, rc, layer, prompt):
    n_b, seq, d = x.shape
    ret_qk, ret_v, diff_qk, diff_v = sizes
    n_seq, lt = _token_tiles(n_b, seq)
    assert lt % rc == 0
    dk_h, dv_h = ret_qk // RET_HEADS, ret_v // RET_HEADS
    hv = diff_v // DIFF_HEADS

    def tok(width):
        return pl.BlockSpec((n_seq, lt, width), lambda b, i: (b, i, 0))

    def feat(width):
        return pl.BlockSpec((n_seq, width, lt), lambda b, i: (b, 0, i))

    def full(a):
        return _resident(a.shape, lambda b, i, _n=a.ndim: (0,) * _n)

    def pos_tab(a):
        return pl.BlockSpec((lt, a.shape[1]), lambda b, i: (i, 0))

    state_blk = (n_seq, RET_HEADS, dk_h, dv_h)
    if s0_layer is None:
        s0_spec = pl.BlockSpec(state_blk, lambda b, i: (b, 0, 0, 0))
    else:
        s0_spec = pl.BlockSpec((None,) + state_blk, lambda b, i: (s0_layer, b, 0, 0, 0))
    dv_spec = pl.BlockSpec((n_seq, lt * DIFF_HEADS, hv), lambda b, i: (b, i, 0))
    dv_shape = jax.ShapeDtypeStruct((n_b, seq * DIFF_HEADS, hv), F32)
    if prompt:
        att_specs = [feat(diff_qk), dv_spec, feat(diff_qk), tok(diff_qk),
                     pl.BlockSpec((n_seq, None, diff_v, lt), lambda b, i: (b, i, 0, 0))]
        att_shapes = [jax.ShapeDtypeStruct((n_b, diff_qk, seq), F32), dv_shape,
                      jax.ShapeDtypeStruct((n_b, diff_qk, seq), BF16),
                      jax.ShapeDtypeStruct((n_b, seq, diff_qk), BF16),
                      jax.ShapeDtypeStruct((n_b, seq // lt, diff_v, lt), BF16)]
    else:
        att_specs = [tok(diff_qk), dv_spec, tok(diff_qk), tok(diff_qk), tok(diff_v)]
        att_shapes = [jax.ShapeDtypeStruct((n_b, seq, diff_qk), F32), dv_shape,
                      jax.ShapeDtypeStruct((n_b, seq, diff_qk), BF16),
                      jax.ShapeDtypeStruct((n_b, seq, diff_qk), BF16),
                      jax.ShapeDtypeStruct((n_b, seq, diff_v), BF16)]
    return pl.pallas_call(
        functools.partial(_mix_in_kernel, rc=rc, sizes=sizes, prompt=prompt),
        grid=(n_b // n_seq, seq // lt),
        in_specs=[tok(d), _mod_spec(layer, 3, n_seq, d), _mod_spec(layer, 4, n_seq, d),
                  pl.BlockSpec((None, None, 1, d), lambda b, i: (layer, 1, 0, 0)),
                  _layer_weight(win, layer),
                  pl.BlockSpec((None, 2, 1, diff_qk), lambda b, i: (layer, 0, 0, 0)),
                  full(gmat),
                  pos_tab(rope[0]), pos_tab(rope[1]), pos_tab(rope[2]), pos_tab(rope[3]),
                  full(ret_tabs[0]), full(ret_tabs[1]), full(ret_tabs[2]), full(ret_tabs[3]),
                  s0_spec],
        out_specs=att_specs + [tok(ret_v), pl.BlockSpec(state_blk, lambda b, i: (b, 0, 0, 0))],
        out_shape=att_shapes + [jax.ShapeDtypeStruct((n_b, seq, ret_v), BF16),
                                jax.ShapeDtypeStruct((n_b,) + state_blk[1:], F32)],
        scratch_shapes=[pltpu.VMEM(state_blk, F32)],
        compiler_params=_cparams(2),
        name="mix_in",
    )(x, mods, mods, g_norm, win, gqk, gmat, *rope, *ret_tabs, s0)


def _lambda(wl_ref, lam_init):
    wl = wl_ref[...]
    a = jnp.sum(wl[0:1] * wl[1:2], axis=-1, keepdims=True)
    b = jnp.sum(wl[2:3] * wl[3:4], axis=-1, keepdims=True)
    return jnp.exp(a) - jnp.exp(b) + lam_init


def _softmax_step_t(s_t, v_t, m_ref, l_ref, acc_ref):
    m_prev = m_ref[...]
    m_new = jnp.maximum(m_prev, jnp.max(s_t, axis=0, keepdims=True))
    alpha = jnp.exp(m_prev - m_new)
    p = jnp.exp(s_t - m_new)
    l_ref[...] = alpha * l_ref[...] + jnp.sum(p, axis=0, keepdims=True)
    acc_ref[...] = alpha * acc_ref[...] + _dot(v_t, p.astype(BF16))
    m_ref[...] = m_new


def _attn_prompt_kernel(wl_ref, gsub_ref, qt_ref, k_ref, vt_ref, o_ref, *stats, lam_init):
    hw, seq = qt_ref.shape
    n_kt, _, tk = vt_ref.shape
    tq = tk
    gw = min(tq, ATTN_GROUP)
    n_g = tq // gw
    feat = lax.broadcasted_iota(jnp.int32, (hw, 1), 0)
    lam = _lambda(wl_ref, lam_init)

    def refs(sub, g):
        return stats[3 * (sub * n_g + g):3 * (sub * n_g + g) + 3]

    units = [(i, j, sub, g, tk if j < i else min(tk, (g + 1) * gw))
             for i in range(seq // tq) for j in range(i + 1) for sub in range(2) for g in range(n_g)]

    def scores(unit):
        i, j, sub, g, n_keys = unit
        q = qt_ref[:, i * tq + g * gw:i * tq + (g + 1) * gw]
        q = jnp.where(feat < hw // 2 if sub == 0 else feat >= hw // 2, q, jnp.zeros_like(q))
        s_t = _dot(k_ref[j * tk:j * tk + n_keys, :], q)
        if j == i:
            k_chunk = lax.broadcasted_iota(jnp.int32, (n_keys, gw), 0) // CHUNK
            q_chunk = (lax.broadcasted_iota(jnp.int32, (n_keys, gw), 1) + g * gw) // CHUNK
            s_t = jnp.where(k_chunk <= q_chunk, s_t, -jnp.inf)
        return s_t

    pending = [scores(u) for u in units[:ATTN_LOOKAHEAD]]
    for n, (i, j, sub, g, n_keys) in enumerate(units):
        if n + ATTN_LOOKAHEAD < len(units):
            pending.append(scores(units[n + ATTN_LOOKAHEAD]))
        m_ref, l_ref, acc_ref = refs(sub, g)
        if j == 0:
            m_ref[...] = jnp.full(m_ref.shape, -jnp.inf, F32)
            l_ref[...] = jnp.zeros(l_ref.shape, F32)
            acc_ref[...] = jnp.zeros(acc_ref.shape, F32)
        _softmax_step_t(pending.pop(0), vt_ref[j, :, :n_keys], m_ref, l_ref, acc_ref)
        if j == i and sub == 1 and g == n_g - 1:
            o_t = jnp.concatenate(
                [refs(0, c)[2][...] / refs(0, c)[1][...] - lam * (refs(1, c)[2][...] / refs(1, c)[1][...])
                 for c in range(n_g)], axis=1)
            o_t = (o_t * lax.rsqrt(jnp.mean(o_t * o_t, axis=0, keepdims=True) + EPS)
                   * gsub_ref[...] * (1.0 - lam_init))
            o_ref[i * tq:(i + 1) * tq, :] = jnp.transpose(o_t).astype(BF16)


def _attn_prompt(qt, k, vt, wl, gsub_col, lam_init, layer):
    n_b, width, seq = qt.shape
    n_kt, diff_v, tk = vt.shape[1:]
    hw, hv = width // DIFF_HEADS, diff_v // DIFF_HEADS
    gw = min(tk, ATTN_GROUP)
    assert seq == n_kt * tk and tk % CHUNK == 0 and tk % gw == 0
    return pl.pallas_call(
        functools.partial(_attn_prompt_kernel, lam_init=lam_init),
        grid=(n_b, DIFF_HEADS),
        in_specs=[pl.BlockSpec((None,) + wl.shape[1:], lambda b, h: (layer, 0, 0)),
                  pl.BlockSpec((None,) + gsub_col.shape[1:], lambda b, h: (layer, 0, 0)),
                  pl.BlockSpec((None, hw, seq), lambda b, h: (b, h, 0)),
                  pl.BlockSpec((None, seq, hw), lambda b, h: (b, 0, h)),
                  pl.BlockSpec((None, n_kt, hv, tk), lambda b, h: (b, 0, h, 0))],
        out_specs=pl.BlockSpec((None, seq, hv), lambda b, h: (b, 0, h)),
        out_shape=jax.ShapeDtypeStruct((n_b, seq, diff_v), BF16),
        scratch_shapes=[pltpu.VMEM((1, gw), F32), pltpu.VMEM((1, gw), F32), pltpu.VMEM((hv, gw), F32)]
        * (2 * (tk // gw)),
        compiler_params=_cparams(2),
        name="attn_prompt",
    )(wl, gsub_col, qt, k, vt)


def _stack_subheads(q):
    w = q.shape[-1]
    lane = lax.broadcasted_iota(jnp.int32, (1, w), 1)
    zero = jnp.zeros_like(q)
    return jnp.concatenate([jnp.where(lane < w // 2, q, zero), jnp.where(lane >= w // 2, q, zero)], axis=0)


def _softmax_step(s, v, m_ref, l_ref, acc_ref):
    m_prev = m_ref[...]
    m_new = jnp.maximum(m_prev, jnp.max(s, axis=-1, keepdims=True))
    alpha = jnp.exp(m_prev - m_new)
    p = jnp.exp(s - m_new)
    l_ref[...] = alpha * l_ref[...] + jnp.sum(p, axis=-1, keepdims=True)
    acc_ref[...] = alpha * acc_ref[...] + _dot(p.astype(BF16), v)
    m_ref[...] = m_new


def _attn_sample_kernel(wl_ref, gsub_ref, q_ref, kct_ref, vc_ref, kn_ref, vn_ref, o_ref,
                        m_ref, l_ref, acc_ref, *, lam_init):
    j = pl.program_id(1)
    hw = q_ref.shape[-1] // DIFF_HEADS
    hv = vc_ref.shape[-1]
    tk = kct_ref.shape[-1]

    @pl.when(j == 0)
    def _():
        m_ref[...] = jnp.full(m_ref.shape, -jnp.inf, F32)
        l_ref[...] = jnp.zeros(l_ref.shape, F32)
        acc_ref[...] = jnp.zeros(acc_ref.shape, F32)

    def head_q(h):
        return _stack_subheads(q_ref[:, h * hw:(h + 1) * hw])

    for h in range(DIFF_HEADS):
        k_t = kct_ref[h * hw:(h + 1) * hw, :].astype(BF16)
        v = vc_ref[pl.ds(h, tk, stride=DIFF_HEADS), :].astype(BF16)
        _softmax_step(_dot(head_q(h), k_t), v, m_ref.at[h], l_ref.at[h], acc_ref.at[h])

    @pl.when(j == pl.num_programs(1) - 1)
    def _():
        lam = _lambda(wl_ref, lam_init)
        for h in range(DIFF_HEADS):
            _softmax_step(_dot_nt(head_q(h), kn_ref[:, h * hw:(h + 1) * hw]), vn_ref[:, h * hv:(h + 1) * hv],
                          m_ref.at[h], l_ref.at[h], acc_ref.at[h])
            acc = acc_ref[h]
            l = l_ref[h]
            rows = acc.shape[0] // 2
            o = acc[:rows] / l[:rows] - lam * (acc[rows:] / l[rows:])
            o_ref[:, h * hv:(h + 1) * hv] = (_rms(o) * gsub_ref[...] * (1.0 - lam_init)).astype(BF16)


def _attn_sample(q, k_new, v_new, cache_kt, cache_vi, wl, gsub_row, lam_init, layer):
    n_b, seq, width = q.shape
    past = cache_kt.shape[-1]
    hv = cache_vi.shape[-1]
    vw = v_new.shape[-1]
    assert past % CHUNK == 0 and seq <= CHUNK
    tk = min(past, CACHE_TK)
    assert past % tk == 0
    return pl.pallas_call(
        functools.partial(_attn_sample_kernel, lam_init=lam_init),
        grid=(n_b, past // tk),
        in_specs=[pl.BlockSpec((None,) + wl.shape[1:], lambda b, j: (layer, 0, 0)),
                  pl.BlockSpec((None,) + gsub_row.shape[1:], lambda b, j: (layer, 0, 0)),
                  pl.BlockSpec((None, seq, width), lambda b, j: (b, 0, 0)),
                  pl.BlockSpec((None, None, width, tk), lambda b, j: (layer, b, 0, j)),
                  pl.BlockSpec((None, None, tk * DIFF_HEADS, hv), lambda b, j: (layer, b, j, 0)),
                  pl.BlockSpec((None, seq, width), lambda b, j: (b, 0, 0)),
                  pl.BlockSpec((None, seq, vw), lambda b, j: (b, 0, 0))],
        out_specs=pl.BlockSpec((None, seq, vw), lambda b, j: (b, 0, 0)),
        out_shape=jax.ShapeDtypeStruct((n_b, seq, vw), BF16),
        scratch_shapes=[pltpu.VMEM((DIFF_HEADS, 2 * seq, 1), F32), pltpu.VMEM((DIFF_HEADS, 2 * seq, 1), F32),
                        pltpu.VMEM((DIFF_HEADS, 2 * seq, hv), F32)],
        compiler_params=_cparams(2),
        name="attn_sample",
    )(wl, gsub_row, q, cache_kt, cache_vi, k_new, v_new)


def _mix_out_kernel(x_ref, or_ref, od_ref, gate1_ref, shift_ref, scale_ref, gate2_ref, g_ref,
                    wout_ref, wup_ref, wdn_ref, o_ref):
    n_seq, lt, d = x_ref.shape
    m = n_seq * lt
    rw = or_ref.shape[-1]
    mix = (_dot(or_ref[...].reshape(m, rw), wout_ref[:rw, :])
           + _dot(od_ref[...].reshape(m, od_ref.shape[-1]), wout_ref[rw:, :]))
    x3 = x_ref[...] + gate1_ref[...] * mix.reshape(n_seq, lt, d)
    o_ref[...] = _ffn_apply(x3, g_ref[...], shift_ref[...], scale_ref[...], gate2_ref[...], wup_ref, wdn_ref)


def _mix_out(x, o_r, o_d, mods, g_norm, wout, wup, wdn, layer):
    n_b, seq, d = x.shape
    n_seq, lt = _token_tiles(n_b, seq)

    def tok(width):
        return pl.BlockSpec((n_seq, lt, width), lambda b, i: (b, i, 0))

    return pl.pallas_call(
        _mix_out_kernel,
        grid=(n_b // n_seq, seq // lt),
        in_specs=[tok(d), tok(o_r.shape[-1]), tok(o_d.shape[-1]),
                  _mod_spec(layer, 5, n_seq, d), _mod_spec(layer, 6, n_seq, d),
                  _mod_spec(layer, 7, n_seq, d), _mod_spec(layer, 8, n_seq, d),
                  pl.BlockSpec((None, None, 1, d), lambda b, i: (layer, 2, 0, 0)),
                  _layer_weight(wout, layer), _layer_weight(wup, layer, 1), _layer_weight(wdn, layer, 1)],
        out_specs=tok(d),
        out_shape=jax.ShapeDtypeStruct(x.shape, F32),
        compiler_params=_cparams(2),
        name="mix_out",
    )(x, o_r, o_d, mods, mods, mods, mods, g_norm, wout, wup, wdn)


def kernel(x_prompt, x_sample, cache_k, cache_v, state_ret, c_prompt, c_sample, w_ada, b_ada, g_norm,
           w_ff_up, w_ff_down, w_in, w_out, g_qk, w_lambda, g_sub):
    depth = w_in.shape[0]
    n_p, seq_p, d = x_prompt.shape
    n_s, seq_s, _ = x_sample.shape
    past = cache_k.shape[2]
    dk_h, dv_h = state_ret.shape[-2], state_ret.shape[-1]
    sub = cache_k.shape[-1]
    hv = cache_v.shape[-1]
    ret_qk, ret_v = RET_HEADS * dk_h, RET_HEADS * dv_h
    diff_qk, diff_v = DIFF_HEADS * 2 * sub, DIFF_HEADS * hv
    sizes = (ret_qk, ret_v, diff_qk, diff_v)

    mods = _ada_mods(jnp.concatenate([c_prompt, c_sample], axis=0), w_ada, b_ada)
    mods = mods[:, :, :, None, :]
    mods_p, mods_s = mods[:, :, :n_p], mods[:, :, n_p:]

    rope_p = _rope_tables(jnp.arange(seq_p), dk_h, sub)
    rope_s = _rope_tables(past + jnp.arange(seq_s), dk_h, sub)
    rc_p, rc_s = min(RET_CHUNK, seq_p), min(RET_CHUNK, seq_s)
    tabs_p = _retention_tables(rc_p, dk_h, dv_h)
    tabs_s = _retention_tables(rc_s, dk_h, dv_h)
    group = np.arange(diff_qk) // sub
    gmat = jnp.asarray(group[:, None] == group[None, :], BF16)
    s0_p = jnp.zeros((n_p, RET_HEADS, dk_h, dv_h), F32)
    cache_kt = jnp.transpose(cache_k, (0, 1, 3, 4, 5, 2)).reshape(depth, n_s, diff_qk, past)
    cache_vi = cache_v.reshape(depth, n_s, past * DIFF_HEADS, hv)

    wup = w_ff_up.astype(BF16)
    wdn = w_ff_down.astype(BF16)
    win = w_in.astype(BF16)
    wout = w_out.astype(BF16)
    g_norm4 = g_norm[:, :, None, :]
    gqk = jnp.tile(g_qk, (1, 1, diff_qk // sub))[:, :, None, :]
    wl = w_lambda.astype(F32)
    gsub_row = g_sub[:, None, :]
    gsub_col = g_sub[:, :, None]

    yp, ys = x_prompt, x_sample
    kp, vp, sp, ksl, vsl, ssl = [], [], [], [], [], []
    for l in range(depth):
        lam_init = 0.8 - 0.6 * math.exp(-0.3 * l)

        yp = _ffn1(yp, mods_p, g_norm4, wup, wdn, l)
        dk_t, dv_i, q_t, k_a, v_t, o_r, s_new = _mix_in(yp, mods_p, g_norm4, win, gqk, gmat, rope_p, tabs_p,
                                                         s0_p, None, sizes, rc_p, l, True)
        o_d = _attn_prompt(q_t, k_a, v_t, wl, gsub_col, lam_init, l)
        yp = _mix_out(yp, o_r, o_d, mods_p, g_norm4, wout, wup, wdn, l)
        kp.append(dk_t); vp.append(dv_i); sp.append(s_new)

        ys = _ffn1(ys, mods_s, g_norm4, wup, wdn, l)
        dk, dv_i, q_a, k_a, v_a, o_r, s_new = _mix_in(ys, mods_s, g_norm4, win, gqk, gmat, rope_s, tabs_s,
                                                       state_ret, l, sizes, rc_s, l, False)
        o_d = _attn_sample(q_a, k_a, v_a, cache_kt, cache_vi, wl, gsub_row, lam_init, l)
        ys = _mix_out(ys, o_r, o_d, mods_s, g_norm4, wout, wup, wdn, l)
        ksl.append(dk); vsl.append(dv_i); ssl.append(s_new)

    k_prompt = jnp.transpose(jnp.stack(kp).reshape(depth, n_p, DIFF_HEADS, 2, sub, seq_p), (0, 1, 5, 2, 3, 4))
    v_prompt = jnp.stack(vp).reshape(depth, n_p, seq_p, DIFF_HEADS, hv)
    k_sample = jnp.stack(ksl).reshape(depth, n_s, seq_s, DIFF_HEADS, 2, sub)
    v_sample = jnp.stack(vsl).reshape(depth, n_s, seq_s, DIFF_HEADS, hv)
    return (yp, ys, k_prompt, v_prompt, jnp.stack(sp), k_sample, v_sample, jnp.stack(ssl))
```

```python
import functools
import math

import numpy as np
import jax
import jax.numpy as jnp
from jax import lax
from jax.experimental import pallas as pl
from jax.experimental.pallas import tpu as pltpu

F32 = jnp.float32
BF16 = jnp.bfloat16

CHUNK = 64
RET_HEADS = 4
DIFF_HEADS = 4
ROPE_THETA = 10000.0
EPS = 1e-6
TOKEN_TILE = 512
FF_CHUNK = 256
RET_CHUNK = 256
ATTN_GROUP = 256
ATTN_LOOKAHEAD = 3
CACHE_TK = 2048
VMEM_LIMIT = 56 * 1024 * 1024


def _cparams(n_axes):
    return pltpu.CompilerParams(dimension_semantics=("arbitrary",) * n_axes,
                                vmem_limit_bytes=VMEM_LIMIT)


def _resident(block_shape, index_map):
    return pl.BlockSpec(block_shape, index_map, pipeline_mode=pl.Buffered(1))


def _rms(x):
    return x * lax.rsqrt(jnp.mean(x * x, axis=-1, keepdims=True) + EPS)


def _dot(a, b):
    return jnp.dot(a, b, preferred_element_type=F32)


def _dot_nt(a, b):
    return lax.dot_general(a, b, (((1,), (1,)), ((), ())), preferred_element_type=F32)


def _split_bf16(x):
    hi = x.astype(BF16)
    return hi, (x - hi.astype(F32)).astype(BF16)


def _ada_kernel(c_ref, w_ref, b_ref, o_ref):
    n = c_ref.shape[0]
    c = c_ref[...]
    a_hi, a_lo = _split_bf16(c * jax.nn.sigmoid(c))
    w_hi, w_lo = _split_bf16(w_ref[...])
    a_both = jnp.concatenate([a_hi.astype(F32), a_lo.astype(F32)], axis=0).astype(BF16)
    first = _dot(a_both, w_hi)
    o_ref[...] = first[:n] + first[n:] + _dot(a_hi, w_lo) + b_ref[...]


def _ada_mods(c_all, w_ada, b_ada):
    depth, d, n_out = w_ada.shape
    n_seq = c_all.shape[0]
    n_blk = n_out // d
    return pl.pallas_call(
        _ada_kernel,
        grid=(depth, n_blk),
        in_specs=[
            pl.BlockSpec((n_seq, d), lambda l, j: (0, 0)),
            pl.BlockSpec((None, d, d), lambda l, j: (l, 0, j)),
            pl.BlockSpec((None, None, 1, d), lambda l, j: (l, j, 0, 0)),
        ],
        out_specs=pl.BlockSpec((None, None, n_seq, d), lambda l, j: (l, j, 0, 0)),
        out_shape=jax.ShapeDtypeStruct((depth, n_blk, n_seq, d), F32),
        compiler_params=_cparams(2),
        name="ada_mods",
    )(c_all, w_ada, b_ada.reshape(depth, n_blk, 1, d))


def _pre(x3, g, shift, scale):
    return _rms(x3) * g * (1.0 + scale) + shift


def _ffn_apply(x3, g, shift, scale, gate, wup_ref, wdn_ref):
    n_seq, lt, d = x3.shape
    d_ff = wdn_ref.shape[0]
    hb = _pre(x3, g, shift, scale).reshape(n_seq * lt, d).astype(BF16)
    acc = jnp.zeros((n_seq * lt, d), F32)
    for c in range(d_ff // FF_CHUNK):
        lo = c * FF_CHUNK
        a = _dot(hb, wup_ref[:, lo:lo + FF_CHUNK])
        b = _dot(hb, wup_ref[:, d_ff + lo:d_ff + lo + FF_CHUNK])
        act = (a * jax.nn.sigmoid(a) * b).astype(BF16)
        acc = acc + _dot(act, wdn_ref[lo:lo + FF_CHUNK, :])
    return x3 + (0.5 * gate) * acc.reshape(n_seq, lt, d)


def _ffn_kernel(x_ref, shift_ref, scale_ref, gate_ref, g_ref, wup_ref, wdn_ref, o_ref):
    o_ref[...] = _ffn_apply(x_ref[...], g_ref[...], shift_ref[...], scale_ref[...], gate_ref[...],
                            wup_ref, wdn_ref)


def _token_tiles(n_b, seq):
    lt = min(seq, TOKEN_TILE)
    n_seq = max(1, min(n_b, TOKEN_TILE // lt))
    assert seq % lt == 0 and n_b % n_seq == 0 and lt % 8 == 0
    return n_seq, lt


def _mod_spec(layer, idx, n_seq, d):
    return pl.BlockSpec((None, None, n_seq, 1, d), lambda b, i: (layer, idx, b, 0, 0))


def _layer_weight(w, *lead):
    n_lead = len(lead)
    return _resident((None,) * n_lead + w.shape[n_lead:], lambda b, i: lead + (0,) * (w.ndim - n_lead))


def _ffn1(x, mods, g_norm, wup, wdn, layer):
    n_b, seq, d = x.shape
    n_seq, lt = _token_tiles(n_b, seq)
    x_spec = pl.BlockSpec((n_seq, lt, d), lambda b, i: (b, i, 0))
    return pl.pallas_call(
        _ffn_kernel,
        grid=(n_b // n_seq, seq // lt),
        in_specs=[x_spec, _mod_spec(layer, 0, n_seq, d), _mod_spec(layer, 1, n_seq, d),
                  _mod_spec(layer, 2, n_seq, d),
                  pl.BlockSpec((None, None, 1, d), lambda b, i: (layer, 0, 0, 0)),
                  _layer_weight(wup, layer, 0), _layer_weight(wdn, layer, 0)],
        out_specs=x_spec,
        out_shape=jax.ShapeDtypeStruct(x.shape, F32),
        compiler_params=_cparams(2),
        name="ffn1",
    )(x, mods, mods, mods, g_norm, wup, wdn)


def _swap_pairs(x):
    w = x.shape[-1]
    lane = lax.broadcasted_iota(jnp.int32, (1, w), 1)
    return jnp.where((lane & 1) == 0, pltpu.roll(x, w - 1, 1), pltpu.roll(x, 1, 1))


def _swap_halves(x, half):
    w = x.shape[-1]
    lane = lax.broadcasted_iota(jnp.int32, (1, w), 1)
    return jnp.where((lane & (2 * half - 1)) < half, pltpu.roll(x, w - half, 1), pltpu.roll(x, half, 1))


def _mix_in_kernel(x_ref, shift_ref, scale_ref, g_ref, win_ref, gqk_ref, gmat_ref,
                   cosr_ref, sinr_ref, cosd_ref, sind_ref, dmask_ref, qdec_ref, kdec_ref, cdec_ref, s0_ref,
                   *rest, rc, sizes, prompt, n_prev):
    if n_prev:
        dkp_ref, dvp_ref = rest[:2]
        rest = rest[2:]
    dk_ref, dv_ref, qa_ref, ka_ref, va_ref, or_ref, snew_ref, s_scr = rest
    if n_prev:
        dk_ref[:n_prev] = dkp_ref[...]
        dv_ref[:n_prev] = dvp_ref[...]
    ret_qk, ret_v, diff_qk, diff_v = sizes
    n_seq, lt, d = x_ref.shape
    m = n_seq * lt
    dk_h = ret_qk // RET_HEADS
    dv_h = ret_v // RET_HEADS
    hv = diff_v // DIFF_HEADS
    hb = _pre(x_ref[...], g_ref[...], shift_ref[...], scale_ref[...]).reshape(m, d).astype(BF16)

    widths = (ret_qk, ret_qk, ret_v, ret_v, diff_qk, diff_qk, diff_v)

    def proj(n):
        lo = sum(widths[:n])
        return _dot(hb, win_ref[:, lo:lo + widths[n]])

    def rope(x, partner, cos_ref, sin_ref):
        w = x.shape[-1]
        y = x.reshape(n_seq, lt, w) * cos_ref[...] + partner.reshape(n_seq, lt, w) * sin_ref[...]
        return y.reshape(m, w)

    rq, rk = proj(0), proj(1)
    dq, dk = proj(4), proj(5)
    rv = proj(2)

    rq = rope(rq, _swap_pairs(rq), cosr_ref, sinr_ref)
    rk = rope(rk, _swap_pairs(rk), cosr_ref, sinr_ref) * (dk_h ** -0.5)

    sub = diff_qk // (2 * DIFF_HEADS)
    qk = jnp.concatenate([dq, dk], axis=0)
    gmean = _dot((qk * qk).astype(BF16), gmat_ref[...]) * (1.0 / sub)
    rg, dv = proj(3), proj(6)
    qk = qk * lax.rsqrt(gmean + EPS)
    dq = qk[:m] * gqk_ref[0]
    dk = qk[m:] * gqk_ref[1]
    dq = rope(dq, _swap_halves(dq, sub // 2), cosd_ref, sind_ref) * (sub ** -0.5)
    dk = rope(dk, _swap_halves(dk, sub // 2), cosd_ref, sind_ref)

    for s in range(n_seq):
        rows = slice(s * lt, (s + 1) * lt)
        for h in range(DIFF_HEADS):
            dv_ref[n_prev, s, pl.ds(h, lt, stride=DIFF_HEADS), :] = dv[rows, h * hv:(h + 1) * hv]
        if prompt:
            dk_t = jnp.transpose(dk[rows])
            dk_ref[n_prev, s] = dk_t
            ka_ref[s] = dk[rows].astype(BF16)
            qa_ref[s] = jnp.transpose(dq[rows]).astype(BF16)
            va_ref[s] = jnp.transpose(dv[rows]).astype(BF16)
        else:
            dk_ref[n_prev, s] = dk[rows]
            ka_ref[s] = dk[rows].astype(BF16)
            qa_ref[s] = dq[rows].astype(BF16)
            va_ref[s] = dv[rows].astype(BF16)

    @pl.when(pl.program_id(1) == 0)
    def _():
        s_scr[...] = s0_ref[...]

    lane_q = lax.broadcasted_iota(jnp.int32, (1, ret_qk), 1)
    gate = rg * jax.nn.sigmoid(rg)
    rv_b = rv.astype(BF16)
    zero_blk = jnp.zeros((dk_h, dv_h), BF16)
    chunks = [(s, c, s * lt + c * rc) for s in range(n_seq) for c in range(lt // rc)]

    def head_cols(h, width):
        return slice(h * width, (h + 1) * width)

    att, upd = {}, {}
    for s, c, r0 in chunks:
        q = rq[r0:r0 + rc]
        q_heads = jnp.concatenate(
            [jnp.where((lane_q >= h * dk_h) & (lane_q < (h + 1) * dk_h), q, 0.0) for h in range(RET_HEADS)],
            axis=0).astype(BF16)
        att[s, c] = (_dot_nt(q_heads, rk[r0:r0 + rc].astype(BF16)) * dmask_ref[...]).astype(BF16)
    for s, c, r0 in chunks:
        kd_t = jnp.transpose(rk[r0:r0 + rc] * kdec_ref[...]).astype(BF16)
        upd[s, c] = [_dot(kd_t[head_cols(h, dk_h)], rv_b[r0:r0 + rc, head_cols(h, dv_h)])
                     for h in range(RET_HEADS)]
    for s in range(n_seq):
        state = [s_scr[s, h] for h in range(RET_HEADS)]
        for c in range(lt // rc):
            r0 = s * lt + c * rc
            s_diag = jnp.concatenate(
                [jnp.concatenate([state[h].astype(BF16) if g == h else zero_blk for g in range(RET_HEADS)], axis=1)
                 for h in range(RET_HEADS)], axis=0)
            inter = _dot((rq[r0:r0 + rc] * qdec_ref[...]).astype(BF16), s_diag)
            for h in range(RET_HEADS):
                cols = head_cols(h, dv_h)
                o = _dot(att[s, c][h * rc:(h + 1) * rc], rv_b[r0:r0 + rc, cols]) + inter[:, cols]
                or_ref[s, c * rc:(c + 1) * rc, cols] = (_rms(o) * gate[r0:r0 + rc, cols]).astype(BF16)
                state[h] = cdec_ref[h] * state[h] + upd[s, c][h]
        for h in range(RET_HEADS):
            s_scr[s, h] = state[h]
            snew_ref[s, h] = state[h]


def _retention_tables(rc, dk_h, dv_h):
    n_h = RET_HEADS
    log_g = jnp.log1p(-jnp.exp2(-5.0 - jnp.arange(n_h, dtype=F32)))
    idx = jnp.arange(rc, dtype=F32)
    rel = idx[:, None] - idx[None, :]
    dmask = jnp.where(rel >= 0, jnp.exp(log_g[:, None, None] * jnp.maximum(rel, 0.0)), 0.0)
    q_dec = jnp.exp(log_g[None, :] * (idx[:, None] + 1.0))
    k_dec = jnp.exp(log_g[None, :] * (rc - 1.0 - idx[:, None]))
    c_dec = jnp.exp(log_g * rc)
    return (dmask.reshape(n_h * rc, rc), jnp.repeat(q_dec, dk_h, axis=1), jnp.repeat(k_dec, dk_h, axis=1),
            jnp.broadcast_to(c_dec[:, None, None], (n_h, dk_h, dv_h)))


def _rope_tables(pos, dk_h, sub):
    posf = pos.astype(F32)[:, None]
    ret_freq = 1.0 / (ROPE_THETA ** jnp.linspace(0.0, 1.0, dk_h // 2, dtype=F32))
    ang = posf * ret_freq[None, :]
    sign = jnp.tile(jnp.array([-1.0, 1.0], F32), dk_h // 2)
    cos_r = jnp.tile(jnp.repeat(jnp.cos(ang), 2, axis=1), (1, RET_HEADS))
    sin_r = jnp.tile(jnp.repeat(jnp.sin(ang), 2, axis=1) * sign, (1, RET_HEADS))
    rope_freq = 1.0 / (ROPE_THETA ** (jnp.arange(0, sub, 2, dtype=F32) / sub))
    ang = posf * rope_freq[None, :]
    sign = jnp.concatenate([-jnp.ones((sub // 2,), F32), jnp.ones((sub // 2,), F32)])
    cos_d = jnp.tile(jnp.tile(jnp.cos(ang), (1, 2)), (1, 2 * DIFF_HEADS))
    sin_d = jnp.tile(jnp.tile(jnp.sin(ang), (1, 2)) * sign, (1, 2 * DIFF_HEADS))
    return cos_r, sin_r, cos_d, sin_d


def _mix_in(x, mods, g_norm, win, gqk, gmat, rope, ret_tabs, s0, s0_layer, sizes, rc, layer, prompt, prev_kv):
    n_b, seq, d = x.shape
    n_prev = 0 if prev_kv is None else prev_kv[0].shape[0]
    ret_qk, ret_v, diff_qk, diff_v = sizes
    n_seq, lt = _token_tiles(n_b, seq)
    assert lt % rc == 0
    dk_h, dv_h = ret_qk // RET_HEADS, ret_v // RET_HEADS
    hv = diff_v // DIFF_HEADS

    def tok(width):
        return pl.BlockSpec((n_seq, lt, width), lambda b, i: (b, i, 0))

    def feat(width):
        return pl.BlockSpec((n_seq, width, lt), lambda b, i: (b, 0, i))

    def full(a):
        return _resident(a.shape, lambda b, i, _n=a.ndim: (0,) * _n)

    def pos_tab(a):
        return pl.BlockSpec((lt, a.shape[1]), lambda b, i: (i, 0))

    state_blk = (n_seq, RET_HEADS, dk_h, dv_h)
    if s0_layer is None:
        s0_spec = pl.BlockSpec(state_blk, lambda b, i: (b, 0, 0, 0))
    else:
        s0_spec = pl.BlockSpec((None,) + state_blk, lambda b, i: (s0_layer, b, 0, 0, 0))
    dv_blk, dv_idx = (n_seq, lt * DIFF_HEADS, hv), (lambda b, i: (b, i, 0))
    dv_full = (n_b, seq * DIFF_HEADS, hv)
    if prompt:
        dk_blk, dk_idx, dk_full = (n_seq, diff_qk, lt), (lambda b, i: (b, 0, i)), (n_b, diff_qk, seq)
    else:
        dk_blk, dk_idx, dk_full = (n_seq, lt, diff_qk), (lambda b, i: (b, i, 0)), (n_b, seq, diff_qk)

    def stacked_spec(n, blk, idx):
        return pl.BlockSpec((n,) + blk, lambda b, i: (0,) + idx(b, i))

    kv_specs = [stacked_spec(n_prev + 1, dk_blk, dk_idx), stacked_spec(n_prev + 1, dv_blk, dv_idx)]
    kv_shapes = [jax.ShapeDtypeStruct((n_prev + 1,) + dk_full, F32),
                 jax.ShapeDtypeStruct((n_prev + 1,) + dv_full, F32)]
    prev_specs = [stacked_spec(n_prev, dk_blk, dk_idx), stacked_spec(n_prev, dv_blk, dv_idx)] if n_prev else []
    if prompt:
        att_specs = kv_specs + [feat(diff_qk), tok(diff_qk),
                                pl.BlockSpec((n_seq, None, diff_v, lt), lambda b, i: (b, i, 0, 0))]
        att_shapes = kv_shapes + [jax.ShapeDtypeStruct((n_b, diff_qk, seq), BF16),
                                  jax.ShapeDtypeStruct((n_b, seq, diff_qk), BF16),
                                  jax.ShapeDtypeStruct((n_b, seq // lt, diff_v, lt), BF16)]
    else:
        att_specs = kv_specs + [tok(diff_qk), tok(diff_qk), tok(diff_v)]
        att_shapes = kv_shapes + [jax.ShapeDtypeStruct((n_b, seq, diff_qk), BF16),
                                  jax.ShapeDtypeStruct((n_b, seq, diff_qk), BF16),
                                  jax.ShapeDtypeStruct((n_b, seq, diff_v), BF16)]
    return pl.pallas_call(
        functools.partial(_mix_in_kernel, rc=rc, sizes=sizes, prompt=prompt, n_prev=n_prev),
        grid=(n_b // n_seq, seq // lt),
        in_specs=[tok(d), _mod_spec(layer, 3, n_seq, d), _mod_spec(layer, 4, n_seq, d),
                  pl.BlockSpec((None, None, 1, d), lambda b, i: (layer, 1, 0, 0)),
                  _layer_weight(win, layer),
                  pl.BlockSpec((None, 2, 1, diff_qk), lambda b, i: (layer, 0, 0, 0)),
                  full(gmat),
                  pos_tab(rope[0]), pos_tab(rope[1]), pos_tab(rope[2]), pos_tab(rope[3]),
                  full(ret_tabs[0]), full(ret_tabs[1]), full(ret_tabs[2]), full(ret_tabs[3]),
                  s0_spec] + prev_specs,
        out_specs=att_specs + [tok(ret_v), pl.BlockSpec(state_blk, lambda b, i: (b, 0, 0, 0))],
        out_shape=att_shapes + [jax.ShapeDtypeStruct((n_b, seq, ret_v), BF16),
                                jax.ShapeDtypeStruct((n_b,) + state_blk[1:], F32)],
        scratch_shapes=[pltpu.VMEM(state_blk, F32)],
        compiler_params=_cparams(2),
        name="mix_in",
    )(x, mods, mods, g_norm, win, gqk, gmat, *rope, *ret_tabs, s0, *(prev_kv or ()))


def _lambda(wl_ref, lam_init):
    wl = wl_ref[...]
    a = jnp.sum(wl[0:1] * wl[1:2], axis=-1, keepdims=True)
    b = jnp.sum(wl[2:3] * wl[3:4], axis=-1, keepdims=True)
    return jnp.exp(a) - jnp.exp(b) + lam_init


def _softmax_step_t(s_t, v_t, m_ref, l_ref, acc_ref):
    m_prev = m_ref[...]
    m_new = jnp.maximum(m_prev, jnp.max(s_t, axis=0, keepdims=True))
    alpha = jnp.exp(m_prev - m_new)
    p = jnp.exp(s_t - m_new)
    l_ref[...] = alpha * l_ref[...] + jnp.sum(p, axis=0, keepdims=True)
    acc_ref[...] = alpha * acc_ref[...] + _dot(v_t, p.astype(BF16))
    m_ref[...] = m_new


def _attn_prompt_kernel(wl_ref, gsub_ref, qt_ref, k_ref, vt_ref, o_ref, *stats, lam_init):
    hw, seq = qt_ref.shape
    n_kt, _, tk = vt_ref.shape
    tq = tk
    gw = min(tq, ATTN_GROUP)
    n_g = tq // gw
    feat = lax.broadcasted_iota(jnp.int32, (hw, 1), 0)
    lam = _lambda(wl_ref, lam_init)

    def refs(sub, g):
        return stats[3 * (sub * n_g + g):3 * (sub * n_g + g) + 3]

    units = [(i, j, sub, g, tk if j < i else min(tk, (g + 1) * gw))
             for i in range(seq // tq) for j in range(i + 1) for sub in range(2) for g in range(n_g)]

    def scores(unit):
        i, j, sub, g, n_keys = unit
        q = qt_ref[:, i * tq + g * gw:i * tq + (g + 1) * gw]
        q = jnp.where(feat < hw // 2 if sub == 0 else feat >= hw // 2, q, jnp.zeros_like(q))
        s_t = _dot(k_ref[j * tk:j * tk + n_keys, :], q)
        if j == i:
            k_chunk = lax.broadcasted_iota(jnp.int32, (n_keys, gw), 0) // CHUNK
            q_chunk = (lax.broadcasted_iota(jnp.int32, (n_keys, gw), 1) + g * gw) // CHUNK
            s_t = jnp.where(k_chunk <= q_chunk, s_t, -jnp.inf)
        return s_t

    pending = [scores(u) for u in units[:ATTN_LOOKAHEAD]]
    for n, (i, j, sub, g, n_keys) in enumerate(units):
        if n + ATTN_LOOKAHEAD < len(units):
            pending.append(scores(units[n + ATTN_LOOKAHEAD]))
        m_ref, l_ref, acc_ref = refs(sub, g)
        if j == 0:
            m_ref[...] = jnp.full(m_ref.shape, -jnp.inf, F32)
            l_ref[...] = jnp.zeros(l_ref.shape, F32)
            acc_ref[...] = jnp.zeros(acc_ref.shape, F32)
        _softmax_step_t(pending.pop(0), vt_ref[j, :, :n_keys], m_ref, l_ref, acc_ref)
        if j == i and sub == 1 and g == n_g - 1:
            o_t = jnp.concatenate(
                [refs(0, c)[2][...] / refs(0, c)[1][...] - lam * (refs(1, c)[2][...] / refs(1, c)[1][...])
                 for c in range(n_g)], axis=1)
            o_t = (o_t * lax.rsqrt(jnp.mean(o_t * o_t, axis=0, keepdims=True) + EPS)
                   * gsub_ref[...] * (1.0 - lam_init))
            o_ref[i * tq:(i + 1) * tq, :] = jnp.transpose(o_t).astype(BF16)


def _attn_prompt(qt, k, vt, wl, gsub_col, lam_init, layer):
    n_b, width, seq = qt.shape
    n_kt, diff_v, tk = vt.shape[1:]
    hw, hv = width // DIFF_HEADS, diff_v // DIFF_HEADS
    gw = min(tk, ATTN_GROUP)
    assert seq == n_kt * tk and tk % CHUNK == 0 and tk % gw == 0
    return pl.pallas_call(
        functools.partial(_attn_prompt_kernel, lam_init=lam_init),
        grid=(n_b, DIFF_HEADS),
        in_specs=[pl.BlockSpec((None,) + wl.shape[1:], lambda b, h: (layer, 0, 0)),
                  pl.BlockSpec((None,) + gsub_col.shape[1:], lambda b, h: (layer, 0, 0)),
                  pl.BlockSpec((None, hw, seq), lambda b, h: (b, h, 0)),
                  pl.BlockSpec((None, seq, hw), lambda b, h: (b, 0, h)),
                  pl.BlockSpec((None, n_kt, hv, tk), lambda b, h: (b, 0, h, 0))],
        out_specs=pl.BlockSpec((None, seq, hv), lambda b, h: (b, 0, h)),
        out_shape=jax.ShapeDtypeStruct((n_b, seq, diff_v), BF16),
        scratch_shapes=[pltpu.VMEM((1, gw), F32), pltpu.VMEM((1, gw), F32), pltpu.VMEM((hv, gw), F32)]
        * (2 * (tk // gw)),
        compiler_params=_cparams(2),
        name="attn_prompt",
    )(wl, gsub_col, qt, k, vt)


def _stack_subheads(q):
    w = q.shape[-1]
    lane = lax.broadcasted_iota(jnp.int32, (1, w), 1)
    zero = jnp.zeros_like(q)
    return jnp.concatenate([jnp.where(lane < w // 2, q, zero), jnp.where(lane >= w // 2, q, zero)], axis=0)


def _softmax_step(s, v, m_ref, l_ref, acc_ref):
    m_prev = m_ref[...]
    m_new = jnp.maximum(m_prev, jnp.max(s, axis=-1, keepdims=True))
    alpha = jnp.exp(m_prev - m_new)
    p = jnp.exp(s - m_new)
    l_ref[...] = alpha * l_ref[...] + jnp.sum(p, axis=-1, keepdims=True)
    acc_ref[...] = alpha * acc_ref[...] + _dot(p.astype(BF16), v)
    m_ref[...] = m_new


def _attn_sample_kernel(wl_ref, gsub_ref, q_ref, kct_ref, vc_ref, kn_ref, vn_ref, o_ref,
                        m_ref, l_ref, acc_ref, *, lam_init):
    j = pl.program_id(1)
    hw = q_ref.shape[-1] // DIFF_HEADS
    hv = vc_ref.shape[-1]
    tk = kct_ref.shape[-1]

    @pl.when(j == 0)
    def _():
        m_ref[...] = jnp.full(m_ref.shape, -jnp.inf, F32)
        l_ref[...] = jnp.zeros(l_ref.shape, F32)
        acc_ref[...] = jnp.zeros(acc_ref.shape, F32)

    def head_q(h):
        return _stack_subheads(q_ref[:, h * hw:(h + 1) * hw])

    scores = [_dot(head_q(h), kct_ref[h * hw:(h + 1) * hw, :].astype(BF16)) for h in range(DIFF_HEADS)]
    for h in range(DIFF_HEADS):
        v = vc_ref[pl.ds(h, tk, stride=DIFF_HEADS), :].astype(BF16)
        _softmax_step(scores[h], v, m_ref.at[h], l_ref.at[h], acc_ref.at[h])

    @pl.when(j == pl.num_programs(1) - 1)
    def _():
        lam = _lambda(wl_ref, lam_init)
        for h in range(DIFF_HEADS):
            _softmax_step(_dot_nt(head_q(h), kn_ref[:, h * hw:(h + 1) * hw]), vn_ref[:, h * hv:(h + 1) * hv],
                          m_ref.at[h], l_ref.at[h], acc_ref.at[h])
            acc = acc_ref[h]
            l = l_ref[h]
            rows = acc.shape[0] // 2
            o = acc[:rows] / l[:rows] - lam * (acc[rows:] / l[rows:])
            o_ref[:, h * hv:(h + 1) * hv] = (_rms(o) * gsub_ref[...] * (1.0 - lam_init)).astype(BF16)


def _attn_sample(q, k_new, v_new, cache_kt, cache_vi, wl, gsub_row, lam_init, layer):
    n_b, seq, width = q.shape
    past = cache_kt.shape[-1]
    hv = cache_vi.shape[-1]
    vw = v_new.shape[-1]
    assert past % CHUNK == 0 and seq <= CHUNK
    tk = min(past, CACHE_TK)
    assert past % tk == 0
    return pl.pallas_call(
        functools.partial(_attn_sample_kernel, lam_init=lam_init),
        grid=(n_b, past // tk),
        in_specs=[pl.BlockSpec((None,) + wl.shape[1:], lambda b, j: (layer, 0, 0)),
                  pl.BlockSpec((None,) + gsub_row.shape[1:], lambda b, j: (layer, 0, 0)),
                  pl.BlockSpec((None, seq, width), lambda b, j: (b, 0, 0)),
                  pl.BlockSpec((None, None, width, tk), lambda b, j: (layer, b, 0, j)),
                  pl.BlockSpec((None, None, tk * DIFF_HEADS, hv), lambda b, j: (layer, b, j, 0)),
                  pl.BlockSpec((None, seq, width), lambda b, j: (b, 0, 0)),
                  pl.BlockSpec((None, seq, vw), lambda b, j: (b, 0, 0))],
        out_specs=pl.BlockSpec((None, seq, vw), lambda b, j: (b, 0, 0)),
        out_shape=jax.ShapeDtypeStruct((n_b, seq, vw), BF16),
        scratch_shapes=[pltpu.VMEM((DIFF_HEADS, 2 * seq, 1), F32), pltpu.VMEM((DIFF_HEADS, 2 * seq, 1), F32),
                        pltpu.VMEM((DIFF_HEADS, 2 * seq, hv), F32)],
        compiler_params=_cparams(2),
        name="attn_sample",
    )(wl, gsub_row, q, cache_kt, cache_vi, k_new, v_new)


def _mix_out_kernel(x_ref, or_ref, od_ref, gate1_ref, shift_ref, scale_ref, gate2_ref, g_ref,
                    wout_ref, wup_ref, wdn_ref, o_ref):
    n_seq, lt, d = x_ref.shape
    m = n_seq * lt
    rw = or_ref.shape[-1]
    mix = (_dot(or_ref[...].reshape(m, rw), wout_ref[:rw, :])
           + _dot(od_ref[...].reshape(m, od_ref.shape[-1]), wout_ref[rw:, :]))
    x3 = x_ref[...] + gate1_ref[...] * mix.reshape(n_seq, lt, d)
    o_ref[...] = _ffn_apply(x3, g_ref[...], shift_ref[...], scale_ref[...], gate2_ref[...], wup_ref, wdn_ref)


def _mix_out(x, o_r, o_d, mods, g_norm, wout, wup, wdn, layer):
    n_b, seq, d = x.shape
    n_seq, lt = _token_tiles(n_b, seq)

    def tok(width):
        return pl.BlockSpec((n_seq, lt, width), lambda b, i: (b, i, 0))

    return pl.pallas_call(
        _mix_out_kernel,
        grid=(n_b // n_seq, seq // lt),
        in_specs=[tok(d), tok(o_r.shape[-1]), tok(o_d.shape[-1]),
                  _mod_spec(layer, 5, n_seq, d), _mod_spec(layer, 6, n_seq, d),
                  _mod_spec(layer, 7, n_seq, d), _mod_spec(layer, 8, n_seq, d),
                  pl.BlockSpec((None, None, 1, d), lambda b, i: (layer, 2, 0, 0)),
                  _layer_weight(wout, layer), _layer_weight(wup, layer, 1), _layer_weight(wdn, layer, 1)],
        out_specs=tok(d),
        out_shape=jax.ShapeDtypeStruct(x.shape, F32),
        compiler_params=_cparams(2),
        name="mix_out",
    )(x, o_r, o_d, mods, mods, mods, mods, g_norm, wout, wup, wdn)


def kernel(x_prompt, x_sample, cache_k, cache_v, state_ret, c_prompt, c_sample, w_ada, b_ada, g_norm,
           w_ff_up, w_ff_down, w_in, w_out, g_qk, w_lambda, g_sub):
    depth = w_in.shape[0]
    n_p, seq_p, d = x_prompt.shape
    n_s, seq_s, _ = x_sample.shape
    past = cache_k.shape[2]
    dk_h, dv_h = state_ret.shape[-2], state_ret.shape[-1]
    sub = cache_k.shape[-1]
    hv = cache_v.shape[-1]
    ret_qk, ret_v = RET_HEADS * dk_h, RET_HEADS * dv_h
    diff_qk, diff_v = DIFF_HEADS * 2 * sub, DIFF_HEADS * hv
    sizes = (ret_qk, ret_v, diff_qk, diff_v)

    mods = _ada_mods(jnp.concatenate([c_prompt, c_sample], axis=0), w_ada, b_ada)
    mods = mods[:, :, :, None, :]
    mods_p, mods_s = mods[:, :, :n_p], mods[:, :, n_p:]

    rope_p = _rope_tables(jnp.arange(seq_p), dk_h, sub)
    rope_s = _rope_tables(past + jnp.arange(seq_s), dk_h, sub)
    rc_p, rc_s = min(RET_CHUNK, seq_p), min(RET_CHUNK, seq_s)
    tabs_p = _retention_tables(rc_p, dk_h, dv_h)
    tabs_s = _retention_tables(rc_s, dk_h, dv_h)
    group = np.arange(diff_qk) // sub
    gmat = jnp.asarray(group[:, None] == group[None, :], BF16)
    s0_p = jnp.zeros((n_p, RET_HEADS, dk_h, dv_h), F32)
    cache_kt = jnp.transpose(cache_k, (0, 1, 3, 4, 5, 2)).reshape(depth, n_s, diff_qk, past)
    cache_vi = cache_v.reshape(depth, n_s, past * DIFF_HEADS, hv)

    wup = w_ff_up.astype(BF16)
    wdn = w_ff_down.astype(BF16)
    win = w_in.astype(BF16)
    wout = w_out.astype(BF16)
    g_norm4 = g_norm[:, :, None, :]
    gqk = jnp.tile(g_qk, (1, 1, diff_qk // sub))[:, :, None, :]
    wl = w_lambda.astype(F32)
    gsub_row = g_sub[:, None, :]
    gsub_col = g_sub[:, :, None]

    yp, ys = x_prompt, x_sample
    kv_p, kv_s, sp, ssl = None, None, [], []
    for l in range(depth):
        lam_init = 0.8 - 0.6 * math.exp(-0.3 * l)

        yp = _ffn1(yp, mods_p, g_norm4, wup, wdn, l)
        *kv_p, q_t, k_a, v_t, o_r, s_new = _mix_in(yp, mods_p, g_norm4, win, gqk, gmat, rope_p, tabs_p,
                                                    s0_p, None, sizes, rc_p, l, True, kv_p)
        o_d = _attn_prompt(q_t, k_a, v_t, wl, gsub_col, lam_init, l)
        yp = _mix_out(yp, o_r, o_d, mods_p, g_norm4, wout, wup, wdn, l)
        sp.append(s_new)

        ys = _ffn1(ys, mods_s, g_norm4, wup, wdn, l)
        *kv_s, q_a, k_a, v_a, o_r, s_new = _mix_in(ys, mods_s, g_norm4, win, gqk, gmat, rope_s, tabs_s,
                                                    state_ret, l, sizes, rc_s, l, False, kv_s)
        o_d = _attn_sample(q_a, k_a, v_a, cache_kt, cache_vi, wl, gsub_row, lam_init, l)
        ys = _mix_out(ys, o_r, o_d, mods_s, g_norm4, wout, wup, wdn, l)
        ssl.append(s_new)

    k_prompt = jnp.transpose(kv_p[0].reshape(depth, n_p, DIFF_HEADS, 2, sub, seq_p), (0, 1, 5, 2, 3, 4))
    v_prompt = kv_p[1].reshape(depth, n_p, seq_p, DIFF_HEADS, hv)
    k_sample = kv_s[0].reshape(depth, n_s, seq_s, DIFF_HEADS, 2, sub)
    v_sample = kv_s[1].reshape(depth, n_s, seq_s, DIFF_HEADS, hv)
    return (yp, ys, k_prompt, v_prompt, jnp.stack(sp), k_sample, v_sample, jnp.stack(ssl))
```

```python
import functools
import math

import numpy as np
import jax
import jax.numpy as jnp
from jax import lax
from jax.experimental import pallas as pl
from jax.experimental.pallas import tpu as pltpu

F32 = jnp.float32
BF16 = jnp.bfloat16

CHUNK = 64
RET_HEADS = 4
DIFF_HEADS = 4
ROPE_THETA = 10000.0
EPS = 1e-6
TOKEN_TILE = 512
FF_CHUNK = 256
RET_CHUNK = 256
ATTN_GROUP = 256
ATTN_LOOKAHEAD = 3
CACHE_TK = 4096
VMEM_LIMIT = 56 * 1024 * 1024


def _cparams(n_axes):
    return pltpu.CompilerParams(dimension_semantics=("arbitrary",) * n_axes,
                                vmem_limit_bytes=VMEM_LIMIT)


def _resident(block_shape, index_map):
    return pl.BlockSpec(block_shape, index_map, pipeline_mode=pl.Buffered(1))


def _rms(x):
    return x * lax.rsqrt(jnp.mean(x * x, axis=-1, keepdims=True) + EPS)


def _dot(a, b):
    return jnp.dot(a, b, preferred_element_type=F32)


def _dot_nt(a, b):
    return lax.dot_general(a, b, (((1,), (1,)), ((), ())), preferred_element_type=F32)


def _split_bf16(x):
    hi = x.astype(BF16)
    return hi, (x - hi.astype(F32)).astype(BF16)


def _ada_kernel(c_ref, w_ref, b_ref, o_ref):
    n = c_ref.shape[0]
    c = c_ref[...]
    a_hi, a_lo = _split_bf16(c * jax.nn.sigmoid(c))
    w_hi, w_lo = _split_bf16(w_ref[...])
    a_both = jnp.concatenate([a_hi.astype(F32), a_lo.astype(F32)], axis=0).astype(BF16)
    first = _dot(a_both, w_hi)
    o_ref[...] = first[:n] + first[n:] + _dot(a_hi, w_lo) + b_ref[...]


def _ada_mods(c_all, w_ada, b_ada):
    depth, d, n_out = w_ada.shape
    n_seq = c_all.shape[0]
    n_blk = n_out // d
    return pl.pallas_call(
        _ada_kernel,
        grid=(depth, n_blk),
        in_specs=[
            pl.BlockSpec((n_seq, d), lambda l, j: (0, 0)),
            pl.BlockSpec((None, d, d), lambda l, j: (l, 0, j)),
            pl.BlockSpec((None, None, 1, d), lambda l, j: (l, j, 0, 0)),
        ],
        out_specs=pl.BlockSpec((None, None, n_seq, d), lambda l, j: (l, j, 0, 0)),
        out_shape=jax.ShapeDtypeStruct((depth, n_blk, n_seq, d), F32),
        compiler_params=_cparams(2),
        name="ada_mods",
    )(c_all, w_ada, b_ada.reshape(depth, n_blk, 1, d))


def _pre(x3, g, shift, scale):
    return _rms(x3) * g * (1.0 + scale) + shift


def _ffn_apply(x3, g, shift, scale, gate, wup_ref, wdn_ref):
    n_seq, lt, d = x3.shape
    d_ff = wdn_ref.shape[0]
    hb = _pre(x3, g, shift, scale).reshape(n_seq * lt, d).astype(BF16)
    acc = jnp.zeros((n_seq * lt, d), F32)
    for c in range(d_ff // FF_CHUNK):
        lo = c * FF_CHUNK
        a = _dot(hb, wup_ref[:, lo:lo + FF_CHUNK])
        b = _dot(hb, wup_ref[:, d_ff + lo:d_ff + lo + FF_CHUNK])
        act = (a * jax.nn.sigmoid(a) * b).astype(BF16)
        acc = acc + _dot(act, wdn_ref[lo:lo + FF_CHUNK, :])
    return x3 + (0.5 * gate) * acc.reshape(n_seq, lt, d)


def _ffn_kernel(x_ref, shift_ref, scale_ref, gate_ref, g_ref, wup_ref, wdn_ref, o_ref):
    o_ref[...] = _ffn_apply(x_ref[...], g_ref[...], shift_ref[...], scale_ref[...], gate_ref[...],
                            wup_ref, wdn_ref)


def _token_tiles(n_b, seq):
    lt = min(seq, TOKEN_TILE)
    n_seq = max(1, min(n_b, TOKEN_TILE // lt))
    assert seq % lt == 0 and n_b % n_seq == 0 and lt % 8 == 0
    return n_seq, lt


def _mod_spec(layer, idx, n_seq, d):
    return pl.BlockSpec((None, None, n_seq, 1, d), lambda b, i: (layer, idx, b, 0, 0))


def _layer_weight(w, *lead):
    n_lead = len(lead)
    return _resident((None,) * n_lead + w.shape[n_lead:], lambda b, i: lead + (0,) * (w.ndim - n_lead))


def _ffn1(x, mods, g_norm, wup, wdn, layer):
    n_b, seq, d = x.shape
    n_seq, lt = _token_tiles(n_b, seq)
    x_spec = pl.BlockSpec((n_seq, lt, d), lambda b, i: (b, i, 0))
    return pl.pallas_call(
        _ffn_kernel,
        grid=(n_b // n_seq, seq // lt),
        in_specs=[x_spec, _mod_spec(layer, 0, n_seq, d), _mod_spec(layer, 1, n_seq, d),
                  _mod_spec(layer, 2, n_seq, d),
                  pl.BlockSpec((None, None, 1, d), lambda b, i: (layer, 0, 0, 0)),
                  _layer_weight(wup, layer, 0), _layer_weight(wdn, layer, 0)],
        out_specs=x_spec,
        out_shape=jax.ShapeDtypeStruct(x.shape, F32),
        compiler_params=_cparams(2),
        name="ffn1",
    )(x, mods, mods, mods, g_norm, wup, wdn)


def _swap_pairs(x):
    w = x.shape[-1]
    lane = lax.broadcasted_iota(jnp.int32, (1, w), 1)
    return jnp.where((lane & 1) == 0, pltpu.roll(x, w - 1, 1), pltpu.roll(x, 1, 1))


def _swap_halves(x, half):
    w = x.shape[-1]
    lane = lax.broadcasted_iota(jnp.int32, (1, w), 1)
    return jnp.where((lane & (2 * half - 1)) < half, pltpu.roll(x, w - half, 1), pltpu.roll(x, half, 1))


def _mix_in_kernel(x_ref, shift_ref, scale_ref, g_ref, win_ref, gqk_ref, gmat_ref,
                   cosr_ref, sinr_ref, cosd_ref, sind_ref, dmask_ref, qdec_ref, kdec_ref, cdec_ref, s0_ref,
                   *rest, rc, sizes, prompt, n_prev):
    if n_prev:
        dkp_ref, dvp_ref = rest[:2]
        rest = rest[2:]
    dk_ref, dv_ref, qa_ref, ka_ref, va_ref, or_ref, snew_ref, s_scr = rest
    if n_prev:
        dk_ref[:n_prev] = dkp_ref[...]
        dv_ref[:n_prev] = dvp_ref[...]
    ret_qk, ret_v, diff_qk, diff_v = sizes
    n_seq, lt, d = x_ref.shape
    m = n_seq * lt
    dk_h = ret_qk // RET_HEADS
    dv_h = ret_v // RET_HEADS
    hv = diff_v // DIFF_HEADS
    hb = _pre(x_ref[...], g_ref[...], shift_ref[...], scale_ref[...]).reshape(m, d).astype(BF16)

    widths = (ret_qk, ret_qk, ret_v, ret_v, diff_qk, diff_qk, diff_v)

    def proj(n):
        lo = sum(widths[:n])
        return _dot(hb, win_ref[:, lo:lo + widths[n]])

    def rope(x, partner, cos_ref, sin_ref):
        w = x.shape[-1]
        y = x.reshape(n_seq, lt, w) * cos_ref[...] + partner.reshape(n_seq, lt, w) * sin_ref[...]
        return y.reshape(m, w)

    rq, rk = proj(0), proj(1)
    dq, dk = proj(4), proj(5)
    dv = proj(6)

    rq = rope(rq, _swap_pairs(rq), cosr_ref, sinr_ref)
    rk = rope(rk, _swap_pairs(rk), cosr_ref, sinr_ref) * (dk_h ** -0.5)
    chunks = [(s, c, s * lt + c * rc) for s in range(n_seq) for c in range(lt // rc)]
    kd_t = {(s, c): jnp.transpose(rk[r0:r0 + rc] * kdec_ref[...]).astype(BF16)
            for s, c, r0 in chunks}

    sub = diff_qk // (2 * DIFF_HEADS)
    qk = jnp.concatenate([dq, dk], axis=0)
    gmean = _dot((qk * qk).astype(BF16), gmat_ref[...]) * (1.0 / sub)
    rv, rg = proj(2), proj(3)
    qk = qk * lax.rsqrt(gmean + EPS)
    dq = qk[:m] * gqk_ref[0]
    dk = qk[m:] * gqk_ref[1]
    dq = rope(dq, _swap_halves(dq, sub // 2), cosd_ref, sind_ref) * (sub ** -0.5)
    dk = rope(dk, _swap_halves(dk, sub // 2), cosd_ref, sind_ref)

    for s in range(n_seq):
        rows = slice(s * lt, (s + 1) * lt)
        for h in range(DIFF_HEADS):
            dv_ref[n_prev, s, pl.ds(h, lt, stride=DIFF_HEADS), :] = dv[rows, h * hv:(h + 1) * hv]
        if prompt:
            dk_t = jnp.transpose(dk[rows])
            dk_ref[n_prev, s] = dk_t
            ka_ref[s] = dk[rows].astype(BF16)
            qa_ref[s] = jnp.transpose(dq[rows]).astype(BF16)
            va_ref[s] = jnp.transpose(dv[rows]).astype(BF16)
        else:
            dk_ref[n_prev, s] = dk[rows]
            ka_ref[s] = dk[rows].astype(BF16)
            qa_ref[s] = dq[rows].astype(BF16)
            va_ref[s] = dv[rows].astype(BF16)

    @pl.when(pl.program_id(1) == 0)
    def _():
        s_scr[...] = s0_ref[...]

    lane_q = lax.broadcasted_iota(jnp.int32, (1, ret_qk), 1)
    gate = rg * jax.nn.sigmoid(rg)
    rv_b = rv.astype(BF16)
    zero_blk = jnp.zeros((dk_h, dv_h), BF16)
    def head_cols(h, width):
        return slice(h * width, (h + 1) * width)

    att, upd = {}, {}
    for s, c, r0 in chunks:
        q = rq[r0:r0 + rc]
        q_heads = jnp.concatenate(
            [jnp.where((lane_q >= h * dk_h) & (lane_q < (h + 1) * dk_h), q, 0.0) for h in range(RET_HEADS)],
            axis=0).astype(BF16)
        att[s, c] = (_dot_nt(q_heads, rk[r0:r0 + rc].astype(BF16)) * dmask_ref[...]).astype(BF16)
    for s, c, r0 in chunks:
        upd[s, c] = [_dot(kd_t[s, c][head_cols(h, dk_h)], rv_b[r0:r0 + rc, head_cols(h, dv_h)])
                     for h in range(RET_HEADS)]
    for s in range(n_seq):
        state = [s_scr[s, h] for h in range(RET_HEADS)]
        for c in range(lt // rc):
            r0 = s * lt + c * rc
            s_diag = jnp.concatenate(
                [jnp.concatenate([state[h].astype(BF16) if g == h else zero_blk for g in range(RET_HEADS)], axis=1)
                 for h in range(RET_HEADS)], axis=0)
            inter = _dot((rq[r0:r0 + rc] * qdec_ref[...]).astype(BF16), s_diag)
            for h in range(RET_HEADS):
                cols = head_cols(h, dv_h)
                o = _dot(att[s, c][h * rc:(h + 1) * rc], rv_b[r0:r0 + rc, cols]) + inter[:, cols]
                or_ref[s, c * rc:(c + 1) * rc, cols] = (_rms(o) * gate[r0:r0 + rc, cols]).astype(BF16)
                state[h] = cdec_ref[h] * state[h] + upd[s, c][h]
        for h in range(RET_HEADS):
            s_scr[s, h] = state[h]
            snew_ref[s, h] = state[h]


def _retention_tables(rc, dk_h, dv_h):
    n_h = RET_HEADS
    log_g = jnp.log1p(-jnp.exp2(-5.0 - jnp.arange(n_h, dtype=F32)))
    idx = jnp.arange(rc, dtype=F32)
    rel = idx[:, None] - idx[None, :]
    dmask = jnp.where(rel >= 0, jnp.exp(log_g[:, None, None] * jnp.maximum(rel, 0.0)), 0.0)
    q_dec = jnp.exp(log_g[None, :] * (idx[:, None] + 1.0))
    k_dec = jnp.exp(log_g[None, :] * (rc - 1.0 - idx[:, None]))
    c_dec = jnp.exp(log_g * rc)
    return (dmask.reshape(n_h * rc, rc), jnp.repeat(q_dec, dk_h, axis=1), jnp.repeat(k_dec, dk_h, axis=1),
            jnp.broadcast_to(c_dec[:, None, None], (n_h, dk_h, dv_h)))


def _rope_tables(pos, dk_h, sub):
    posf = pos.astype(F32)[:, None]
    ret_freq = 1.0 / (ROPE_THETA ** jnp.linspace(0.0, 1.0, dk_h // 2, dtype=F32))
    ang = posf * ret_freq[None, :]
    sign = jnp.tile(jnp.array([-1.0, 1.0], F32), dk_h // 2)
    cos_r = jnp.tile(jnp.repeat(jnp.cos(ang), 2, axis=1), (1, RET_HEADS))
    sin_r = jnp.tile(jnp.repeat(jnp.sin(ang), 2, axis=1) * sign, (1, RET_HEADS))
    rope_freq = 1.0 / (ROPE_THETA ** (jnp.arange(0, sub, 2, dtype=F32) / sub))
    ang = posf * rope_freq[None, :]
    sign = jnp.concatenate([-jnp.ones((sub // 2,), F32), jnp.ones((sub // 2,), F32)])
    cos_d = jnp.tile(jnp.tile(jnp.cos(ang), (1, 2)), (1, 2 * DIFF_HEADS))
    sin_d = jnp.tile(jnp.tile(jnp.sin(ang), (1, 2)) * sign, (1, 2 * DIFF_HEADS))
    return cos_r, sin_r, cos_d, sin_d


def _mix_in(x, mods, g_norm, win, gqk, gmat, rope, ret_tabs, s0, s0_layer, sizes, rc, layer, prompt, prev_kv):
    n_b, seq, d = x.shape
    n_prev = 0 if prev_kv is None else prev_kv[0].shape[0]
    ret_qk, ret_v, diff_qk, diff_v = sizes
    n_seq, lt = _token_tiles(n_b, seq)
    assert lt % rc == 0
    dk_h, dv_h = ret_qk // RET_HEADS, ret_v // RET_HEADS
    hv = diff_v // DIFF_HEADS

    def tok(width):
        return pl.BlockSpec((n_seq, lt, width), lambda b, i: (b, i, 0))

    def feat(width):
        return pl.BlockSpec((n_seq, width, lt), lambda b, i: (b, 0, i))

    def full(a):
        return _resident(a.shape, lambda b, i, _n=a.ndim: (0,) * _n)

    def pos_tab(a):
        return pl.BlockSpec((lt, a.shape[1]), lambda b, i: (i, 0))

    state_blk = (n_seq, RET_HEADS, dk_h, dv_h)
    if s0_layer is None:
        s0_spec = pl.BlockSpec(state_blk, lambda b, i: (b, 0, 0, 0))
    else:
        s0_spec = pl.BlockSpec((None,) + state_blk, lambda b, i: (s0_layer, b, 0, 0, 0))
    dv_blk, dv_idx = (n_seq, lt * DIFF_HEADS, hv), (lambda b, i: (b, i, 0))
    dv_full = (n_b, seq * DIFF_HEADS, hv)
    if prompt:
        dk_blk, dk_idx, dk_full = (n_seq, diff_qk, lt), (lambda b, i: (b, 0, i)), (n_b, diff_qk, seq)
    else:
        dk_blk, dk_idx, dk_full = (n_seq, lt, diff_qk), (lambda b, i: (b, i, 0)), (n_b, seq, diff_qk)

    def stacked_spec(n, blk, idx):
        return pl.BlockSpec((n,) + blk, lambda b, i: (0,) + idx(b, i))

    kv_specs = [stacked_spec(n_prev + 1, dk_blk, dk_idx), stacked_spec(n_prev + 1, dv_blk, dv_idx)]
    kv_shapes = [jax.ShapeDtypeStruct((n_prev + 1,) + dk_full, F32),
                 jax.ShapeDtypeStruct((n_prev + 1,) + dv_full, F32)]
    prev_specs = [stacked_spec(n_prev, dk_blk, dk_idx), stacked_spec(n_prev, dv_blk, dv_idx)] if n_prev else []
    if prompt:
        att_specs = kv_specs + [feat(diff_qk), tok(diff_qk),
                                pl.BlockSpec((n_seq, None, diff_v, lt), lambda b, i: (b, i, 0, 0))]
        att_shapes = kv_shapes + [jax.ShapeDtypeStruct((n_b, diff_qk, seq), BF16),
                                  jax.ShapeDtypeStruct((n_b, seq, diff_qk), BF16),
                                  jax.ShapeDtypeStruct((n_b, seq // lt, diff_v, lt), BF16)]
    else:
        att_specs = kv_specs + [tok(diff_qk), tok(diff_qk), tok(diff_v)]
        att_shapes = kv_shapes + [jax.ShapeDtypeStruct((n_b, seq, diff_qk), BF16),
                                  jax.ShapeDtypeStruct((n_b, seq, diff_qk), BF16),
                                  jax.ShapeDtypeStruct((n_b, seq, diff_v), BF16)]
    return pl.pallas_call(
        functools.partial(_mix_in_kernel, rc=rc, sizes=sizes, prompt=prompt, n_prev=n_prev),
        grid=(n_b // n_seq, seq // lt),
        in_specs=[tok(d), _mod_spec(layer, 3, n_seq, d), _mod_spec(layer, 4, n_seq, d),
                  pl.BlockSpec((None, None, 1, d), lambda b, i: (layer, 1, 0, 0)),
                  _layer_weight(win, layer),
                  pl.BlockSpec((None, 2, 1, diff_qk), lambda b, i: (layer, 0, 0, 0)),
                  full(gmat),
                  pos_tab(rope[0]), pos_tab(rope[1]), pos_tab(rope[2]), pos_tab(rope[3]),
                  full(ret_tabs[0]), full(ret_tabs[1]), full(ret_tabs[2]), full(ret_tabs[3]),
                  s0_spec] + prev_specs,
        out_specs=att_specs + [tok(ret_v), pl.BlockSpec(state_blk, lambda b, i: (b, 0, 0, 0))],
        out_shape=att_shapes + [jax.ShapeDtypeStruct((n_b, seq, ret_v), BF16),
                                jax.ShapeDtypeStruct((n_b,) + state_blk[1:], F32)],
        scratch_shapes=[pltpu.VMEM(state_blk, F32)],
        compiler_params=_cparams(2),
        name="mix_in",
    )(x, mods, mods, g_norm, win, gqk, gmat, *rope, *ret_tabs, s0, *(prev_kv or ()))


def _lambda(wl_ref, lam_init):
    wl = wl_ref[...]
    a = jnp.sum(wl[0:1] * wl[1:2], axis=-1, keepdims=True)
    b = jnp.sum(wl[2:3] * wl[3:4], axis=-1, keepdims=True)
    return jnp.exp(a) - jnp.exp(b) + lam_init


def _softmax_step_t(s_t, v_t, m_ref, l_ref, acc_ref):
    m_prev = m_ref[...]
    m_new = jnp.maximum(m_prev, jnp.max(s_t, axis=0, keepdims=True))
    alpha = jnp.exp(m_prev - m_new)
    p = jnp.exp(s_t - m_new)
    l_ref[...] = alpha * l_ref[...] + jnp.sum(p, axis=0, keepdims=True)
    acc_ref[...] = alpha * acc_ref[...] + _dot(v_t, p.astype(BF16))
    m_ref[...] = m_new


def _attn_prompt_kernel(wl_ref, gsub_ref, qt_ref, k_ref, vt_ref, o_ref, *stats, lam_init):
    hw, seq = qt_ref.shape
    n_kt, _, tk = vt_ref.shape
    tq = tk
    gw = min(tq, ATTN_GROUP)
    n_g = tq // gw
    feat = lax.broadcasted_iota(jnp.int32, (hw, 1), 0)
    lam = _lambda(wl_ref, lam_init)
    k_chunk = lax.broadcasted_iota(jnp.int32, (gw, gw), 0) // CHUNK
    q_chunk = lax.broadcasted_iota(jnp.int32, (gw, gw), 1) // CHUNK
    diag_mask = jnp.where(k_chunk <= q_chunk, 0.0, -jnp.inf).astype(F32)

    def refs(sub, g):
        return stats[3 * (sub * n_g + g):3 * (sub * n_g + g) + 3]

    units = [(i, j, sub, g, tk if j < i else min(tk, (g + 1) * gw))
             for i in range(seq // tq) for j in range(i + 1) for sub in range(2) for g in range(n_g)]

    def scores(unit):
        i, j, sub, g, n_keys = unit
        q = qt_ref[:, i * tq + g * gw:i * tq + (g + 1) * gw]
        q = jnp.where(feat < hw // 2 if sub == 0 else feat >= hw // 2, q, jnp.zeros_like(q))
        s_t = _dot(k_ref[j * tk:j * tk + n_keys, :], q)
        if j == i:
            tail = s_t[n_keys - gw:] + diag_mask
            s_t = tail if n_keys == gw else jnp.concatenate([s_t[:n_keys - gw], tail], axis=0)
        return s_t

    pending = [scores(u) for u in units[:ATTN_LOOKAHEAD]]
    for n, (i, j, sub, g, n_keys) in enumerate(units):
        if n + ATTN_LOOKAHEAD < len(units):
            pending.append(scores(units[n + ATTN_LOOKAHEAD]))
        m_ref, l_ref, acc_ref = refs(sub, g)
        if j == 0:
            m_ref[...] = jnp.full(m_ref.shape, -jnp.inf, F32)
            l_ref[...] = jnp.zeros(l_ref.shape, F32)
            acc_ref[...] = jnp.zeros(acc_ref.shape, F32)
        _softmax_step_t(pending.pop(0), vt_ref[j, :, :n_keys], m_ref, l_ref, acc_ref)
        if j == i and sub == 1 and g == n_g - 1:
            o_t = jnp.concatenate(
                [refs(0, c)[2][...] / refs(0, c)[1][...] - lam * (refs(1, c)[2][...] / refs(1, c)[1][...])
                 for c in range(n_g)], axis=1)
            o_t = (o_t * lax.rsqrt(jnp.mean(o_t * o_t, axis=0, keepdims=True) + EPS)
                   * gsub_ref[...] * (1.0 - lam_init))
            o_ref[i * tq:(i + 1) * tq, :] = jnp.transpose(o_t).astype(BF16)


def _attn_prompt(qt, k, vt, wl, gsub_col, lam_init, layer):
    n_b, width, seq = qt.shape
    n_kt, diff_v, tk = vt.shape[1:]
    hw, hv = width // DIFF_HEADS, diff_v // DIFF_HEADS
    gw = min(tk, ATTN_GROUP)
    assert seq == n_kt * tk and tk % CHUNK == 0 and tk % gw == 0
    return pl.pallas_call(
        functools.partial(_attn_prompt_kernel, lam_init=lam_init),
        grid=(n_b, DIFF_HEADS),
        in_specs=[pl.BlockSpec((None,) + wl.shape[1:], lambda b, h: (layer, 0, 0)),
                  pl.BlockSpec((None,) + gsub_col.shape[1:], lambda b, h: (layer, 0, 0)),
                  pl.BlockSpec((None, hw, seq), lambda b, h: (b, h, 0)),
                  pl.BlockSpec((None, seq, hw), lambda b, h: (b, 0, h)),
                  pl.BlockSpec((None, n_kt, hv, tk), lambda b, h: (b, 0, h, 0))],
        out_specs=pl.BlockSpec((None, seq, hv), lambda b, h: (b, 0, h)),
        out_shape=jax.ShapeDtypeStruct((n_b, seq, diff_v), BF16),
        scratch_shapes=[pltpu.VMEM((1, gw), F32), pltpu.VMEM((1, gw), F32), pltpu.VMEM((hv, gw), F32)]
        * (2 * (tk // gw)),
        compiler_params=_cparams(2),
        name="attn_prompt",
    )(wl, gsub_col, qt, k, vt)


def _stack_subheads(q):
    w = q.shape[-1]
    lane = lax.broadcasted_iota(jnp.int32, (1, w), 1)
    zero = jnp.zeros_like(q)
    return jnp.concatenate([jnp.where(lane < w // 2, q, zero), jnp.where(lane >= w // 2, q, zero)], axis=0)


def _softmax_step(s, v, m_ref, l_ref, acc_ref):
    m_prev = m_ref[...]
    m_new = jnp.maximum(m_prev, jnp.max(s, axis=-1, keepdims=True))
    alpha = jnp.exp(m_prev - m_new)
    p = jnp.exp(s - m_new)
    l_ref[...] = alpha * l_ref[...] + jnp.sum(p, axis=-1, keepdims=True)
    acc_ref[...] = alpha * acc_ref[...] + _dot(p.astype(BF16), v)
    m_ref[...] = m_new


def _attn_sample_kernel(wl_ref, gsub_ref, q_ref, kct_ref, vc_ref, kn_ref, vn_ref, o_ref,
                        m_ref, l_ref, acc_ref, *, lam_init):
    j = pl.program_id(1)
    hw = q_ref.shape[-1] // DIFF_HEADS
    hv = vc_ref.shape[-1]
    tk = kct_ref.shape[-1]

    @pl.when(j == 0)
    def _():
        m_ref[...] = jnp.full(m_ref.shape, -jnp.inf, F32)
        l_ref[...] = jnp.zeros(l_ref.shape, F32)
        acc_ref[...] = jnp.zeros(acc_ref.shape, F32)

    def head_q(h):
        return _stack_subheads(q_ref[:, h * hw:(h + 1) * hw])

    scores = [_dot(head_q(h), kct_ref[h * hw:(h + 1) * hw, :].astype(BF16)) for h in range(DIFF_HEADS)]
    for h in range(DIFF_HEADS):
        v = vc_ref[pl.ds(h, tk, stride=DIFF_HEADS), :].astype(BF16)
        _softmax_step(scores[h], v, m_ref.at[h], l_ref.at[h], acc_ref.at[h])

    @pl.when(j == pl.num_programs(1) - 1)
    def _():
        lam = _lambda(wl_ref, lam_init)
        for h in range(DIFF_HEADS):
            _softmax_step(_dot_nt(head_q(h), kn_ref[:, h * hw:(h + 1) * hw]), vn_ref[:, h * hv:(h + 1) * hv],
                          m_ref.at[h], l_ref.at[h], acc_ref.at[h])
            acc = acc_ref[h]
            l = l_ref[h]
            rows = acc.shape[0] // 2
            o = acc[:rows] / l[:rows] - lam * (acc[rows:] / l[rows:])
            o_ref[:, h * hv:(h + 1) * hv] = (_rms(o) * gsub_ref[...] * (1.0 - lam_init)).astype(BF16)


def _attn_sample(q, k_new, v_new, cache_kt, cache_vi, wl, gsub_row, lam_init, layer):
    n_b, seq, width = q.shape
    past = cache_kt.shape[-1]
    hv = cache_vi.shape[-1]
    vw = v_new.shape[-1]
    assert past % CHUNK == 0 and seq <= CHUNK
    tk = min(past, CACHE_TK)
    assert past % tk == 0
    return pl.pallas_call(
        functools.partial(_attn_sample_kernel, lam_init=lam_init),
        grid=(n_b, past // tk),
        in_specs=[pl.BlockSpec((None,) + wl.shape[1:], lambda b, j: (layer, 0, 0)),
                  pl.BlockSpec((None,) + gsub_row.shape[1:], lambda b, j: (layer, 0, 0)),
                  pl.BlockSpec((None, seq, width), lambda b, j: (b, 0, 0)),
                  pl.BlockSpec((None, None, width, tk), lambda b, j: (layer, b, 0, j)),
                  pl.BlockSpec((None, None, tk * DIFF_HEADS, hv), lambda b, j: (layer, b, j, 0)),
                  pl.BlockSpec((None, seq, width), lambda b, j: (b, 0, 0)),
                  pl.BlockSpec((None, seq, vw), lambda b, j: (b, 0, 0))],
        out_specs=pl.BlockSpec((None, seq, vw), lambda b, j: (b, 0, 0)),
        out_shape=jax.ShapeDtypeStruct((n_b, seq, vw), BF16),
        scratch_shapes=[pltpu.VMEM((DIFF_HEADS, 2 * seq, 1), F32), pltpu.VMEM((DIFF_HEADS, 2 * seq, 1), F32),
                        pltpu.VMEM((DIFF_HEADS, 2 * seq, hv), F32)],
        compiler_params=_cparams(2),
        name="attn_sample",
    )(wl, gsub_row, q, cache_kt, cache_vi, k_new, v_new)


def _mix_out_kernel(x_ref, or_ref, od_ref, gate1_ref, shift_ref, scale_ref, gate2_ref, g_ref,
                    wout_ref, wup_ref, wdn_ref, o_ref):
    n_seq, lt, d = x_ref.shape
    m = n_seq * lt
    rw = or_ref.shape[-1]
    mix = (_dot(or_ref[...].reshape(m, rw), wout_ref[:rw, :])
           + _dot(od_ref[...].reshape(m, od_ref.shape[-1]), wout_ref[rw:, :]))
    x3 = x_ref[...] + gate1_ref[...] * mix.reshape(n_seq, lt, d)
    o_ref[...] = _ffn_apply(x3, g_ref[...], shift_ref[...], scale_ref[...], gate2_ref[...], wup_ref, wdn_ref)


def _mix_out(x, o_r, o_d, mods, g_norm, wout, wup, wdn, layer):
    n_b, seq, d = x.shape
    n_seq, lt = _token_tiles(n_b, seq)

    def tok(width):
        return pl.BlockSpec((n_seq, lt, width), lambda b, i: (b, i, 0))

    return pl.pallas_call(
        _mix_out_kernel,
        grid=(n_b // n_seq, seq // lt),
        in_specs=[tok(d), tok(o_r.shape[-1]), tok(o_d.shape[-1]),
                  _mod_spec(layer, 5, n_seq, d), _mod_spec(layer, 6, n_seq, d),
                  _mod_spec(layer, 7, n_seq, d), _mod_spec(layer, 8, n_seq, d),
                  pl.BlockSpec((None, None, 1, d), lambda b, i: (layer, 2, 0, 0)),
                  _layer_weight(wout, layer), _layer_weight(wup, layer, 1), _layer_weight(wdn, layer, 1)],
        out_specs=tok(d),
        out_shape=jax.ShapeDtypeStruct(x.shape, F32),
        compiler_params=_cparams(2),
        name="mix_out",
    )(x, o_r, o_d, mods, mods, mods, mods, g_norm, wout, wup, wdn)


def kernel(x_prompt, x_sample, cache_k, cache_v, state_ret, c_prompt, c_sample, w_ada, b_ada, g_norm,
           w_ff_up, w_ff_down, w_in, w_out, g_qk, w_lambda, g_sub):
    depth = w_in.shape[0]
    n_p, seq_p, d = x_prompt.shape
    n_s, seq_s, _ = x_sample.shape
    past = cache_k.shape[2]
    dk_h, dv_h = state_ret.shape[-2], state_ret.shape[-1]
    sub = cache_k.shape[-1]
    hv = cache_v.shape[-1]
    ret_qk, ret_v = RET_HEADS * dk_h, RET_HEADS * dv_h
    diff_qk, diff_v = DIFF_HEADS * 2 * sub, DIFF_HEADS * hv
    sizes = (ret_qk, ret_v, diff_qk, diff_v)

    mods = _ada_mods(jnp.concatenate([c_prompt, c_sample], axis=0), w_ada, b_ada)
    mods = mods[:, :, :, None, :]
    mods_p, mods_s = mods[:, :, :n_p], mods[:, :, n_p:]

    rope_p = _rope_tables(jnp.arange(seq_p), dk_h, sub)
    rope_s = _rope_tables(past + jnp.arange(seq_s), dk_h, sub)
    rc_p, rc_s = min(RET_CHUNK, seq_p), min(RET_CHUNK, seq_s)
    tabs_p = _retention_tables(rc_p, dk_h, dv_h)
    tabs_s = _retention_tables(rc_s, dk_h, dv_h)
    group = np.arange(diff_qk) // sub
    gmat = jnp.asarray(group[:, None] == group[None, :], BF16)
    s0_p = jnp.zeros((n_p, RET_HEADS, dk_h, dv_h), F32)
    cache_kt = jnp.transpose(cache_k, (0, 1, 3, 4, 5, 2)).reshape(depth, n_s, diff_qk, past)
    cache_vi = cache_v.reshape(depth, n_s, past * DIFF_HEADS, hv)

    wup = w_ff_up.astype(BF16)
    wdn = w_ff_down.astype(BF16)
    win = w_in.astype(BF16)
    wout = w_out.astype(BF16)
    g_norm4 = g_norm[:, :, None, :]
    gqk = jnp.tile(g_qk, (1, 1, diff_qk // sub))[:, :, None, :]
    wl = w_lambda.astype(F32)
    gsub_row = g_sub[:, None, :]
    gsub_col = g_sub[:, :, None]

    yp, ys = x_prompt, x_sample
    kv_p, kv_s, sp, ssl = None, None, [], []
    for l in range(depth):
        lam_init = 0.8 - 0.6 * math.exp(-0.3 * l)

        yp = _ffn1(yp, mods_p, g_norm4, wup, wdn, l)
        *kv_p, q_t, k_a, v_t, o_r, s_new = _mix_in(yp, mods_p, g_norm4, win, gqk, gmat, rope_p, tabs_p,
                                                    s0_p, None, sizes, rc_p, l, True, kv_p)
        o_d = _attn_prompt(q_t, k_a, v_t, wl, gsub_col, lam_init, l)
        yp = _mix_out(yp, o_r, o_d, mods_p, g_norm4, wout, wup, wdn, l)
        sp.append(s_new)

        ys = _ffn1(ys, mods_s, g_norm4, wup, wdn, l)
        *kv_s, q_a, k_a, v_a, o_r, s_new = _mix_in(ys, mods_s, g_norm4, win, gqk, gmat, rope_s, tabs_s,
                                                    state_ret, l, sizes, rc_s, l, False, kv_s)
        o_d = _attn_sample(q_a, k_a, v_a, cache_kt, cache_vi, wl, gsub_row, lam_init, l)
        ys = _mix_out(ys, o_r, o_d, mods_s, g_norm4, wout, wup, wdn, l)
        ssl.append(s_new)

    k_prompt = jnp.transpose(kv_p[0].reshape(depth, n_p, DIFF_HEADS, 2, sub, seq_p), (0, 1, 5, 2, 3, 4))
    v_prompt = kv_p[1].reshape(depth, n_p, seq_p, DIFF_HEADS, hv)
    k_sample = kv_s[0].reshape(depth, n_s, seq_s, DIFF_HEADS, 2, sub)
    v_sample = kv_s[1].reshape(depth, n_s, seq_s, DIFF_HEADS, hv)
    return (yp, ys, k_prompt, v_prompt, jnp.stack(sp), k_sample, v_sample, jnp.stack(ssl))
```

```python
import functools
import math

import numpy as np
import jax
import jax.numpy as jnp
from jax import lax
from jax.experimental import pallas as pl
from jax.experimental.pallas import tpu as pltpu

F32 = jnp.float32
BF16 = jnp.bfloat16

CHUNK = 64
RET_HEADS = 4
DIFF_HEADS = 4
ROPE_THETA = 10000.0
EPS = 1e-6
MIX_TILE = 512
FFN_TILE = 1024
OUT_TILE = 512
FF_CHUNK = 256
RET_CHUNK = 256
ATTN_GROUP = 256
ATTN_LOOKAHEAD = 3
CACHE_TK = 4096
VMEM_LIMIT = 56 * 1024 * 1024


def _cparams(n_axes):
    return pltpu.CompilerParams(dimension_semantics=("arbitrary",) * n_axes,
                                vmem_limit_bytes=VMEM_LIMIT)


def _resident(block_shape, index_map):
    return pl.BlockSpec(block_shape, index_map, pipeline_mode=pl.Buffered(1))


def _rms(x):
    return x * lax.rsqrt(jnp.mean(x * x, axis=-1, keepdims=True) + EPS)


def _dot(a, b):
    return jnp.dot(a, b, preferred_element_type=F32)


def _dot_nt(a, b):
    return lax.dot_general(a, b, (((1,), (1,)), ((), ())), preferred_element_type=F32)


def _split_bf16(x):
    hi = x.astype(BF16)
    return hi, (x - hi.astype(F32)).astype(BF16)


def _ada_kernel(c_ref, w_ref, b_ref, o_ref):
    n = c_ref.shape[0]
    c = c_ref[...]
    a_hi, a_lo = _split_bf16(c * jax.nn.sigmoid(c))
    w_hi, w_lo = _split_bf16(w_ref[...])
    a_both = jnp.concatenate([a_hi.astype(F32), a_lo.astype(F32)], axis=0).astype(BF16)
    first = _dot(a_both, w_hi)
    o_ref[...] = first[:n] + first[n:] + _dot(a_hi, w_lo) + b_ref[...]


def _ada_mods(c_all, w_ada, b_ada):
    depth, d, n_out = w_ada.shape
    n_seq = c_all.shape[0]
    n_blk = n_out // d
    return pl.pallas_call(
        _ada_kernel,
        grid=(depth, n_blk),
        in_specs=[
            pl.BlockSpec((n_seq, d), lambda l, j: (0, 0)),
            pl.BlockSpec((None, d, d), lambda l, j: (l, 0, j)),
            pl.BlockSpec((None, None, 1, d), lambda l, j: (l, j, 0, 0)),
        ],
        out_specs=pl.BlockSpec((None, None, n_seq, d), lambda l, j: (l, j, 0, 0)),
        out_shape=jax.ShapeDtypeStruct((depth, n_blk, n_seq, d), F32),
        compiler_params=_cparams(2),
        name="ada_mods",
    )(c_all, w_ada, b_ada.reshape(depth, n_blk, 1, d))


def _pre(x3, g, shift, scale):
    return _rms(x3) * g * (1.0 + scale) + shift


def _ffn_apply(x3, g, shift, scale, gate, wup_ref, wdn_ref):
    n_seq, lt, d = x3.shape
    d_ff = wdn_ref.shape[0]
    hb = _pre(x3, g, shift, scale).reshape(n_seq * lt, d).astype(BF16)
    acc = jnp.zeros((n_seq * lt, d), F32)
    for c in range(d_ff // FF_CHUNK):
        lo = c * FF_CHUNK
        a = _dot(hb, wup_ref[:, lo:lo + FF_CHUNK])
        b = _dot(hb, wup_ref[:, d_ff + lo:d_ff + lo + FF_CHUNK])
        act = (a * jax.nn.sigmoid(a) * b).astype(BF16)
        acc = acc + _dot(act, wdn_ref[lo:lo + FF_CHUNK, :])
    return x3 + (0.5 * gate) * acc.reshape(n_seq, lt, d)


def _ffn_kernel(x_ref, shift_ref, scale_ref, gate_ref, g_ref, wup_ref, wdn_ref, o_ref):
    o_ref[...] = _ffn_apply(x_ref[...], g_ref[...], shift_ref[...], scale_ref[...], gate_ref[...],
                            wup_ref, wdn_ref)


def _token_tiles(n_b, seq, tile):
    lt = min(seq, tile)
    n_seq = max(1, min(n_b, tile // lt))
    assert seq % lt == 0 and n_b % n_seq == 0 and lt % 8 == 0
    return n_seq, lt


def _mod_spec(layer, idx, n_seq, d):
    return pl.BlockSpec((None, None, n_seq, 1, d), lambda b, i: (layer, idx, b, 0, 0))


def _layer_weight(w, *lead):
    n_lead = len(lead)
    return _resident((None,) * n_lead + w.shape[n_lead:], lambda b, i: lead + (0,) * (w.ndim - n_lead))


def _ffn1(x, mods, g_norm, wup, wdn, layer):
    n_b, seq, d = x.shape
    n_seq, lt = _token_tiles(n_b, seq, FFN_TILE)
    x_spec = pl.BlockSpec((n_seq, lt, d), lambda b, i: (b, i, 0))
    return pl.pallas_call(
        _ffn_kernel,
        grid=(n_b // n_seq, seq // lt),
        in_specs=[x_spec, _mod_spec(layer, 0, n_seq, d), _mod_spec(layer, 1, n_seq, d),
                  _mod_spec(layer, 2, n_seq, d),
                  pl.BlockSpec((None, None, 1, d), lambda b, i: (layer, 0, 0, 0)),
                  _layer_weight(wup, layer, 0), _layer_weight(wdn, layer, 0)],
        out_specs=x_spec,
        out_shape=jax.ShapeDtypeStruct(x.shape, F32),
        compiler_params=_cparams(2),
        name="ffn1",
    )(x, mods, mods, mods, g_norm, wup, wdn)


def _swap_pairs(x):
    w = x.shape[-1]
    lane = lax.broadcasted_iota(jnp.int32, (1, w), 1)
    return jnp.where((lane & 1) == 0, pltpu.roll(x, w - 1, 1), pltpu.roll(x, 1, 1))


def _swap_halves(x, half):
    w = x.shape[-1]
    lane = lax.broadcasted_iota(jnp.int32, (1, w), 1)
    return jnp.where((lane & (2 * half - 1)) < half, pltpu.roll(x, w - half, 1), pltpu.roll(x, half, 1))


def _mix_in_kernel(x_ref, shift_ref, scale_ref, g_ref, win_ref, gqk_ref, gmat_ref,
                   cosr_ref, sinr_ref, cosd_ref, sind_ref, dmask_ref, qdec_ref, kdec_ref, cdec_ref, s0_ref,
                   *rest, rc, sizes, prompt, n_prev):
    if n_prev:
        dkp_ref, dvp_ref = rest[:2]
        rest = rest[2:]
    dk_ref, dv_ref, qa_ref, ka_ref, va_ref, or_ref, snew_ref, s_scr = rest
    @pl.when(pl.program_id(1) == 0)
    def _():
        s_scr[...] = s0_ref[...]

    if n_prev:
        dk_ref[:n_prev] = dkp_ref[...]
        dv_ref[:n_prev] = dvp_ref[...]
    ret_qk, ret_v, diff_qk, diff_v = sizes
    n_seq, lt, d = x_ref.shape
    m = n_seq * lt
    dk_h = ret_qk // RET_HEADS
    dv_h = ret_v // RET_HEADS
    hv = diff_v // DIFF_HEADS
    hb = _pre(x_ref[...], g_ref[...], shift_ref[...], scale_ref[...]).reshape(m, d).astype(BF16)

    widths = (ret_qk, ret_qk, ret_v, ret_v, diff_qk, diff_qk, diff_v)

    def proj(n):
        lo = sum(widths[:n])
        return _dot(hb, win_ref[:, lo:lo + widths[n]])

    def rope(x, partner, cos_ref, sin_ref):
        w = x.shape[-1]
        y = x.reshape(n_seq, lt, w) * cos_ref[...] + partner.reshape(n_seq, lt, w) * sin_ref[...]
        return y.reshape(m, w)

    rq, rk = proj(0), proj(1)
    dq, dk = proj(4), proj(5)
    dv = proj(6)

    rq = rope(rq, _swap_pairs(rq), cosr_ref, sinr_ref)
    rk = rope(rk, _swap_pairs(rk), cosr_ref, sinr_ref) * (dk_h ** -0.5)
    chunks = [(s, c, s * lt + c * rc) for s in range(n_seq) for c in range(lt // rc)]
    kd_t = {(s, c): jnp.transpose(rk[r0:r0 + rc] * kdec_ref[...]).astype(BF16)
            for s, c, r0 in chunks}

    sub = diff_qk // (2 * DIFF_HEADS)
    qk = jnp.concatenate([dq, dk], axis=0)
    gmean = _dot((qk * qk).astype(BF16), gmat_ref[...]) * (1.0 / sub)
    rv, rg = proj(2), proj(3)
    qk = qk * lax.rsqrt(gmean + EPS)
    dq = qk[:m] * gqk_ref[0]
    dk = qk[m:] * gqk_ref[1]
    dq = rope(dq, _swap_halves(dq, sub // 2), cosd_ref, sind_ref) * (sub ** -0.5)
    dk = rope(dk, _swap_halves(dk, sub // 2), cosd_ref, sind_ref)

    for s in range(n_seq):
        rows = slice(s * lt, (s + 1) * lt)
        for h in range(DIFF_HEADS):
            dv_ref[n_prev, s, pl.ds(h, lt, stride=DIFF_HEADS), :] = dv[rows, h * hv:(h + 1) * hv]
        if prompt:
            dk_t = jnp.transpose(dk[rows])
            dk_ref[n_prev, s] = dk_t
            ka_ref[s] = dk[rows].astype(BF16)
            qa_ref[s] = jnp.transpose(dq[rows]).astype(BF16)
            va_ref[s] = jnp.transpose(dv[rows]).astype(BF16)
        else:
            dk_ref[n_prev, s] = dk[rows]
            ka_ref[s] = dk[rows].astype(BF16)
            qa_ref[s] = dq[rows].astype(BF16)
            va_ref[s] = dv[rows].astype(BF16)

    lane_q =lax.broadcasted_iota(jnp.int32, (1, ret_qk), 1)
    gate = rg * jax.nn.sigmoid(rg)
    rv_b = rv.astype(BF16)
    zero_blk = jnp.zeros((dk_h, dv_h), BF16)
    def head_cols(h, width):
        return slice(h * width, (h + 1) * width)

    att, upd = {}, {}
    for s, c, r0 in chunks:
        q = rq[r0:r0 + rc]
        q_heads = jnp.concatenate(
            [jnp.where((lane_q >= h * dk_h) & (lane_q < (h + 1) * dk_h), q, 0.0) for h in range(RET_HEADS)],
            axis=0).astype(BF16)
        att[s, c] = (_dot_nt(q_heads, rk[r0:r0 + rc].astype(BF16)) * dmask_ref[...]).astype(BF16)
    for s, c, r0 in chunks:
        upd[s, c] = [_dot(kd_t[s, c][head_cols(h, dk_h)], rv_b[r0:r0 + rc, head_cols(h, dv_h)])
                     for h in range(RET_HEADS)]
    for s in range(n_seq):
        state = [s_scr[s, h] for h in range(RET_HEADS)]
        for c in range(lt // rc):
            r0 = s * lt + c * rc
            s_diag = jnp.concatenate(
                [jnp.concatenate([state[h].astype(BF16) if g == h else zero_blk for g in range(RET_HEADS)], axis=1)
                 for h in range(RET_HEADS)], axis=0)
            inter = _dot((rq[r0:r0 + rc] * qdec_ref[...]).astype(BF16), s_diag)
            for h in range(RET_HEADS):
                cols = head_cols(h, dv_h)
                o = _dot(att[s, c][h * rc:(h + 1) * rc], rv_b[r0:r0 + rc, cols]) + inter[:, cols]
                or_ref[s, c * rc:(c + 1) * rc, cols] = (_rms(o) * gate[r0:r0 + rc, cols]).astype(BF16)
                state[h] = cdec_ref[h] * state[h] + upd[s, c][h]
        for h in range(RET_HEADS):
            s_scr[s, h] = state[h]
            snew_ref[s, h] = state[h]


def _retention_tables(rc, dk_h, dv_h):
    n_h = RET_HEADS
    log_g = jnp.log1p(-jnp.exp2(-5.0 - jnp.arange(n_h, dtype=F32)))
    idx = jnp.arange(rc, dtype=F32)
    rel = idx[:, None] - idx[None, :]
    dmask = jnp.where(rel >= 0, jnp.exp(log_g[:, None, None] * jnp.maximum(rel, 0.0)), 0.0)
    q_dec = jnp.exp(log_g[None, :] * (idx[:, None] + 1.0))
    k_dec = jnp.exp(log_g[None, :] * (rc - 1.0 - idx[:, None]))
    c_dec = jnp.exp(log_g * rc)
    return (dmask.reshape(n_h * rc, rc), jnp.repeat(q_dec, dk_h, axis=1), jnp.repeat(k_dec, dk_h, axis=1),
            jnp.broadcast_to(c_dec[:, None, None], (n_h, dk_h, dv_h)))


def _rope_tables(pos, dk_h, sub):
    posf = pos.astype(F32)[:, None]
    ret_freq = 1.0 / (ROPE_THETA ** jnp.linspace(0.0, 1.0, dk_h // 2, dtype=F32))
    ang = posf * ret_freq[None, :]
    sign = jnp.tile(jnp.array([-1.0, 1.0], F32), dk_h // 2)
    cos_r = jnp.tile(jnp.repeat(jnp.cos(ang), 2, axis=1), (1, RET_HEADS))
    sin_r = jnp.tile(jnp.repeat(jnp.sin(ang), 2, axis=1) * sign, (1, RET_HEADS))
    rope_freq = 1.0 / (ROPE_THETA ** (jnp.arange(0, sub, 2, dtype=F32) / sub))
    ang = posf * rope_freq[None, :]
    sign = jnp.concatenate([-jnp.ones((sub // 2,), F32), jnp.ones((sub // 2,), F32)])
    cos_d = jnp.tile(jnp.tile(jnp.cos(ang), (1, 2)), (1, 2 * DIFF_HEADS))
    sin_d = jnp.tile(jnp.tile(jnp.sin(ang), (1, 2)) * sign, (1, 2 * DIFF_HEADS))
    return cos_r, sin_r, cos_d, sin_d


def _mix_in(x, mods, g_norm, win, gqk, gmat, rope, ret_tabs, s0, s0_layer, sizes, rc, layer, prompt, prev_kv):
    n_b, seq, d = x.shape
    n_prev = 0 if prev_kv is None else prev_kv[0].shape[0]
    ret_qk, ret_v, diff_qk, diff_v = sizes
    n_seq, lt = _token_tiles(n_b, seq, MIX_TILE)
    assert lt % rc == 0
    dk_h, dv_h = ret_qk // RET_HEADS, ret_v // RET_HEADS
    hv = diff_v // DIFF_HEADS

    def tok(width):
        return pl.BlockSpec((n_seq, lt, width), lambda b, i: (b, i, 0))

    def feat(width):
        return pl.BlockSpec((n_seq, width, lt), lambda b, i: (b, 0, i))

    def full(a):
        return _resident(a.shape, lambda b, i, _n=a.ndim: (0,) * _n)

    def pos_tab(a):
        return pl.BlockSpec((lt, a.shape[1]), lambda b, i: (i, 0))

    state_blk = (n_seq, RET_HEADS, dk_h, dv_h)
    if s0_layer is None:
        s0_spec = pl.BlockSpec(state_blk, lambda b, i: (b, 0, 0, 0))
    else:
        s0_spec = pl.BlockSpec((None,) + state_blk, lambda b, i: (s0_layer, b, 0, 0, 0))
    dv_blk, dv_idx = (n_seq, lt * DIFF_HEADS, hv), (lambda b, i: (b, i, 0))
    dv_full = (n_b, seq * DIFF_HEADS, hv)
    if prompt:
        dk_blk, dk_idx, dk_full = (n_seq, diff_qk, lt), (lambda b, i: (b, 0, i)), (n_b, diff_qk, seq)
    else:
        dk_blk, dk_idx, dk_full = (n_seq, lt, diff_qk), (lambda b, i: (b, i, 0)), (n_b, seq, diff_qk)

    def stacked_spec(n, blk, idx):
        return pl.BlockSpec((n,) + blk, lambda b, i: (0,) + idx(b, i))

    kv_specs = [stacked_spec(n_prev + 1, dk_blk, dk_idx), stacked_spec(n_prev + 1, dv_blk, dv_idx)]
    kv_shapes = [jax.ShapeDtypeStruct((n_prev + 1,) + dk_full, F32),
                 jax.ShapeDtypeStruct((n_prev + 1,) + dv_full, F32)]
    prev_specs = [stacked_spec(n_prev, dk_blk, dk_idx), stacked_spec(n_prev, dv_blk, dv_idx)] if n_prev else []
    if prompt:
        att_specs = kv_specs + [feat(diff_qk), tok(diff_qk),
                                pl.BlockSpec((n_seq, None, diff_v, lt), lambda b, i: (b, i, 0, 0))]
        att_shapes = kv_shapes + [jax.ShapeDtypeStruct((n_b, diff_qk, seq), BF16),
                                  jax.ShapeDtypeStruct((n_b, seq, diff_qk), BF16),
                                  jax.ShapeDtypeStruct((n_b, seq // lt, diff_v, lt), BF16)]
    else:
        att_specs = kv_specs + [tok(diff_qk), tok(diff_qk), tok(diff_v)]
        att_shapes = kv_shapes + [jax.ShapeDtypeStruct((n_b, seq, diff_qk), BF16),
                                  jax.ShapeDtypeStruct((n_b, seq, diff_qk), BF16),
                                  jax.ShapeDtypeStruct((n_b, seq, diff_v), BF16)]
    return pl.pallas_call(
        functools.partial(_mix_in_kernel, rc=rc, sizes=sizes, prompt=prompt, n_prev=n_prev),
        grid=(n_b // n_seq, seq // lt),
        in_specs=[tok(d), _mod_spec(layer, 3, n_seq, d), _mod_spec(layer, 4, n_seq, d),
                  pl.BlockSpec((None, None, 1, d), lambda b, i: (layer, 1, 0, 0)),
                  _layer_weight(win, layer),
                  pl.BlockSpec((None, 2, 1, diff_qk), lambda b, i: (layer, 0, 0, 0)),
                  full(gmat),
                  pos_tab(rope[0]), pos_tab(rope[1]), pos_tab(rope[2]), pos_tab(rope[3]),
                  full(ret_tabs[0]), full(ret_tabs[1]), full(ret_tabs[2]), full(ret_tabs[3]),
                  s0_spec] + prev_specs,
        out_specs=att_specs + [tok(ret_v), pl.BlockSpec(state_blk, lambda b, i: (b, 0, 0, 0))],
        out_shape=att_shapes + [jax.ShapeDtypeStruct((n_b, seq, ret_v), BF16),
                                jax.ShapeDtypeStruct((n_b,) + state_blk[1:], F32)],
        scratch_shapes=[pltpu.VMEM(state_blk, F32)],
        compiler_params=_cparams(2),
        name="mix_in",
    )(x, mods, mods, g_norm, win, gqk, gmat, *rope, *ret_tabs, s0, *(prev_kv or ()))


def _lambda(wl_ref, lam_init):
    wl = wl_ref[...]
    a = jnp.sum(wl[0:1] * wl[1:2], axis=-1, keepdims=True)
    b = jnp.sum(wl[2:3] * wl[3:4], axis=-1, keepdims=True)
    return jnp.exp(a) - jnp.exp(b) + lam_init


def _softmax_step_t(s_t, v_t, m_ref, l_ref, acc_ref):
    m_prev = m_ref[...]
    m_new = jnp.maximum(m_prev, jnp.max(s_t, axis=0, keepdims=True))
    alpha = jnp.exp(m_prev - m_new)
    p = jnp.exp(s_t - m_new)
    l_ref[...] = alpha * l_ref[...] + jnp.sum(p, axis=0, keepdims=True)
    acc_ref[...] = alpha * acc_ref[...] + _dot(v_t, p.astype(BF16))
    m_ref[...] = m_new


def _attn_prompt_kernel(wl_ref, gsub_ref, qt_ref, k_ref, vt_ref, o_ref, *stats, lam_init):
    hw, seq = qt_ref.shape
    n_kt, _, tk = vt_ref.shape
    tq = tk
    gw = min(tq, ATTN_GROUP)
    n_g = tq // gw
    feat = lax.broadcasted_iota(jnp.int32, (hw, 1), 0)
    lam = _lambda(wl_ref, lam_init)
    k_chunk = lax.broadcasted_iota(jnp.int32, (gw, gw), 0) // CHUNK
    q_chunk = lax.broadcasted_iota(jnp.int32, (gw, gw), 1) // CHUNK
    diag_mask = jnp.where(k_chunk <= q_chunk, 0.0, -jnp.inf).astype(F32)

    def refs(sub, g):
        return stats[3 * (sub * n_g + g):3 * (sub * n_g + g) + 3]

    units = [(i, j, sub, g, tk if j < i else min(tk, (g + 1) * gw))
             for i in range(seq // tq) for j in range(i + 1) for sub in range(2) for g in range(n_g)]

    def scores(unit):
        i, j, sub, g, n_keys = unit
        q = qt_ref[:, i * tq + g * gw:i * tq + (g + 1) * gw]
        q = jnp.where(feat < hw // 2 if sub == 0 else feat >= hw // 2, q, jnp.zeros_like(q))
        s_t = _dot(k_ref[j * tk:j * tk + n_keys, :], q)
        if j == i:
            tail = s_t[n_keys - gw:] + diag_mask
            s_t = tail if n_keys == gw else jnp.concatenate([s_t[:n_keys - gw], tail], axis=0)
        return s_t

    pending = [scores(u) for u in units[:ATTN_LOOKAHEAD]]
    for n, (i, j, sub, g, n_keys) in enumerate(units):
        if n + ATTN_LOOKAHEAD < len(units):
            pending.append(scores(units[n + ATTN_LOOKAHEAD]))
        m_ref, l_ref, acc_ref = refs(sub, g)
        if j == 0:
            m_ref[...] = jnp.full(m_ref.shape, -jnp.inf, F32)
            l_ref[...] = jnp.zeros(l_ref.shape, F32)
            acc_ref[...] = jnp.zeros(acc_ref.shape, F32)
        _softmax_step_t(pending.pop(0), vt_ref[j, :, :n_keys], m_ref, l_ref, acc_ref)
        if j == i and sub == 1 and g == n_g - 1:
            o_t = jnp.concatenate(
                [refs(0, c)[2][...] / refs(0, c)[1][...] - lam * (refs(1, c)[2][...] / refs(1, c)[1][...])
                 for c in range(n_g)], axis=1)
            o_t = (o_t * lax.rsqrt(jnp.mean(o_t * o_t, axis=0, keepdims=True) + EPS)
                   * gsub_ref[...] * (1.0 - lam_init))
            o_ref[i * tq:(i + 1) * tq, :] = jnp.transpose(o_t).astype(BF16)


def _attn_prompt(qt, k, vt, wl, gsub_col, lam_init, layer):
    n_b, width, seq = qt.shape
    n_kt, diff_v, tk = vt.shape[1:]
    hw, hv = width // DIFF_HEADS, diff_v // DIFF_HEADS
    gw = min(tk, ATTN_GROUP)
    assert seq == n_kt * tk and tk % CHUNK == 0 and tk % gw == 0
    return pl.pallas_call(
        functools.partial(_attn_prompt_kernel, lam_init=lam_init),
        grid=(n_b, DIFF_HEADS),
        in_specs=[pl.BlockSpec((None,) + wl.shape[1:], lambda b, h: (layer, 0, 0)),
                  pl.BlockSpec((None,) + gsub_col.shape[1:], lambda b, h: (layer, 0, 0)),
                  pl.BlockSpec((None, hw, seq), lambda b, h: (b, h, 0)),
                  pl.BlockSpec((None, seq, hw), lambda b, h: (b, 0, h)),
                  pl.BlockSpec((None, n_kt, hv, tk), lambda b, h: (b, 0, h, 0))],
        out_specs=pl.BlockSpec((None, seq, hv), lambda b, h: (b, 0, h)),
        out_shape=jax.ShapeDtypeStruct((n_b, seq, diff_v), BF16),
        scratch_shapes=[pltpu.VMEM((1, gw), F32), pltpu.VMEM((1, gw), F32), pltpu.VMEM((hv, gw), F32)]
        * (2 * (tk // gw)),
        compiler_params=_cparams(2),
        name="attn_prompt",
    )(wl, gsub_col, qt, k, vt)


def _stack_subheads(q):
    w = q.shape[-1]
    lane = lax.broadcasted_iota(jnp.int32, (1, w), 1)
    zero = jnp.zeros_like(q)
    return jnp.concatenate([jnp.where(lane < w // 2, q, zero), jnp.where(lane >= w // 2, q, zero)], axis=0)


def _softmax_step(s, v, m_ref, l_ref, acc_ref):
    m_prev = m_ref[...]
    m_new = jnp.maximum(m_prev, jnp.max(s, axis=-1, keepdims=True))
    alpha = jnp.exp(m_prev - m_new)
    p = jnp.exp(s - m_new)
    l_ref[...] = alpha * l_ref[...] + jnp.sum(p, axis=-1, keepdims=True)
    acc_ref[...] = alpha * acc_ref[...] + _dot(p.astype(BF16), v)
    m_ref[...] = m_new


def _attn_sample_kernel(wl_ref, gsub_ref, q_ref, kct_ref, vc_ref, kn_ref, vn_ref, o_ref,
                        m_ref, l_ref, acc_ref, *, lam_init, single_tile):
    j = pl.program_id(1)
    hw = q_ref.shape[-1] // DIFF_HEADS
    hv = vc_ref.shape[-1]
    tk = kct_ref.shape[-1]

    def head_q(h):
        return _stack_subheads(q_ref[:, h * hw:(h + 1) * hw])

    def cache_v(h):
        return vc_ref[pl.ds(h, tk, stride=DIFF_HEADS), :].astype(BF16)

    def new_scores(h):
        return _dot_nt(head_q(h), kn_ref[:, h * hw:(h + 1) * hw])

    def finish(h, acc, l, lam):
        rows = acc.shape[0] // 2
        o = acc[:rows] / l[:rows] - lam * (acc[rows:] / l[rows:])
        o_ref[:, h * hv:(h + 1) * hv] = (_rms(o) * gsub_ref[...] * (1.0 - lam_init)).astype(BF16)

    scores = [_dot(head_q(h), kct_ref[h * hw:(h + 1) * hw, :].astype(BF16)) for h in range(DIFF_HEADS)]

    if single_tile:
        lam = _lambda(wl_ref, lam_init)
        s_new = [new_scores(h) for h in range(DIFF_HEADS)]
        for h in range(DIFF_HEADS):
            m = jnp.maximum(jnp.max(scores[h], axis=-1, keepdims=True), jnp.max(s_new[h], axis=-1, keepdims=True))
            p_c = jnp.exp(scores[h] - m)
            p_n = jnp.exp(s_new[h] - m)
            l = jnp.sum(p_c, axis=-1, keepdims=True) + jnp.sum(p_n, axis=-1, keepdims=True)
            acc = _dot(p_c.astype(BF16), cache_v(h)) + _dot(p_n.astype(BF16), vn_ref[:, h * hv:(h + 1) * hv])
            finish(h, acc, l, lam)
        return

    @pl.when(j == 0)
    def _():
        m_ref[...] = jnp.full(m_ref.shape, -jnp.inf, F32)
        l_ref[...] = jnp.zeros(l_ref.shape, F32)
        acc_ref[...] = jnp.zeros(acc_ref.shape, F32)

    for h in range(DIFF_HEADS):
        _softmax_step(scores[h], cache_v(h), m_ref.at[h], l_ref.at[h], acc_ref.at[h])

    @pl.when(j == pl.num_programs(1) - 1)
    def _():
        lam = _lambda(wl_ref, lam_init)
        for h in range(DIFF_HEADS):
            _softmax_step(new_scores(h), vn_ref[:, h * hv:(h + 1) * hv], m_ref.at[h], l_ref.at[h], acc_ref.at[h])
            finish(h, acc_ref[h], l_ref[h], lam)


def _attn_sample(q, k_new, v_new, cache_kt, cache_vi, wl, gsub_row, lam_init, layer):
    n_b, seq, width = q.shape
    past = cache_kt.shape[-1]
    hv = cache_vi.shape[-1]
    vw = v_new.shape[-1]
    assert past % CHUNK == 0 and seq <= CHUNK
    tk = min(past, CACHE_TK)
    assert past % tk == 0
    return pl.pallas_call(
        functools.partial(_attn_sample_kernel, lam_init=lam_init, single_tile=past == tk),
        grid=(n_b, past // tk),
        in_specs=[pl.BlockSpec((None,) + wl.shape[1:], lambda b, j: (layer, 0, 0)),
                  pl.BlockSpec((None,) + gsub_row.shape[1:], lambda b, j: (layer, 0, 0)),
                  pl.BlockSpec((None, seq, width), lambda b, j: (b, 0, 0)),
                  pl.BlockSpec((None, None, width, tk), lambda b, j: (layer, b, 0, j)),
                  pl.BlockSpec((None, None, tk * DIFF_HEADS, hv), lambda b, j: (layer, b, j, 0)),
                  pl.BlockSpec((None, seq, width), lambda b, j: (b, 0, 0)),
                  pl.BlockSpec((None, seq, vw), lambda b, j: (b, 0, 0))],
        out_specs=pl.BlockSpec((None, seq, vw), lambda b, j: (b, 0, 0)),
        out_shape=jax.ShapeDtypeStruct((n_b, seq, vw), BF16),
        scratch_shapes=[pltpu.VMEM((DIFF_HEADS, 2 * seq, 1), F32), pltpu.VMEM((DIFF_HEADS, 2 * seq, 1), F32),
                        pltpu.VMEM((DIFF_HEADS, 2 * seq, hv), F32)],
        compiler_params=_cparams(2),
        name="attn_sample",
    )(wl, gsub_row, q, cache_kt, cache_vi, k_new, v_new)


def _mix_out_kernel(x_ref, or_ref, od_ref, gate1_ref, shift_ref, scale_ref, gate2_ref, g_ref,
                    wout_ref, wup_ref, wdn_ref, o_ref):
    n_seq, lt, d = x_ref.shape
    m = n_seq * lt
    rw = or_ref.shape[-1]
    mix = (_dot(or_ref[...].reshape(m, rw), wout_ref[:rw, :])
           + _dot(od_ref[...].reshape(m, od_ref.shape[-1]), wout_ref[rw:, :]))
    x3 = x_ref[...] + gate1_ref[...] * mix.reshape(n_seq, lt, d)
    o_ref[...] = _ffn_apply(x3, g_ref[...], shift_ref[...], scale_ref[...], gate2_ref[...], wup_ref, wdn_ref)


def _mix_out(x, o_r, o_d, mods, g_norm, wout, wup, wdn, layer):
    n_b, seq, d = x.shape
    n_seq, lt = _token_tiles(n_b, seq, OUT_TILE)

    def tok(width):
        return pl.BlockSpec((n_seq, lt, width), lambda b, i: (b, i, 0))

    return pl.pallas_call(
        _mix_out_kernel,
        grid=(n_b // n_seq, seq // lt),
        in_specs=[tok(d), tok(o_r.shape[-1]), tok(o_d.shape[-1]),
                  _mod_spec(layer, 5, n_seq, d), _mod_spec(layer, 6, n_seq, d),
                  _mod_spec(layer, 7, n_seq, d), _mod_spec(layer, 8, n_seq, d),
                  pl.BlockSpec((None, None, 1, d), lambda b, i: (layer, 2, 0, 0)),
                  _layer_weight(wout, layer), _layer_weight(wup, layer, 1), _layer_weight(wdn, layer, 1)],
        out_specs=tok(d),
        out_shape=jax.ShapeDtypeStruct(x.shape, F32),
        compiler_params=_cparams(2),
        name="mix_out",
    )(x, o_r, o_d, mods, mods, mods, mods, g_norm, wout, wup, wdn)


def kernel(x_prompt, x_sample, cache_k, cache_v, state_ret, c_prompt, c_sample, w_ada, b_ada, g_norm,
           w_ff_up, w_ff_down, w_in, w_out, g_qk, w_lambda, g_sub):
    depth = w_in.shape[0]
    n_p, seq_p, d = x_prompt.shape
    n_s, seq_s, _ = x_sample.shape
    past = cache_k.shape[2]
    dk_h, dv_h = state_ret.shape[-2], state_ret.shape[-1]
    sub = cache_k.shape[-1]
    hv = cache_v.shape[-1]
    ret_qk, ret_v = RET_HEADS * dk_h, RET_HEADS * dv_h
    diff_qk, diff_v = DIFF_HEADS * 2 * sub, DIFF_HEADS * hv
    sizes = (ret_qk, ret_v, diff_qk, diff_v)

    mods = _ada_mods(jnp.concatenate([c_prompt, c_sample], axis=0), w_ada, b_ada)
    mods = mods[:, :, :, None, :]
    mods_p, mods_s = mods[:, :, :n_p], mods[:, :, n_p:]

    rope_p = _rope_tables(jnp.arange(seq_p), dk_h, sub)
    rope_s = _rope_tables(past + jnp.arange(seq_s), dk_h, sub)
    rc_p, rc_s = min(RET_CHUNK, seq_p), min(RET_CHUNK, seq_s)
    tabs_p = _retention_tables(rc_p, dk_h, dv_h)
    tabs_s = _retention_tables(rc_s, dk_h, dv_h)
    group = np.arange(diff_qk) // sub
    gmat = jnp.asarray(group[:, None] == group[None, :], BF16)
    s0_p = jnp.zeros((n_p, RET_HEADS, dk_h, dv_h), F32)
    cache_kt = jnp.transpose(cache_k, (0, 1, 3, 4, 5, 2)).reshape(depth, n_s, diff_qk, past)
    cache_vi = cache_v.reshape(depth, n_s, past * DIFF_HEADS, hv)

    wup = w_ff_up.astype(BF16)
    wdn = w_ff_down.astype(BF16)
    win = w_in.astype(BF16)
    wout = w_out.astype(BF16)
    g_norm4 = g_norm[:, :, None, :]
    gqk = jnp.tile(g_qk, (1, 1, diff_qk // sub))[:, :, None, :]
    wl = w_lambda.astype(F32)
    gsub_row = g_sub[:, None, :]
    gsub_col = g_sub[:, :, None]

    yp, ys = x_prompt, x_sample
    kv_p, kv_s, sp, ssl = None, None, [], []
    for l in range(depth):
        lam_init = 0.8 - 0.6 * math.exp(-0.3 * l)

        yp = _ffn1(yp, mods_p, g_norm4, wup, wdn, l)
        *kv_p, q_t, k_a, v_t, o_r, s_new = _mix_in(yp, mods_p, g_norm4, win, gqk, gmat, rope_p, tabs_p,
                                                    s0_p, None, sizes, rc_p, l, True, kv_p)
        o_d = _attn_prompt(q_t, k_a, v_t, wl, gsub_col, lam_init, l)
        yp = _mix_out(yp, o_r, o_d, mods_p, g_norm4, wout, wup, wdn, l)
        sp.append(s_new)

        ys = _ffn1(ys, mods_s, g_norm4, wup, wdn, l)
        *kv_s, q_a, k_a, v_a, o_r, s_new = _mix_in(ys, mods_s, g_norm4, win, gqk, gmat, rope_s, tabs_s,
                                                    state_ret, l, sizes, rc_s, l, False, kv_s)
        o_d = _attn_sample(q_a, k_a, v_a, cache_kt, cache_vi, wl, gsub_row, lam_init, l)
        ys = _mix_out(ys, o_r, o_d, mods_s, g_norm4, wout, wup, wdn, l)
        ssl.append(s_new)

    k_prompt = jnp.transpose(kv_p[0].reshape(depth, n_p, DIFF_HEADS, 2, sub, seq_p), (0, 1, 5, 2, 3, 4))
    v_prompt = kv_p[1].reshape(depth, n_p, seq_p, DIFF_HEADS, hv)
    k_sample = kv_s[0].reshape(depth, n_s, seq_s, DIFF_HEADS, 2, sub)
    v_sample = kv_s[1].reshape(depth, n_s, seq_s, DIFF_HEADS, hv)
    return (yp, ys, k_prompt, v_prompt, jnp.stack(sp), k_sample, v_sample, jnp.stack(ssl))
```

```python
import functools
import math

import numpy as np
import jax
import jax.numpy as jnp
from jax import lax
from jax.experimental import pallas as pl
from jax.experimental.pallas import tpu as pltpu

F32 = jnp.float32
BF16 = jnp.bfloat16

CHUNK = 64
RET_HEADS = 4
DIFF_HEADS = 4
ROPE_THETA = 10000.0
EPS = 1e-6
LOG2E = math.log2(math.e)
MIX_TILE = 512
FFN_TILE = 1024
OUT_TILE = 512
FF_CHUNK = 256
RET_CHUNK = 256
ATTN_GROUP = 256
ATTN_LOOKAHEAD = 3
CACHE_TK = 4096
VMEM_LIMIT = 56 * 1024 * 1024


def _cparams(n_axes):
    return pltpu.CompilerParams(dimension_semantics=("arbitrary",) * n_axes,
                                vmem_limit_bytes=VMEM_LIMIT)


def _resident(block_shape, index_map):
    return pl.BlockSpec(block_shape, index_map, pipeline_mode=pl.Buffered(1))


def _rms(x):
    return x * lax.rsqrt(jnp.mean(x * x, axis=-1, keepdims=True) + EPS)


def _dot(a, b):
    return jnp.dot(a, b, preferred_element_type=F32)


def _dot_nt(a, b):
    return lax.dot_general(a, b, (((1,), (1,)), ((), ())), preferred_element_type=F32)


def _split_bf16(x):
    hi = x.astype(BF16)
    return hi, (x - hi.astype(F32)).astype(BF16)


def _ada_kernel(c_ref, w_ref, b_ref, o_ref):
    n = c_ref.shape[0]
    c = c_ref[...]
    a_hi, a_lo = _split_bf16(c * jax.nn.sigmoid(c))
    w_hi, w_lo = _split_bf16(w_ref[...])
    a_both = jnp.concatenate([a_hi.astype(F32), a_lo.astype(F32)], axis=0).astype(BF16)
    first = _dot(a_both, w_hi)
    o_ref[...] = first[:n] + first[n:] + _dot(a_hi, w_lo) + b_ref[...]


def _ada_mods(c_all, w_ada, b_ada):
    depth, d, n_out = w_ada.shape
    n_seq = c_all.shape[0]
    n_blk = n_out // d
    return pl.pallas_call(
        _ada_kernel,
        grid=(depth, n_blk),
        in_specs=[
            pl.BlockSpec((n_seq, d), lambda l, j: (0, 0)),
            pl.BlockSpec((None, d, d), lambda l, j: (l, 0, j)),
            pl.BlockSpec((None, None, 1, d), lambda l, j: (l, j, 0, 0)),
        ],
        out_specs=pl.BlockSpec((None, None, n_seq, d), lambda l, j: (l, j, 0, 0)),
        out_shape=jax.ShapeDtypeStruct((depth, n_blk, n_seq, d), F32),
        compiler_params=_cparams(2),
        name="ada_mods",
    )(c_all, w_ada, b_ada.reshape(depth, n_blk, 1, d))


def _pre(x3, g, shift, scale):
    return _rms(x3) * g * (1.0 + scale) + shift


def _ffn_apply(x3, g, shift, scale, gate, wup_ref, wdn_ref):
    n_seq, lt, d = x3.shape
    d_ff = wdn_ref.shape[0]
    hb = _pre(x3, g, shift, scale).reshape(n_seq * lt, d).astype(BF16)
    acc = jnp.zeros((n_seq * lt, d), F32)
    for c in range(d_ff // FF_CHUNK):
        lo = c * FF_CHUNK
        a = _dot(hb, wup_ref[:, lo:lo + FF_CHUNK])
        b = _dot(hb, wup_ref[:, d_ff + lo:d_ff + lo + FF_CHUNK])
        act = (a * jax.nn.sigmoid(a) * b).astype(BF16)
        acc = acc + _dot(act, wdn_ref[lo:lo + FF_CHUNK, :])
    return x3 + (0.5 * gate) * acc.reshape(n_seq, lt, d)


def _ffn_kernel(x_ref, shift_ref, scale_ref, gate_ref, g_ref, wup_ref, wdn_ref, o_ref):
    o_ref[...] = _ffn_apply(x_ref[...], g_ref[...], shift_ref[...], scale_ref[...], gate_ref[...],
                            wup_ref, wdn_ref)


def _token_tiles(n_b, seq, tile):
    lt = min(seq, tile)
    n_seq = max(1, min(n_b, tile // lt))
    assert seq % lt == 0 and n_b % n_seq == 0 and lt % 8 == 0
    return n_seq, lt


def _mod_spec(layer, idx, n_seq, d):
    return pl.BlockSpec((None, None, n_seq, 1, d), lambda b, i: (layer, idx, b, 0, 0))


def _layer_weight(w, *lead):
    n_lead = len(lead)
    return _resident((None,) * n_lead + w.shape[n_lead:], lambda b, i: lead + (0,) * (w.ndim - n_lead))


def _ffn1(x, mods, g_norm, wup, wdn, layer):
    n_b, seq, d = x.shape
    n_seq, lt = _token_tiles(n_b, seq, FFN_TILE)
    x_spec = pl.BlockSpec((n_seq, lt, d), lambda b, i: (b, i, 0))
    return pl.pallas_call(
        _ffn_kernel,
        grid=(n_b // n_seq, seq // lt),
        in_specs=[x_spec, _mod_spec(layer, 0, n_seq, d), _mod_spec(layer, 1, n_seq, d),
                  _mod_spec(layer, 2, n_seq, d),
                  pl.BlockSpec((None, None, 1, d), lambda b, i: (layer, 0, 0, 0)),
                  _layer_weight(wup, layer, 0), _layer_weight(wdn, layer, 0)],
        out_specs=x_spec,
        out_shape=jax.ShapeDtypeStruct(x.shape, F32),
        compiler_params=_cparams(2),
        name="ffn1",
    )(x, mods, mods, mods, g_norm, wup, wdn)


def _swap_pairs(x):
    w = x.shape[-1]
    lane = lax.broadcasted_iota(jnp.int32, (1, w), 1)
    return jnp.where((lane & 1) == 0, pltpu.roll(x, w - 1, 1), pltpu.roll(x, 1, 1))


def _swap_halves(x, half):
    w = x.shape[-1]
    lane = lax.broadcasted_iota(jnp.int32, (1, w), 1)
    return jnp.where((lane & (2 * half - 1)) < half, pltpu.roll(x, w - half, 1), pltpu.roll(x, half, 1))


def _mix_in_kernel(x_ref, shift_ref, scale_ref, g_ref, win_ref, gqk_ref, gmat_ref,
                   cosr_ref, sinr_ref, cosd_ref, sind_ref, dmask_ref, qdec_ref, kdec_ref, cdec_ref, s0_ref,
                   *rest, rc, sizes, prompt, n_prev):
    if n_prev:
        dkp_ref, dvp_ref = rest[:2]
        rest = rest[2:]
    dk_ref, dv_ref, qa_ref, ka_ref, va_ref, or_ref, snew_ref, s_scr = rest
    @pl.when(pl.program_id(1) == 0)
    def _():
        s_scr[...] = s0_ref[...]

    if n_prev:
        dk_ref[:n_prev] = dkp_ref[...]
        dv_ref[:n_prev] = dvp_ref[...]
    ret_qk, ret_v, diff_qk, diff_v = sizes
    n_seq, lt, d = x_ref.shape
    m = n_seq * lt
    dk_h = ret_qk // RET_HEADS
    dv_h = ret_v // RET_HEADS
    hv = diff_v // DIFF_HEADS
    hb = _pre(x_ref[...], g_ref[...], shift_ref[...], scale_ref[...]).reshape(m, d).astype(BF16)

    widths = (ret_qk, ret_qk, ret_v, ret_v, diff_qk, diff_qk, diff_v)

    def proj(n):
        lo = sum(widths[:n])
        return _dot(hb, win_ref[:, lo:lo + widths[n]])

    def rope(x, partner, cos_ref, sin_ref):
        w = x.shape[-1]
        y = x.reshape(n_seq, lt, w) * cos_ref[...] + partner.reshape(n_seq, lt, w) * sin_ref[...]
        return y.reshape(m, w)

    rq, rk = proj(0), proj(1)
    dq, dk = proj(4), proj(5)
    dv = proj(6)

    rq = rope(rq, _swap_pairs(rq), cosr_ref, sinr_ref)
    rk = rope(rk, _swap_pairs(rk), cosr_ref, sinr_ref) * (dk_h ** -0.5)
    chunks = [(s, c, s * lt + c * rc) for s in range(n_seq) for c in range(lt // rc)]
    kd_t = {(s, c): jnp.transpose(rk[r0:r0 + rc] * kdec_ref[...]).astype(BF16)
            for s, c, r0 in chunks}

    sub = diff_qk // (2 * DIFF_HEADS)
    qk = jnp.concatenate([dq, dk], axis=0)
    gmean = _dot((qk * qk).astype(BF16), gmat_ref[...]) * (1.0 / sub)
    rv, rg = proj(2), proj(3)
    qk = qk * lax.rsqrt(gmean + EPS)
    dq = qk[:m] * gqk_ref[0]
    dk = qk[m:] * gqk_ref[1]
    dq = rope(dq, _swap_halves(dq, sub // 2), cosd_ref, sind_ref) * (sub ** -0.5 * LOG2E)
    dk = rope(dk, _swap_halves(dk, sub // 2), cosd_ref, sind_ref)

    for s in range(n_seq):
        rows = slice(s * lt, (s + 1) * lt)
        for h in range(DIFF_HEADS):
            dv_ref[n_prev, s, pl.ds(h, lt, stride=DIFF_HEADS), :] = dv[rows, h * hv:(h + 1) * hv]
        if prompt:
            dk_t = jnp.transpose(dk[rows])
            dk_ref[n_prev, s] = dk_t
            ka_ref[s] = dk[rows].astype(BF16)
            qa_ref[s] = jnp.transpose(dq[rows]).astype(BF16)
            va_ref[s] = jnp.transpose(dv[rows]).astype(BF16)
        else:
            dk_ref[n_prev, s] = dk[rows]
            ka_ref[s] = dk[rows].astype(BF16)
            qa_ref[s] = dq[rows].astype(BF16)
            va_ref[s] = dv[rows].astype(BF16)

    lane_q =lax.broadcasted_iota(jnp.int32, (1, ret_qk), 1)
    gate = rg * jax.nn.sigmoid(rg)
    rv_b = rv.astype(BF16)
    zero_blk = jnp.zeros((dk_h, dv_h), BF16)
    def head_cols(h, width):
        return slice(h * width, (h + 1) * width)

    att, upd = {}, {}
    for s, c, r0 in chunks:
        q = rq[r0:r0 + rc]
        q_heads = jnp.concatenate(
            [jnp.where((lane_q >= h * dk_h) & (lane_q < (h + 1) * dk_h), q, 0.0) for h in range(RET_HEADS)],
            axis=0).astype(BF16)
        att[s, c] = (_dot_nt(q_heads, rk[r0:r0 + rc].astype(BF16)) * dmask_ref[...]).astype(BF16)
    for s, c, r0 in chunks:
        upd[s, c] = [_dot(kd_t[s, c][head_cols(h, dk_h)], rv_b[r0:r0 + rc, head_cols(h, dv_h)])
                     for h in range(RET_HEADS)]
    for s in range(n_seq):
        state = [s_scr[s, h] for h in range(RET_HEADS)]
        for c in range(lt // rc):
            r0 = s * lt + c * rc
            s_diag = jnp.concatenate(
                [jnp.concatenate([state[h].astype(BF16) if g == h else zero_blk for g in range(RET_HEADS)], axis=1)
                 for h in range(RET_HEADS)], axis=0)
            inter = _dot((rq[r0:r0 + rc] * qdec_ref[...]).astype(BF16), s_diag)
            for h in range(RET_HEADS):
                cols = head_cols(h, dv_h)
                o = _dot(att[s, c][h * rc:(h + 1) * rc], rv_b[r0:r0 + rc, cols]) + inter[:, cols]
                or_ref[s, c * rc:(c + 1) * rc, cols] = (_rms(o) * gate[r0:r0 + rc, cols]).astype(BF16)
                state[h] = cdec_ref[h] * state[h] + upd[s, c][h]
        for h in range(RET_HEADS):
            s_scr[s, h] = state[h]
            snew_ref[s, h] = state[h]


def _retention_tables(rc, dk_h, dv_h):
    n_h = RET_HEADS
    log_g = jnp.log1p(-jnp.exp2(-5.0 - jnp.arange(n_h, dtype=F32)))
    idx = jnp.arange(rc, dtype=F32)
    rel = idx[:, None] - idx[None, :]
    dmask = jnp.where(rel >= 0, jnp.exp(log_g[:, None, None] * jnp.maximum(rel, 0.0)), 0.0)
    q_dec = jnp.exp(log_g[None, :] * (idx[:, None] + 1.0))
    k_dec = jnp.exp(log_g[None, :] * (rc - 1.0 - idx[:, None]))
    c_dec = jnp.exp(log_g * rc)
    return (dmask.reshape(n_h * rc, rc), jnp.repeat(q_dec, dk_h, axis=1), jnp.repeat(k_dec, dk_h, axis=1),
            jnp.broadcast_to(c_dec[:, None, None], (n_h, dk_h, dv_h)))


def _rope_tables(pos, dk_h, sub):
    posf = pos.astype(F32)[:, None]
    ret_freq = 1.0 / (ROPE_THETA ** jnp.linspace(0.0, 1.0, dk_h // 2, dtype=F32))
    ang = posf * ret_freq[None, :]
    sign = jnp.tile(jnp.array([-1.0, 1.0], F32), dk_h // 2)
    cos_r = jnp.tile(jnp.repeat(jnp.cos(ang), 2, axis=1), (1, RET_HEADS))
    sin_r = jnp.tile(jnp.repeat(jnp.sin(ang), 2, axis=1) * sign, (1, RET_HEADS))
    rope_freq = 1.0 / (ROPE_THETA ** (jnp.arange(0, sub, 2, dtype=F32) / sub))
    ang = posf * rope_freq[None, :]
    sign = jnp.concatenate([-jnp.ones((sub // 2,), F32), jnp.ones((sub // 2,), F32)])
    cos_d = jnp.tile(jnp.tile(jnp.cos(ang), (1, 2)), (1, 2 * DIFF_HEADS))
    sin_d = jnp.tile(jnp.tile(jnp.sin(ang), (1, 2)) * sign, (1, 2 * DIFF_HEADS))
    return cos_r, sin_r, cos_d, sin_d


def _mix_in(x, mods, g_norm, win, gqk, gmat, rope, ret_tabs, s0, s0_layer, sizes, rc, layer, prompt, prev_kv):
    n_b, seq, d = x.shape
    n_prev = 0 if prev_kv is None else prev_kv[0].shape[0]
    ret_qk, ret_v, diff_qk, diff_v = sizes
    n_seq, lt = _token_tiles(n_b, seq, MIX_TILE)
    assert lt % rc == 0
    dk_h, dv_h = ret_qk // RET_HEADS, ret_v // RET_HEADS
    hv = diff_v // DIFF_HEADS

    def tok(width):
        return pl.BlockSpec((n_seq, lt, width), lambda b, i: (b, i, 0))

    def feat(width):
        return pl.BlockSpec((n_seq, width, lt), lambda b, i: (b, 0, i))

    def full(a):
        return _resident(a.shape, lambda b, i, _n=a.ndim: (0,) * _n)

    def pos_tab(a):
        return pl.BlockSpec((lt, a.shape[1]), lambda b, i: (i, 0))

    state_blk = (n_seq, RET_HEADS, dk_h, dv_h)
    if s0_layer is None:
        s0_spec = pl.BlockSpec(state_blk, lambda b, i: (b, 0, 0, 0))
    else:
        s0_spec = pl.BlockSpec((None,) + state_blk, lambda b, i: (s0_layer, b, 0, 0, 0))
    dv_blk, dv_idx = (n_seq, lt * DIFF_HEADS, hv), (lambda b, i: (b, i, 0))
    dv_full = (n_b, seq * DIFF_HEADS, hv)
    if prompt:
        dk_blk, dk_idx, dk_full = (n_seq, diff_qk, lt), (lambda b, i: (b, 0, i)), (n_b, diff_qk, seq)
    else:
        dk_blk, dk_idx, dk_full = (n_seq, lt, diff_qk), (lambda b, i: (b, i, 0)), (n_b, seq, diff_qk)

    def stacked_spec(n, blk, idx):
        return pl.BlockSpec((n,) + blk, lambda b, i: (0,) + idx(b, i))

    kv_specs = [stacked_spec(n_prev + 1, dk_blk, dk_idx), stacked_spec(n_prev + 1, dv_blk, dv_idx)]
    kv_shapes = [jax.ShapeDtypeStruct((n_prev + 1,) + dk_full, F32),
                 jax.ShapeDtypeStruct((n_prev + 1,) + dv_full, F32)]
    prev_specs = [stacked_spec(n_prev, dk_blk, dk_idx), stacked_spec(n_prev, dv_blk, dv_idx)] if n_prev else []
    if prompt:
        att_specs = kv_specs + [feat(diff_qk), tok(diff_qk),
                                pl.BlockSpec((n_seq, None, diff_v, lt), lambda b, i: (b, i, 0, 0))]
        att_shapes = kv_shapes + [jax.ShapeDtypeStruct((n_b, diff_qk, seq), BF16),
                                  jax.ShapeDtypeStruct((n_b, seq, diff_qk), BF16),
                                  jax.ShapeDtypeStruct((n_b, seq // lt, diff_v, lt), BF16)]
    else:
        att_specs = kv_specs + [tok(diff_qk), tok(diff_qk), tok(diff_v)]
        att_shapes = kv_shapes + [jax.ShapeDtypeStruct((n_b, seq, diff_qk), BF16),
                                  jax.ShapeDtypeStruct((n_b, seq, diff_qk), BF16),
                                  jax.ShapeDtypeStruct((n_b, seq, diff_v), BF16)]
    return pl.pallas_call(
        functools.partial(_mix_in_kernel, rc=rc, sizes=sizes, prompt=prompt, n_prev=n_prev),
        grid=(n_b // n_seq, seq // lt),
        in_specs=[tok(d), _mod_spec(layer, 3, n_seq, d), _mod_spec(layer, 4, n_seq, d),
                  pl.BlockSpec((None, None, 1, d), lambda b, i: (layer, 1, 0, 0)),
                  _layer_weight(win, layer),
                  pl.BlockSpec((None, 2, 1, diff_qk), lambda b, i: (layer, 0, 0, 0)),
                  full(gmat),
                  pos_tab(rope[0]), pos_tab(rope[1]), pos_tab(rope[2]), pos_tab(rope[3]),
                  full(ret_tabs[0]), full(ret_tabs[1]), full(ret_tabs[2]), full(ret_tabs[3]),
                  s0_spec] + prev_specs,
        out_specs=att_specs + [tok(ret_v), pl.BlockSpec(state_blk, lambda b, i: (b, 0, 0, 0))],
        out_shape=att_shapes + [jax.ShapeDtypeStruct((n_b, seq, ret_v), BF16),
                                jax.ShapeDtypeStruct((n_b,) + state_blk[1:], F32)],
        scratch_shapes=[pltpu.VMEM(state_blk, F32)],
        compiler_params=_cparams(2),
        name="mix_in",
    )(x, mods, mods, g_norm, win, gqk, gmat, *rope, *ret_tabs, s0, *(prev_kv or ()))


def _lambda(wl_ref, lam_init):
    wl = wl_ref[...]
    a = jnp.sum(wl[0:1] * wl[1:2], axis=-1, keepdims=True)
    b = jnp.sum(wl[2:3] * wl[3:4], axis=-1, keepdims=True)
    return jnp.exp(a) - jnp.exp(b) + lam_init


def _softmax_step_t(s_t, v_t, m_ref, l_ref, acc_ref):
    m_prev = m_ref[...]
    m_new = jnp.maximum(m_prev, jnp.max(s_t, axis=0, keepdims=True))
    alpha = jnp.exp2(m_prev - m_new)
    p = jnp.exp2(s_t - m_new)
    l_ref[...] = alpha * l_ref[...] + jnp.sum(p, axis=0, keepdims=True)
    acc_ref[...] = alpha * acc_ref[...] + _dot(v_t, p.astype(BF16))
    m_ref[...] = m_new


def _attn_prompt_kernel(wl_ref, gsub_ref, qt_ref, k_ref, vt_ref, o_ref, *stats, lam_init, extra=()):
    hw, seq = qt_ref.shape
    n_kt, _, tk = vt_ref.shape
    tq = tk
    gw = min(tq, ATTN_GROUP)
    n_g = tq // gw
    feat = lax.broadcasted_iota(jnp.int32, (hw, 1), 0)
    lam = _lambda(wl_ref, lam_init)
    k_chunk = lax.broadcasted_iota(jnp.int32, (gw, gw), 0) // CHUNK
    q_chunk = lax.broadcasted_iota(jnp.int32, (gw, gw), 1) // CHUNK
    diag_mask = jnp.where(k_chunk <= q_chunk, 0.0, -jnp.inf).astype(F32)

    def refs(sub, g):
        return stats[3 * (sub * n_g + g):3 * (sub * n_g + g) + 3]

    units = [(i, j, sub, g, tk if j < i else min(tk, (g + 1) * gw))
             for i in range(seq // tq) for j in range(i + 1) for sub in range(2) for g in range(n_g)]

    def scores(unit):
        i, j, sub, g, n_keys = unit
        q = qt_ref[:, i * tq + g * gw:i * tq + (g + 1) * gw]
        q = jnp.where(feat < hw // 2 if sub == 0 else feat >= hw // 2, q, jnp.zeros_like(q))
        s_t = _dot(k_ref[j * tk:j * tk + n_keys, :], q)
        if j == i:
            tail = s_t[n_keys - gw:] + diag_mask
            s_t = tail if n_keys == gw else jnp.concatenate([s_t[:n_keys - gw], tail], axis=0)
        return s_t

    extra_at = {e * len(units) // max(len(extra), 1): fn for e, fn in enumerate(extra)}
    assert len(extra_at) == len(extra)
    pending = [scores(u) for u in units[:ATTN_LOOKAHEAD]]
    for n, (i, j, sub, g, n_keys) in enumerate(units):
        if n in extra_at:
            extra_at[n]()
        if n + ATTN_LOOKAHEAD < len(units):
            pending.append(scores(units[n + ATTN_LOOKAHEAD]))
        m_ref, l_ref, acc_ref = refs(sub, g)
        if j == 0:
            m_ref[...] = jnp.full(m_ref.shape, -jnp.inf, F32)
            l_ref[...] = jnp.zeros(l_ref.shape, F32)
            acc_ref[...] = jnp.zeros(acc_ref.shape, F32)
        _softmax_step_t(pending.pop(0), vt_ref[j, :, :n_keys], m_ref, l_ref, acc_ref)
        if j == i and sub == 1 and g == n_g - 1:
            o_t = jnp.concatenate(
                [refs(0, c)[2][...] / refs(0, c)[1][...] - lam * (refs(1, c)[2][...] / refs(1, c)[1][...])
                 for c in range(n_g)], axis=1)
            o_t = (o_t * lax.rsqrt(jnp.mean(o_t * o_t, axis=0, keepdims=True) + EPS)
                   * gsub_ref[...] * (1.0 - lam_init))
            o_ref[i * tq:(i + 1) * tq, :] = jnp.transpose(o_t).astype(BF16)


def _attn_prompt(qt, k, vt, wl, gsub_col, lam_init, layer):
    n_b, width, seq = qt.shape
    n_kt, diff_v, tk = vt.shape[1:]
    hw, hv = width // DIFF_HEADS, diff_v // DIFF_HEADS
    gw = min(tk, ATTN_GROUP)
    assert seq == n_kt * tk and tk % CHUNK == 0 and tk % gw == 0
    return pl.pallas_call(
        functools.partial(_attn_prompt_kernel, lam_init=lam_init),
        grid=(n_b, DIFF_HEADS),
        in_specs=[pl.BlockSpec((None,) + wl.shape[1:], lambda b, h: (layer, 0, 0)),
                  pl.BlockSpec((None,) + gsub_col.shape[1:], lambda b, h: (layer, 0, 0)),
                  pl.BlockSpec((None, hw, seq), lambda b, h: (b, h, 0)),
                  pl.BlockSpec((None, seq, hw), lambda b, h: (b, 0, h)),
                  pl.BlockSpec((None, n_kt, hv, tk), lambda b, h: (b, 0, h, 0))],
        out_specs=pl.BlockSpec((None, seq, hv), lambda b, h: (b, 0, h)),
        out_shape=jax.ShapeDtypeStruct((n_b, seq, diff_v), BF16),
        scratch_shapes=[pltpu.VMEM((1, gw), F32), pltpu.VMEM((1, gw), F32), pltpu.VMEM((hv, gw), F32)]
        * (2 * (tk // gw)),
        compiler_params=_cparams(2),
        name="attn_prompt",
    )(wl, gsub_col, qt, k, vt)


def _stack_subheads(q):
    w = q.shape[-1]
    lane = lax.broadcasted_iota(jnp.int32, (1, w), 1)
    zero = jnp.zeros_like(q)
    return jnp.concatenate([jnp.where(lane < w // 2, q, zero), jnp.where(lane >= w // 2, q, zero)], axis=0)


def _softmax_step(s, v, m_ref, l_ref, acc_ref):
    m_prev = m_ref[...]
    m_new = jnp.maximum(m_prev, jnp.max(s, axis=-1, keepdims=True))
    alpha = jnp.exp2(m_prev - m_new)
    p = jnp.exp2(s - m_new)
    l_ref[...] = alpha * l_ref[...] + jnp.sum(p, axis=-1, keepdims=True)
    acc_ref[...] = alpha * acc_ref[...] + _dot(p.astype(BF16), v)
    m_ref[...] = m_new


class _SampleAttn:
    def __init__(self, wl_ref, gsub_ref, q_ref, kct_ref, vc_ref, kn_ref, vn_ref, o_ref, lam_init):
        self.refs = (wl_ref, gsub_ref, q_ref, kct_ref, vc_ref, kn_ref, vn_ref, o_ref)
        self.lam_init = lam_init
        self.hw = q_ref.shape[-1] // DIFF_HEADS
        self.hv = vc_ref.shape[-1]
        self.tk = kct_ref.shape[-1]

    def head_q(self, h):
        return _stack_subheads(self.refs[2][:, h * self.hw:(h + 1) * self.hw])

    def cache_scores(self, h):
        return _dot(self.head_q(h), self.refs[3][h * self.hw:(h + 1) * self.hw, :].astype(BF16))

    def cache_v(self, h):
        return self.refs[4][pl.ds(h, self.tk, stride=DIFF_HEADS), :].astype(BF16)

    def new_scores(self, h):
        return _dot_nt(self.head_q(h), self.refs[5][:, h * self.hw:(h + 1) * self.hw])

    def new_v(self, h):
        return self.refs[6][:, h * self.hv:(h + 1) * self.hv]

    def finish(self, h, acc, l):
        rows = acc.shape[0] // 2
        o = acc[:rows] / l[:rows] - _lambda(self.refs[0], self.lam_init) * (acc[rows:] / l[rows:])
        self.refs[7][:, h * self.hv:(h + 1) * self.hv] = (
            _rms(o) * self.refs[1][...] * (1.0 - self.lam_init)).astype(BF16)

    def whole_head(self, h, s_cache=None):
        s_cache = self.cache_scores(h) if s_cache is None else s_cache
        s_new = self.new_scores(h)
        m = jnp.maximum(jnp.max(s_cache, axis=-1, keepdims=True), jnp.max(s_new, axis=-1, keepdims=True))
        p_c = jnp.exp2(s_cache - m)
        p_n = jnp.exp2(s_new - m)
        l = jnp.sum(p_c, axis=-1, keepdims=True) + jnp.sum(p_n, axis=-1, keepdims=True)
        acc = _dot(p_c.astype(BF16), self.cache_v(h)) + _dot(p_n.astype(BF16), self.new_v(h))
        self.finish(h, acc, l)


def _attn_sample_kernel(wl_ref, gsub_ref, q_ref, kct_ref, vc_ref, kn_ref, vn_ref, o_ref,
                        m_ref, l_ref, acc_ref, *, lam_init, single_tile):
    j = pl.program_id(1)
    att = _SampleAttn(wl_ref, gsub_ref, q_ref, kct_ref, vc_ref, kn_ref, vn_ref, o_ref, lam_init)
    scores = [att.cache_scores(h) for h in range(DIFF_HEADS)]

    if single_tile:
        for h in range(DIFF_HEADS):
            att.whole_head(h, scores[h])
        return

    @pl.when(j == 0)
    def _():
        m_ref[...] = jnp.full(m_ref.shape, -jnp.inf, F32)
        l_ref[...] = jnp.zeros(l_ref.shape, F32)
        acc_ref[...] = jnp.zeros(acc_ref.shape, F32)

    for h in range(DIFF_HEADS):
        _softmax_step(scores[h], att.cache_v(h), m_ref.at[h], l_ref.at[h], acc_ref.at[h])

    @pl.when(j == pl.num_programs(1) - 1)
    def _():
        for h in range(DIFF_HEADS):
            _softmax_step(att.new_scores(h), att.new_v(h), m_ref.at[h], l_ref.at[h], acc_ref.at[h])
            att.finish(h, acc_ref[h], l_ref[h])


def _attn_sample(q, k_new, v_new, cache_kt, cache_vi, wl, gsub_row, lam_init, layer):
    n_b, seq, width = q.shape
    past = cache_kt.shape[-1]
    hv = cache_vi.shape[-1]
    vw = v_new.shape[-1]
    assert past % CHUNK == 0 and seq <= CHUNK
    tk = min(past, CACHE_TK)
    assert past % tk == 0
    return pl.pallas_call(
        functools.partial(_attn_sample_kernel, lam_init=lam_init, single_tile=past == tk),
        grid=(n_b, past // tk),
        in_specs=[pl.BlockSpec((None,) + wl.shape[1:], lambda b, j: (layer, 0, 0)),
                  pl.BlockSpec((None,) + gsub_row.shape[1:], lambda b, j: (layer, 0, 0)),
                  pl.BlockSpec((None, seq, width), lambda b, j: (b, 0, 0)),
                  pl.BlockSpec((None, None, width, tk), lambda b, j: (layer, b, 0, j)),
                  pl.BlockSpec((None, None, tk * DIFF_HEADS, hv), lambda b, j: (layer, b, j, 0)),
                  pl.BlockSpec((None, seq, width), lambda b, j: (b, 0, 0)),
                  pl.BlockSpec((None, seq, vw), lambda b, j: (b, 0, 0))],
        out_specs=pl.BlockSpec((None, seq, vw), lambda b, j: (b, 0, 0)),
        out_shape=jax.ShapeDtypeStruct((n_b, seq, vw), BF16),
        scratch_shapes=[pltpu.VMEM((DIFF_HEADS, 2 * seq, 1), F32), pltpu.VMEM((DIFF_HEADS, 2 * seq, 1), F32),
                        pltpu.VMEM((DIFF_HEADS, 2 * seq, hv), F32)],
        compiler_params=_cparams(2),
        name="attn_sample",
    )(wl, gsub_row, q, cache_kt, cache_vi, k_new, v_new)


def _attn_both_kernel(wl_ref, gcol_ref, qt_ref, k_ref, vt_ref, grow_ref, q_ref, kct_ref, vc_ref, kn_ref, vn_ref,
                      op_ref, os_ref, *stats, lam_init):
    att = _SampleAttn(wl_ref, grow_ref, q_ref, kct_ref, vc_ref, kn_ref, vn_ref, os_ref, lam_init)
    s_cache = {}
    pieces = []
    for h in range(DIFF_HEADS):
        pieces.append(lambda h=h: s_cache.__setitem__(h, att.cache_scores(h)))
        if h > 0:
            pieces.append(lambda h=h: att.whole_head(h - 1, s_cache.pop(h - 1)))
    pieces.append(lambda: att.whole_head(DIFF_HEADS - 1, s_cache.pop(DIFF_HEADS - 1)))
    _attn_prompt_kernel(wl_ref, gcol_ref, qt_ref, k_ref, vt_ref, op_ref, *stats, lam_init=lam_init, extra=pieces)


def _attn_both(qt, k, vt, q_s, k_new, v_new, cache_kt, cache_vi, wl, gsub_col, gsub_row, lam_init, layer):
    n_b, width, seq = qt.shape
    n_kt, diff_v, tk = vt.shape[1:]
    n_s, seq_s, _ = q_s.shape
    past = cache_kt.shape[-1]
    hw, hv = width // DIFF_HEADS, diff_v // DIFF_HEADS
    gw = min(tk, ATTN_GROUP)
    assert seq == n_kt * tk and tk % CHUNK == 0 and tk % gw == 0
    assert n_s == n_b * DIFF_HEADS and past % CHUNK == 0 and seq_s <= CHUNK

    def sample(blk):
        return pl.BlockSpec((None,) + blk, lambda b, h: (b * DIFF_HEADS + h, 0, 0))

    return pl.pallas_call(
        functools.partial(_attn_both_kernel, lam_init=lam_init),
        grid=(n_b, DIFF_HEADS),
        in_specs=[pl.BlockSpec((None,) + wl.shape[1:], lambda b, h: (layer, 0, 0)),
                  pl.BlockSpec((None,) + gsub_col.shape[1:], lambda b, h: (layer, 0, 0)),
                  pl.BlockSpec((None, hw, seq), lambda b, h: (b, h, 0)),
                  pl.BlockSpec((None, seq, hw), lambda b, h: (b, 0, h)),
                  pl.BlockSpec((None, n_kt, hv, tk), lambda b, h: (b, 0, h, 0)),
                  pl.BlockSpec((None,) + gsub_row.shape[1:], lambda b, h: (layer, 0, 0)),
                  sample((seq_s, width)),
                  pl.BlockSpec((None, None, width, past), lambda b, h: (layer, b * DIFF_HEADS + h, 0, 0)),
                  pl.BlockSpec((None, None, past * DIFF_HEADS, hv), lambda b, h: (layer, b * DIFF_HEADS + h, 0, 0)),
                  sample((seq_s, width)), sample((seq_s, diff_v))],
        out_specs=[pl.BlockSpec((None, seq, hv), lambda b, h: (b, 0, h)), sample((seq_s, diff_v))],
        out_shape=[jax.ShapeDtypeStruct((n_b, seq, diff_v), BF16),
                   jax.ShapeDtypeStruct((n_s, seq_s, diff_v), BF16)],
        scratch_shapes=[pltpu.VMEM((1, gw), F32), pltpu.VMEM((1, gw), F32), pltpu.VMEM((hv, gw), F32)]
        * (2 * (tk // gw)),
        compiler_params=_cparams(2),
        name="attn_both",
    )(wl, gsub_col, qt, k, vt, gsub_row, q_s, cache_kt, cache_vi, k_new, v_new)


def _mix_out_kernel(x_ref, or_ref, od_ref, gate1_ref, shift_ref, scale_ref, gate2_ref, g_ref,
                    wout_ref, wup_ref, wdn_ref, o_ref):
    n_seq, lt, d = x_ref.shape
    m = n_seq * lt
    rw = or_ref.shape[-1]
    mix = (_dot(or_ref[...].reshape(m, rw), wout_ref[:rw, :])
           + _dot(od_ref[...].reshape(m, od_ref.shape[-1]), wout_ref[rw:, :]))
    x3 = x_ref[...] + gate1_ref[...] * mix.reshape(n_seq, lt, d)
    o_ref[...] = _ffn_apply(x3, g_ref[...], shift_ref[...], scale_ref[...], gate2_ref[...], wup_ref, wdn_ref)


def _mix_out(x, o_r, o_d, mods, g_norm, wout, wup, wdn, layer):
    n_b, seq, d = x.shape
    n_seq, lt = _token_tiles(n_b, seq, OUT_TILE)

    def tok(width):
        return pl.BlockSpec((n_seq, lt, width), lambda b, i: (b, i, 0))

    return pl.pallas_call(
        _mix_out_kernel,
        grid=(n_b // n_seq, seq // lt),
        in_specs=[tok(d), tok(o_r.shape[-1]), tok(o_d.shape[-1]),
                  _mod_spec(layer, 5, n_seq, d), _mod_spec(layer, 6, n_seq, d),
                  _mod_spec(layer, 7, n_seq, d), _mod_spec(layer, 8, n_seq, d),
                  pl.BlockSpec((None, None, 1, d), lambda b, i: (layer, 2, 0, 0)),
                  _layer_weight(wout, layer), _layer_weight(wup, layer, 1), _layer_weight(wdn, layer, 1)],
        out_specs=tok(d),
        out_shape=jax.ShapeDtypeStruct(x.shape, F32),
        compiler_params=_cparams(2),
        name="mix_out",
    )(x, o_r, o_d, mods, mods, mods, mods, g_norm, wout, wup, wdn)


def kernel(x_prompt, x_sample, cache_k, cache_v, state_ret, c_prompt, c_sample, w_ada, b_ada, g_norm,
           w_ff_up, w_ff_down, w_in, w_out, g_qk, w_lambda, g_sub):
    depth = w_in.shape[0]
    n_p, seq_p, d = x_prompt.shape
    n_s, seq_s, _ = x_sample.shape
    past = cache_k.shape[2]
    dk_h, dv_h = state_ret.shape[-2], state_ret.shape[-1]
    sub = cache_k.shape[-1]
    hv = cache_v.shape[-1]
    ret_qk, ret_v = RET_HEADS * dk_h, RET_HEADS * dv_h
    diff_qk, diff_v = DIFF_HEADS * 2 * sub, DIFF_HEADS * hv
    sizes = (ret_qk, ret_v, diff_qk, diff_v)

    mods = _ada_mods(jnp.concatenate([c_prompt, c_sample], axis=0), w_ada, b_ada)
    mods = mods[:, :, :, None, :]
    mods_p, mods_s = mods[:, :, :n_p], mods[:, :, n_p:]

    rope_p = _rope_tables(jnp.arange(seq_p), dk_h, sub)
    rope_s = _rope_tables(past + jnp.arange(seq_s), dk_h, sub)
    rc_p, rc_s = min(RET_CHUNK, seq_p), min(RET_CHUNK, seq_s)
    tabs_p = _retention_tables(rc_p, dk_h, dv_h)
    tabs_s = _retention_tables(rc_s, dk_h, dv_h)
    group = np.arange(diff_qk) // sub
    gmat = jnp.asarray(group[:, None] == group[None, :], BF16)
    s0_p = jnp.zeros((n_p, RET_HEADS, dk_h, dv_h), F32)
    cache_kt = jnp.transpose(cache_k, (0, 1, 3, 4, 5, 2)).reshape(depth, n_s, diff_qk, past)
    cache_vi = cache_v.reshape(depth, n_s, past * DIFF_HEADS, hv)

    wup = w_ff_up.astype(BF16)
    wdn = w_ff_down.astype(BF16)
    win = w_in.astype(BF16)
    wout = w_out.astype(BF16)
    g_norm4 = g_norm[:, :, None, :]
    gqk = jnp.tile(g_qk, (1, 1, diff_qk // sub))[:, :, None, :]
    wl = w_lambda.astype(F32)
    gsub_row = g_sub[:, None, :]
    gsub_col = g_sub[:, :, None]

    yp, ys = x_prompt, x_sample
    kv_p, kv_s, sp, ssl = None, None, [], []
    for l in range(depth):
        lam_init = 0.8 - 0.6 * math.exp(-0.3 * l)

        yp = _ffn1(yp, mods_p, g_norm4, wup, wdn, l)
        *kv_p, q_t, k_p, v_t, or_p, s_new = _mix_in(yp, mods_p, g_norm4, win, gqk, gmat, rope_p, tabs_p,
                                                     s0_p, None, sizes, rc_p, l, True, kv_p)
        sp.append(s_new)
        ys = _ffn1(ys, mods_s, g_norm4, wup, wdn, l)
        *kv_s, q_a, k_a, v_a, or_s, s_new = _mix_in(ys, mods_s, g_norm4, win, gqk, gmat, rope_s, tabs_s,
                                                     state_ret, l, sizes, rc_s, l, False, kv_s)
        ssl.append(s_new)
        if n_s == n_p * DIFF_HEADS and past <= CACHE_TK:
            od_p, od_s = _attn_both(q_t, k_p, v_t, q_a, k_a, v_a, cache_kt, cache_vi, wl, gsub_col, gsub_row,
                                    lam_init, l)
        else:
            od_p = _attn_prompt(q_t, k_p, v_t, wl, gsub_col, lam_init, l)
            od_s = _attn_sample(q_a, k_a, v_a, cache_kt, cache_vi, wl, gsub_row, lam_init, l)
        yp = _mix_out(yp, or_p, od_p, mods_p, g_norm4, wout, wup, wdn, l)
        ys = _mix_out(ys, or_s, od_s, mods_s, g_norm4, wout, wup, wdn, l)

    k_prompt = jnp.transpose(kv_p[0].reshape(depth, n_p, DIFF_HEADS, 2, sub, seq_p), (0, 1, 5, 2, 3, 4))
    v_prompt = kv_p[1].reshape(depth, n_p, seq_p, DIFF_HEADS, hv)
    k_sample = kv_s[0].reshape(depth, n_s, seq_s, DIFF_HEADS, 2, sub)
    v_sample = kv_s[1].reshape(depth, n_s, seq_s, DIFF_HEADS, hv)
    return (yp, ys, k_prompt, v_prompt, jnp.stack(sp), k_sample, v_sample, jnp.stack(ssl))
```

```python
import functools
import math

import numpy as np
import jax
import jax.numpy as jnp
from jax import lax
from jax.experimental import pallas as pl
from jax.experimental.pallas import tpu as pltpu

F32 = jnp.float32
BF16 = jnp.bfloat16

CHUNK = 64
RET_HEADS = 4
DIFF_HEADS = 4
ROPE_THETA = 10000.0
EPS = 1e-6
LOG2E = math.log2(math.e)
MIX_TILE = 512
FFN_TILE = 1024
OUT_TILE = 512
FF_CHUNK = 256
RET_CHUNK = 256
ATTN_GROUP = 256
ATTN_LOOKAHEAD = 3
CACHE_TK = 4096
VMEM_LIMIT = 56 * 1024 * 1024


def _cparams(n_axes, flags=None):
    return pltpu.CompilerParams(dimension_semantics=("arbitrary",) * n_axes,
                                vmem_limit_bytes=VMEM_LIMIT, flags=flags)


def _resident(block_shape, index_map):
    return pl.BlockSpec(block_shape, index_map, pipeline_mode=pl.Buffered(1))


def _rms(x):
    return x * lax.rsqrt(jnp.mean(x * x, axis=-1, keepdims=True) + EPS)


def _dot(a, b):
    return jnp.dot(a, b, preferred_element_type=F32)


def _dot_nt(a, b):
    return lax.dot_general(a, b, (((1,), (1,)), ((), ())), preferred_element_type=F32)


def _split_bf16(x):
    hi = x.astype(BF16)
    return hi, (x - hi.astype(F32)).astype(BF16)


def _ada_kernel(c_ref, w_ref, b_ref, o_ref):
    n = c_ref.shape[0]
    c = c_ref[...]
    a_hi, a_lo = _split_bf16(c * jax.nn.sigmoid(c))
    w_hi, w_lo = _split_bf16(w_ref[...])
    a_both = jnp.concatenate([a_hi.astype(F32), a_lo.astype(F32)], axis=0).astype(BF16)
    first = _dot(a_both, w_hi)
    o_ref[...] = first[:n] + first[n:] + _dot(a_hi, w_lo) + b_ref[...]


def _ada_mods(c_all, w_ada, b_ada):
    depth, d, n_out = w_ada.shape
    n_seq = c_all.shape[0]
    n_blk = n_out // d
    return pl.pallas_call(
        _ada_kernel,
        grid=(depth, n_blk),
        in_specs=[
            pl.BlockSpec((n_seq, d), lambda l, j: (0, 0)),
            pl.BlockSpec((None, d, d), lambda l, j: (l, 0, j)),
            pl.BlockSpec((None, None, 1, d), lambda l, j: (l, j, 0, 0)),
        ],
        out_specs=pl.BlockSpec((None, None, n_seq, d), lambda l, j: (l, j, 0, 0)),
        out_shape=jax.ShapeDtypeStruct((depth, n_blk, n_seq, d), F32),
        compiler_params=_cparams(2),
        name="ada_mods",
    )(c_all, w_ada, b_ada.reshape(depth, n_blk, 1, d))


def _pre(x3, g, shift, scale):
    return _rms(x3) * g * (1.0 + scale) + shift


def _ffn_apply(x3, g, shift, scale, gate, wup_ref, wdn_ref):
    n_seq, lt, d = x3.shape
    d_ff = wdn_ref.shape[0]
    hb = _pre(x3, g, shift, scale).reshape(n_seq * lt, d).astype(BF16)
    acc = jnp.zeros((n_seq * lt, d), F32)
    for c in range(d_ff // FF_CHUNK):
        lo = c * FF_CHUNK
        a = _dot(hb, wup_ref[:, lo:lo + FF_CHUNK])
        b = _dot(hb, wup_ref[:, d_ff + lo:d_ff + lo + FF_CHUNK])
        act = (a * jax.nn.sigmoid(a) * b).astype(BF16)
        acc = acc + _dot(act, wdn_ref[lo:lo + FF_CHUNK, :])
    return x3 + (0.5 * gate) * acc.reshape(n_seq, lt, d)


def _ffn_kernel(x_ref, shift_ref, scale_ref, gate_ref, g_ref, wup_ref, wdn_ref, o_ref):
    o_ref[...] = _ffn_apply(x_ref[...], g_ref[...], shift_ref[...], scale_ref[...], gate_ref[...],
                            wup_ref, wdn_ref)


def _token_tiles(n_b, seq, tile):
    lt = min(seq, tile)
    n_seq = max(1, min(n_b, tile // lt))
    assert seq % lt == 0 and n_b % n_seq == 0 and lt % 8 == 0
    return n_seq, lt


def _mod_spec(layer, idx, n_seq, d):
    return pl.BlockSpec((None, None, n_seq, 1, d), lambda b, i: (layer, idx, b, 0, 0))


def _layer_weight(w, *lead):
    n_lead = len(lead)
    return _resident((None,) * n_lead + w.shape[n_lead:], lambda b, i: lead + (0,) * (w.ndim - n_lead))


def _ffn1(x, mods, g_norm, wup, wdn, layer):
    n_b, seq, d = x.shape
    n_seq, lt = _token_tiles(n_b, seq, FFN_TILE)
    x_spec = pl.BlockSpec((n_seq, lt, d), lambda b, i: (b, i, 0))
    return pl.pallas_call(
        _ffn_kernel,
        grid=(n_b // n_seq, seq // lt),
        in_specs=[x_spec, _mod_spec(layer, 0, n_seq, d), _mod_spec(layer, 1, n_seq, d),
                  _mod_spec(layer, 2, n_seq, d),
                  pl.BlockSpec((None, None, 1, d), lambda b, i: (layer, 0, 0, 0)),
                  _layer_weight(wup, layer, 0), _layer_weight(wdn, layer, 0)],
        out_specs=x_spec,
        out_shape=jax.ShapeDtypeStruct(x.shape, F32),
        compiler_params=_cparams(2),
        name="ffn1",
    )(x, mods, mods, mods, g_norm, wup, wdn)


def _swap_pairs(x):
    w = x.shape[-1]
    lane = lax.broadcasted_iota(jnp.int32, (1, w), 1)
    return jnp.where((lane & 1) == 0, pltpu.roll(x, w - 1, 1), pltpu.roll(x, 1, 1))


def _swap_halves(x, half):
    w = x.shape[-1]
    lane = lax.broadcasted_iota(jnp.int32, (1, w), 1)
    return jnp.where((lane & (2 * half - 1)) < half, pltpu.roll(x, w - half, 1), pltpu.roll(x, half, 1))


def _mix_in_kernel(x_ref, shift_ref, scale_ref, g_ref, win_ref, gqk_ref, gmat_ref,
                   cosr_ref, sinr_ref, cosd_ref, sind_ref, dmask_ref, qdec_ref, kdec_ref, cdec_ref, s0_ref,
                   *rest, rc, sizes, prompt, n_prev):
    if n_prev:
        dkp_ref, dvp_ref = rest[:2]
        rest = rest[2:]
    dk_ref, dv_ref, qa_ref, ka_ref, va_ref, or_ref, snew_ref, s_scr = rest
    @pl.when(pl.program_id(1) == 0)
    def _():
        s_scr[...] = s0_ref[...]

    if n_prev:
        dk_ref[:n_prev] = dkp_ref[...]
        dv_ref[:n_prev] = dvp_ref[...]
    ret_qk, ret_v, diff_qk, diff_v = sizes
    n_seq, lt, d = x_ref.shape
    m = n_seq * lt
    dk_h = ret_qk // RET_HEADS
    dv_h = ret_v // RET_HEADS
    hv = diff_v // DIFF_HEADS
    hb = _pre(x_ref[...], g_ref[...], shift_ref[...], scale_ref[...]).reshape(m, d).astype(BF16)

    widths = (ret_qk, ret_qk, ret_v, ret_v, diff_qk, diff_qk, diff_v)

    def proj(n):
        lo = sum(widths[:n])
        return _dot(hb, win_ref[:, lo:lo + widths[n]])

    def rope(x, partner, cos_ref, sin_ref):
        w = x.shape[-1]
        y = x.reshape(n_seq, lt, w) * cos_ref[...] + partner.reshape(n_seq, lt, w) * sin_ref[...]
        return y.reshape(m, w)

    rq, rk = proj(0), proj(1)
    dq, dk = proj(4), proj(5)
    dv = proj(6)

    rq = rope(rq, _swap_pairs(rq), cosr_ref, sinr_ref)
    rk = rope(rk, _swap_pairs(rk), cosr_ref, sinr_ref) * (dk_h ** -0.5)
    chunks = [(s, c, s * lt + c * rc) for s in range(n_seq) for c in range(lt // rc)]
    kd_t = {(s, c): jnp.transpose(rk[r0:r0 + rc] * kdec_ref[...]).astype(BF16)
            for s, c, r0 in chunks}

    sub = diff_qk // (2 * DIFF_HEADS)
    qk = jnp.concatenate([dq, dk], axis=0)
    gmean = _dot((qk * qk).astype(BF16), gmat_ref[...]) * (1.0 / sub)
    rv, rg = proj(2), proj(3)
    qk = qk * lax.rsqrt(gmean + EPS)
    dq = qk[:m] * gqk_ref[0]
    dk = qk[m:] * gqk_ref[1]
    dq = rope(dq, _swap_halves(dq, sub // 2), cosd_ref, sind_ref) * (sub ** -0.5 * LOG2E)
    dk = rope(dk, _swap_halves(dk, sub // 2), cosd_ref, sind_ref)

    for s in range(n_seq):
        rows = slice(s * lt, (s + 1) * lt)
        for h in range(DIFF_HEADS):
            dv_ref[n_prev, s, pl.ds(h, lt, stride=DIFF_HEADS), :] = dv[rows, h * hv:(h + 1) * hv]
        if prompt:
            dk_t = jnp.transpose(dk[rows])
            dk_ref[n_prev, s] = dk_t
            ka_ref[s] = dk[rows].astype(BF16)
            qa_ref[s] = jnp.transpose(dq[rows]).astype(BF16)
            va_ref[s] = jnp.transpose(dv[rows]).astype(BF16)
        else:
            for j in range(diff_qk // sub):
                dk_ref[n_prev, s, :, j, :] = dk[rows, j * sub:(j + 1) * sub]
            ka_ref[s] = dk[rows].astype(BF16)
            qa_ref[s] = dq[rows].astype(BF16)
            va_ref[s] = dv[rows].astype(BF16)

    lane_q =lax.broadcasted_iota(jnp.int32, (1, ret_qk), 1)
    gate = rg * jax.nn.sigmoid(rg)
    rv_b = rv.astype(BF16)
    zero_blk = jnp.zeros((dk_h, dv_h), BF16)
    def head_cols(h, width):
        return slice(h * width, (h + 1) * width)

    att, upd = {}, {}
    for s, c, r0 in chunks:
        q = rq[r0:r0 + rc]
        q_heads = jnp.concatenate(
            [jnp.where((lane_q >= h * dk_h) & (lane_q < (h + 1) * dk_h), q, 0.0) for h in range(RET_HEADS)],
            axis=0).astype(BF16)
        att[s, c] = (_dot_nt(q_heads, rk[r0:r0 + rc].astype(BF16)) * dmask_ref[...]).astype(BF16)
    for s, c, r0 in chunks:
        upd[s, c] = [_dot(kd_t[s, c][head_cols(h, dk_h)], rv_b[r0:r0 + rc, head_cols(h, dv_h)])
                     for h in range(RET_HEADS)]
    for s in range(n_seq):
        state = [s_scr[s, h] for h in range(RET_HEADS)]
        for c in range(lt // rc):
            r0 = s * lt + c * rc
            s_diag = jnp.concatenate(
                [jnp.concatenate([state[h].astype(BF16) if g == h else zero_blk for g in range(RET_HEADS)], axis=1)
                 for h in range(RET_HEADS)], axis=0)
            inter = _dot((rq[r0:r0 + rc] * qdec_ref[...]).astype(BF16), s_diag)
            for h in range(RET_HEADS):
                cols = head_cols(h, dv_h)
                o = _dot(att[s, c][h * rc:(h + 1) * rc], rv_b[r0:r0 + rc, cols]) + inter[:, cols]
                or_ref[s, c * rc:(c + 1) * rc, cols] = (_rms(o) * gate[r0:r0 + rc, cols]).astype(BF16)
                state[h] = cdec_ref[h] * state[h] + upd[s, c][h]
        for h in range(RET_HEADS):
            s_scr[s, h] = state[h]
            snew_ref[s, h] = state[h]


def _retention_tables(rc, dk_h, dv_h):
    n_h = RET_HEADS
    log_g = np.log1p(-np.exp2(-5.0 - np.arange(n_h, dtype=np.float64)))
    idx = np.arange(rc, dtype=np.float64)
    rel = idx[:, None] - idx[None, :]
    dmask = np.where(rel >= 0, np.exp(log_g[:, None, None] * np.maximum(rel, 0.0)), 0.0)
    q_dec = np.exp(log_g[None, :] * (idx[:, None] + 1.0))
    k_dec = np.exp(log_g[None, :] * (rc - 1.0 - idx[:, None]))
    c_dec = np.exp(log_g * rc)
    tabs = (dmask.reshape(n_h * rc, rc), np.repeat(q_dec, dk_h, axis=1), np.repeat(k_dec, dk_h, axis=1),
            np.broadcast_to(c_dec[:, None, None], (n_h, dk_h, dv_h)))
    return tuple(jnp.asarray(t, F32) for t in tabs)


def _rope_tables(pos, dk_h, sub):
    posf = np.asarray(pos, np.float64)[:, None]
    ret_freq = 1.0 / (ROPE_THETA ** np.linspace(0.0, 1.0, dk_h // 2))
    ang = posf * ret_freq[None, :]
    sign = np.tile(np.array([-1.0, 1.0]), dk_h // 2)
    cos_r = np.tile(np.repeat(np.cos(ang), 2, axis=1), (1, RET_HEADS))
    sin_r = np.tile(np.repeat(np.sin(ang), 2, axis=1) * sign, (1, RET_HEADS))
    rope_freq = 1.0 / (ROPE_THETA ** (np.arange(0, sub, 2, dtype=np.float64) / sub))
    ang = posf * rope_freq[None, :]
    sign = np.concatenate([-np.ones(sub // 2), np.ones(sub // 2)])
    cos_d = np.tile(np.tile(np.cos(ang), (1, 2)), (1, 2 * DIFF_HEADS))
    sin_d = np.tile(np.tile(np.sin(ang), (1, 2)) * sign, (1, 2 * DIFF_HEADS))
    return tuple(jnp.asarray(t, F32) for t in (cos_r, sin_r, cos_d, sin_d))


def _mix_in(x, mods, g_norm, win, gqk, gmat, rope, ret_tabs, s0, s0_layer, sizes, rc, layer, prompt, prev_kv):
    n_b, seq, d = x.shape
    n_prev = 0 if prev_kv is None else prev_kv[0].shape[0]
    ret_qk, ret_v, diff_qk, diff_v = sizes
    n_seq, lt = _token_tiles(n_b, seq, MIX_TILE)
    assert lt % rc == 0
    dk_h, dv_h = ret_qk // RET_HEADS, ret_v // RET_HEADS
    hv = diff_v // DIFF_HEADS

    def tok(width):
        return pl.BlockSpec((n_seq, lt, width), lambda b, i: (b, i, 0))

    def feat(width):
        return pl.BlockSpec((n_seq, width, lt), lambda b, i: (b, 0, i))

    def full(a):
        return _resident(a.shape, lambda b, i, _n=a.ndim: (0,) * _n)

    def pos_tab(a):
        return pl.BlockSpec((lt, a.shape[1]), lambda b, i: (i, 0))

    state_blk = (n_seq, RET_HEADS, dk_h, dv_h)
    if s0_layer is None:
        s0_spec = pl.BlockSpec(state_blk, lambda b, i: (b, 0, 0, 0))
    else:
        s0_spec = pl.BlockSpec((None,) + state_blk, lambda b, i: (s0_layer, b, 0, 0, 0))
    dv_blk, dv_idx = (n_seq, lt * DIFF_HEADS, hv), (lambda b, i: (b, i, 0))
    dv_full = (n_b, seq * DIFF_HEADS, hv)
    if prompt:
        dk_blk, dk_idx, dk_full = (n_seq, diff_qk, lt), (lambda b, i: (b, 0, i)), (n_b, diff_qk, seq)
    else:
        sub = diff_qk // (2 * DIFF_HEADS)
        dk_blk, dk_idx = (n_seq, lt, diff_qk // sub, sub), (lambda b, i: (b, i, 0, 0))
        dk_full = (n_b, seq, diff_qk // sub, sub)

    def stacked_spec(n, blk, idx):
        return pl.BlockSpec((n,) + blk, lambda b, i: (0,) + idx(b, i))

    kv_specs = [stacked_spec(n_prev + 1, dk_blk, dk_idx), stacked_spec(n_prev + 1, dv_blk, dv_idx)]
    kv_shapes = [jax.ShapeDtypeStruct((n_prev + 1,) + dk_full, F32),
                 jax.ShapeDtypeStruct((n_prev + 1,) + dv_full, F32)]
    prev_specs = [stacked_spec(n_prev, dk_blk, dk_idx), stacked_spec(n_prev, dv_blk, dv_idx)] if n_prev else []
    if prompt:
        att_specs = kv_specs + [feat(diff_qk), tok(diff_qk),
                                pl.BlockSpec((n_seq, None, diff_v, lt), lambda b, i: (b, i, 0, 0))]
        att_shapes = kv_shapes + [jax.ShapeDtypeStruct((n_b, diff_qk, seq), BF16),
                                  jax.ShapeDtypeStruct((n_b, seq, diff_qk), BF16),
                                  jax.ShapeDtypeStruct((n_b, seq // lt, diff_v, lt), BF16)]
    else:
        att_specs = kv_specs + [tok(diff_qk), tok(diff_qk), tok(diff_v)]
        att_shapes = kv_shapes + [jax.ShapeDtypeStruct((n_b, seq, diff_qk), BF16),
                                  jax.ShapeDtypeStruct((n_b, seq, diff_qk), BF16),
                                  jax.ShapeDtypeStruct((n_b, seq, diff_v), BF16)]
    return pl.pallas_call(
        functools.partial(_mix_in_kernel, rc=rc, sizes=sizes, prompt=prompt, n_prev=n_prev),
        grid=(n_b // n_seq, seq // lt),
        in_specs=[tok(d), _mod_spec(layer, 3, n_seq, d), _mod_spec(layer, 4, n_seq, d),
                  pl.BlockSpec((None, None, 1, d), lambda b, i: (layer, 1, 0, 0)),
                  _layer_weight(win, layer),
                  pl.BlockSpec((None, 2, 1, diff_qk), lambda b, i: (layer, 0, 0, 0)),
                  full(gmat),
                  pos_tab(rope[0]), pos_tab(rope[1]), pos_tab(rope[2]), pos_tab(rope[3]),
                  full(ret_tabs[0]), full(ret_tabs[1]), full(ret_tabs[2]), full(ret_tabs[3]),
                  s0_spec] + prev_specs,
        out_specs=att_specs + [tok(ret_v), pl.BlockSpec(state_blk, lambda b, i: (b, 0, 0, 0))],
        out_shape=att_shapes + [jax.ShapeDtypeStruct((n_b, seq, ret_v), BF16),
                                jax.ShapeDtypeStruct((n_b,) + state_blk[1:], F32)],
        scratch_shapes=[pltpu.VMEM(state_blk, F32)],
        compiler_params=_cparams(2),
        name="mix_in",
    )(x, mods, mods, g_norm, win, gqk, gmat, *rope, *ret_tabs, s0, *(prev_kv or ()))


def _lambda(wl_ref, lam_init):
    wl = wl_ref[...]
    a = jnp.sum(wl[0:1] * wl[1:2], axis=-1, keepdims=True)
    b = jnp.sum(wl[2:3] * wl[3:4], axis=-1, keepdims=True)
    return jnp.exp(a) - jnp.exp(b) + lam_init


def _softmax_step_t(s_t, v_t, m_ref, l_ref, acc_ref):
    m_prev = m_ref[...]
    m_new = jnp.maximum(m_prev, jnp.max(s_t, axis=0, keepdims=True))
    alpha = jnp.exp2(m_prev - m_new)
    p = jnp.exp2(s_t - m_new)
    l_ref[...] = alpha * l_ref[...] + jnp.sum(p, axis=0, keepdims=True)
    acc_ref[...] = alpha * acc_ref[...] + _dot(v_t, p.astype(BF16))
    m_ref[...] = m_new


def _attn_prompt_kernel(wl_ref, gsub_ref, qt_ref, k_ref, vt_ref, o_ref, *stats, lam_init, extra=()):
    hw, seq = qt_ref.shape
    n_kt, _, tk = vt_ref.shape
    tq = tk
    gw = min(tq, ATTN_GROUP)
    n_g = tq // gw
    feat = lax.broadcasted_iota(jnp.int32, (hw, 1), 0)
    lam = _lambda(wl_ref, lam_init)
    k_chunk = lax.broadcasted_iota(jnp.int32, (gw, gw), 0) // CHUNK
    q_chunk = lax.broadcasted_iota(jnp.int32, (gw, gw), 1) // CHUNK
    diag_mask = jnp.where(k_chunk <= q_chunk, 0.0, -jnp.inf).astype(F32)

    def refs(sub, g):
        return stats[3 * (sub * n_g + g):3 * (sub * n_g + g) + 3]

    units = [(i, j, sub, g, tk if j < i else min(tk, (g + 1) * gw))
             for i in range(seq // tq) for j in range(i + 1) for sub in range(2) for g in range(n_g)]

    def scores(unit):
        i, j, sub, g, n_keys = unit
        q = qt_ref[:, i * tq + g * gw:i * tq + (g + 1) * gw]
        q = jnp.where(feat < hw // 2 if sub == 0 else feat >= hw // 2, q, jnp.zeros_like(q))
        s_t = _dot(k_ref[j * tk:j * tk + n_keys, :], q)
        if j == i:
            tail = s_t[n_keys - gw:] + diag_mask
            s_t = tail if n_keys == gw else jnp.concatenate([s_t[:n_keys - gw], tail], axis=0)
        return s_t

    extra_at = {e * len(units) // max(len(extra), 1): fn for e, fn in enumerate(extra)}
    assert len(extra_at) == len(extra)
    pending = [scores(u) for u in units[:ATTN_LOOKAHEAD]]
    for n, (i, j, sub, g, n_keys) in enumerate(units):
        if n in extra_at:
            extra_at[n]()
        if n + ATTN_LOOKAHEAD < len(units):
            pending.append(scores(units[n + ATTN_LOOKAHEAD]))
        m_ref, l_ref, acc_ref = refs(sub, g)
        if j == 0:
            m_ref[...] = jnp.full(m_ref.shape, -jnp.inf, F32)
            l_ref[...] = jnp.zeros(l_ref.shape, F32)
            acc_ref[...] = jnp.zeros(acc_ref.shape, F32)
        _softmax_step_t(pending.pop(0), vt_ref[j, :, :n_keys], m_ref, l_ref, acc_ref)
        if j == i and sub == 1 and g == n_g - 1:
            o_t = jnp.concatenate(
                [refs(0, c)[2][...] / refs(0, c)[1][...] - lam * (refs(1, c)[2][...] / refs(1, c)[1][...])
                 for c in range(n_g)], axis=1)
            o_t = (o_t * lax.rsqrt(jnp.mean(o_t * o_t, axis=0, keepdims=True) + EPS)
                   * gsub_ref[...] * (1.0 - lam_init))
            o_ref[i * tq:(i + 1) * tq, :] = jnp.transpose(o_t).astype(BF16)


def _attn_prompt(qt, k, vt, wl, gsub_col, lam_init, layer):
    n_b, width, seq = qt.shape
    n_kt, diff_v, tk = vt.shape[1:]
    hw, hv = width // DIFF_HEADS, diff_v // DIFF_HEADS
    gw = min(tk, ATTN_GROUP)
    assert seq == n_kt * tk and tk % CHUNK == 0 and tk % gw == 0
    return pl.pallas_call(
        functools.partial(_attn_prompt_kernel, lam_init=lam_init),
        grid=(n_b, DIFF_HEADS),
        in_specs=[pl.BlockSpec((None,) + wl.shape[1:], lambda b, h: (layer, 0, 0)),
                  pl.BlockSpec((None,) + gsub_col.shape[1:], lambda b, h: (layer, 0, 0)),
                  pl.BlockSpec((None, hw, seq), lambda b, h: (b, h, 0)),
                  pl.BlockSpec((None, seq, hw), lambda b, h: (b, 0, h)),
                  pl.BlockSpec((None, n_kt, hv, tk), lambda b, h: (b, 0, h, 0))],
        out_specs=pl.BlockSpec((None, seq, hv), lambda b, h: (b, 0, h)),
        out_shape=jax.ShapeDtypeStruct((n_b, seq, diff_v), BF16),
        scratch_shapes=[pltpu.VMEM((1, gw), F32), pltpu.VMEM((1, gw), F32), pltpu.VMEM((hv, gw), F32)]
        * (2 * (tk // gw)),
        compiler_params=_cparams(2),
        name="attn_prompt",
    )(wl, gsub_col, qt, k, vt)


def _stack_subheads(q):
    w = q.shape[-1]
    lane = lax.broadcasted_iota(jnp.int32, (1, w), 1)
    zero = jnp.zeros_like(q)
    return jnp.concatenate([jnp.where(lane < w // 2, q, zero), jnp.where(lane >= w // 2, q, zero)], axis=0)


def _softmax_step(s, v, m_ref, l_ref, acc_ref):
    m_prev = m_ref[...]
    m_new = jnp.maximum(m_prev, jnp.max(s, axis=-1, keepdims=True))
    alpha = jnp.exp2(m_prev - m_new)
    p = jnp.exp2(s - m_new)
    l_ref[...] = alpha * l_ref[...] + jnp.sum(p, axis=-1, keepdims=True)
    acc_ref[...] = alpha * acc_ref[...] + _dot(p.astype(BF16), v)
    m_ref[...] = m_new


class _SampleAttn:
    def __init__(self, wl_ref, gsub_ref, q_ref, kct_ref, vc_ref, kn_ref, vn_ref, o_ref, lam_init):
        self.refs = (wl_ref, gsub_ref, q_ref, kct_ref, vc_ref, kn_ref, vn_ref, o_ref)
        self.lam_init = lam_init
        self.hw = q_ref.shape[-1] // DIFF_HEADS
        self.hv = vc_ref.shape[-1]
        self.tk = kct_ref.shape[-1]

    def head_q(self, h):
        return _stack_subheads(self.refs[2][:, h * self.hw:(h + 1) * self.hw])

    def cache_scores(self, h):
        return _dot(self.head_q(h), self.refs[3][h * self.hw:(h + 1) * self.hw, :].astype(BF16))

    def cache_v(self, h):
        return self.refs[4][pl.ds(h, self.tk, stride=DIFF_HEADS), :].astype(BF16)

    def new_scores(self, h):
        return _dot_nt(self.head_q(h), self.refs[5][:, h * self.hw:(h + 1) * self.hw])

    def new_v(self, h):
        return self.refs[6][:, h * self.hv:(h + 1) * self.hv]

    def finish(self, h, acc, l):
        rows = acc.shape[0] // 2
        o = acc[:rows] / l[:rows] - _lambda(self.refs[0], self.lam_init) * (acc[rows:] / l[rows:])
        self.refs[7][:, h * self.hv:(h + 1) * self.hv] = (
            _rms(o) * self.refs[1][...] * (1.0 - self.lam_init)).astype(BF16)

    def whole_head(self, h, s_cache=None):
        s_cache = self.cache_scores(h) if s_cache is None else s_cache
        s_new = self.new_scores(h)
        m = jnp.maximum(jnp.max(s_cache, axis=-1, keepdims=True), jnp.max(s_new, axis=-1, keepdims=True))
        p_c = jnp.exp2(s_cache - m)
        p_n = jnp.exp2(s_new - m)
        l = jnp.sum(p_c, axis=-1, keepdims=True) + jnp.sum(p_n, axis=-1, keepdims=True)
        acc = _dot(p_c.astype(BF16), self.cache_v(h)) + _dot(p_n.astype(BF16), self.new_v(h))
        self.finish(h, acc, l)


def _attn_sample_kernel(wl_ref, gsub_ref, q_ref, kct_ref, vc_ref, kn_ref, vn_ref, o_ref,
                        m_ref, l_ref, acc_ref, *, lam_init, single_tile):
    j = pl.program_id(1)
    att = _SampleAttn(wl_ref, gsub_ref, q_ref, kct_ref, vc_ref, kn_ref, vn_ref, o_ref, lam_init)
    scores = [att.cache_scores(h) for h in range(DIFF_HEADS)]

    if single_tile:
        for h in range(DIFF_HEADS):
            att.whole_head(h, scores[h])
        return

    @pl.when(j == 0)
    def _():
        m_ref[...] = jnp.full(m_ref.shape, -jnp.inf, F32)
        l_ref[...] = jnp.zeros(l_ref.shape, F32)
        acc_ref[...] = jnp.zeros(acc_ref.shape, F32)

    for h in range(DIFF_HEADS):
        _softmax_step(scores[h], att.cache_v(h), m_ref.at[h], l_ref.at[h], acc_ref.at[h])

    @pl.when(j == pl.num_programs(1) - 1)
    def _():
        for h in range(DIFF_HEADS):
            _softmax_step(att.new_scores(h), att.new_v(h), m_ref.at[h], l_ref.at[h], acc_ref.at[h])
            att.finish(h, acc_ref[h], l_ref[h])


def _attn_sample(q, k_new, v_new, cache_kt, cache_vi, wl, gsub_row, lam_init, layer):
    n_b, seq, width = q.shape
    past = cache_kt.shape[-1]
    hv = cache_vi.shape[-1]
    vw = v_new.shape[-1]
    assert past % CHUNK == 0 and seq <= CHUNK
    tk = min(past, CACHE_TK)
    assert past % tk == 0
    return pl.pallas_call(
        functools.partial(_attn_sample_kernel, lam_init=lam_init, single_tile=past == tk),
        grid=(n_b, past // tk),
        in_specs=[pl.BlockSpec((None,) + wl.shape[1:], lambda b, j: (layer, 0, 0)),
                  pl.BlockSpec((None,) + gsub_row.shape[1:], lambda b, j: (layer, 0, 0)),
                  pl.BlockSpec((None, seq, width), lambda b, j: (b, 0, 0)),
                  pl.BlockSpec((None, None, width, tk), lambda b, j: (layer, b, 0, j)),
                  pl.BlockSpec((None, None, tk * DIFF_HEADS, hv), lambda b, j: (layer, b, j, 0)),
                  pl.BlockSpec((None, seq, width), lambda b, j: (b, 0, 0)),
                  pl.BlockSpec((None, seq, vw), lambda b, j: (b, 0, 0))],
        out_specs=pl.BlockSpec((None, seq, vw), lambda b, j: (b, 0, 0)),
        out_shape=jax.ShapeDtypeStruct((n_b, seq, vw), BF16),
        scratch_shapes=[pltpu.VMEM((DIFF_HEADS, 2 * seq, 1), F32), pltpu.VMEM((DIFF_HEADS, 2 * seq, 1), F32),
                        pltpu.VMEM((DIFF_HEADS, 2 * seq, hv), F32)],
        compiler_params=_cparams(2),
        name="attn_sample",
    )(wl, gsub_row, q, cache_kt, cache_vi, k_new, v_new)


def _attn_both_kernel(wl_ref, gcol_ref, qt_ref, k_ref, vt_ref, grow_ref, q_ref, kct_ref, vc_ref, kn_ref, vn_ref,
                      op_ref, os_ref, *stats, lam_init):
    att = _SampleAttn(wl_ref, grow_ref, q_ref, kct_ref, vc_ref, kn_ref, vn_ref, os_ref, lam_init)
    s_cache = {}
    pieces = []
    for h in range(DIFF_HEADS):
        pieces.append(lambda h=h: s_cache.__setitem__(h, att.cache_scores(h)))
        if h > 0:
            pieces.append(lambda h=h: att.whole_head(h - 1, s_cache.pop(h - 1)))
    pieces.append(lambda: att.whole_head(DIFF_HEADS - 1, s_cache.pop(DIFF_HEADS - 1)))
    _attn_prompt_kernel(wl_ref, gcol_ref, qt_ref, k_ref, vt_ref, op_ref, *stats, lam_init=lam_init, extra=pieces)


def _attn_both(qt, k, vt, q_s, k_new, v_new, cache_kt, cache_vi, wl, gsub_col, gsub_row, lam_init, layer):
    n_b, width, seq = qt.shape
    n_kt, diff_v, tk = vt.shape[1:]
    n_s, seq_s, _ = q_s.shape
    past = cache_kt.shape[-1]
    hw, hv = width // DIFF_HEADS, diff_v // DIFF_HEADS
    gw = min(tk, ATTN_GROUP)
    assert seq == n_kt * tk and tk % CHUNK == 0 and tk % gw == 0
    assert n_s == n_b * DIFF_HEADS and past % CHUNK == 0 and seq_s <= CHUNK

    def sample(blk):
        return pl.BlockSpec((None,) + blk, lambda b, h: (b * DIFF_HEADS + h, 0, 0))

    return pl.pallas_call(
        functools.partial(_attn_both_kernel, lam_init=lam_init),
        grid=(n_b, DIFF_HEADS),
        in_specs=[pl.BlockSpec((None,) + wl.shape[1:], lambda b, h: (layer, 0, 0)),
                  pl.BlockSpec((None,) + gsub_col.shape[1:], lambda b, h: (layer, 0, 0)),
                  pl.BlockSpec((None, hw, seq), lambda b, h: (b, h, 0)),
                  pl.BlockSpec((None, seq, hw), lambda b, h: (b, 0, h)),
                  pl.BlockSpec((None, n_kt, hv, tk), lambda b, h: (b, 0, h, 0)),
                  pl.BlockSpec((None,) + gsub_row.shape[1:], lambda b, h: (layer, 0, 0)),
                  sample((seq_s, width)),
                  pl.BlockSpec((None, None, width, past), lambda b, h: (layer, b * DIFF_HEADS + h, 0, 0)),
                  pl.BlockSpec((None, None, past * DIFF_HEADS, hv), lambda b, h: (layer, b * DIFF_HEADS + h, 0, 0)),
                  sample((seq_s, width)), sample((seq_s, diff_v))],
        out_specs=[pl.BlockSpec((None, seq, hv), lambda b, h: (b, 0, h)), sample((seq_s, diff_v))],
        out_shape=[jax.ShapeDtypeStruct((n_b, seq, diff_v), BF16),
                   jax.ShapeDtypeStruct((n_s, seq_s, diff_v), BF16)],
        scratch_shapes=[pltpu.VMEM((1, gw), F32), pltpu.VMEM((1, gw), F32), pltpu.VMEM((hv, gw), F32)]
        * (2 * (tk // gw)),
        compiler_params=_cparams(2),
        name="attn_both",
    )(wl, gsub_col, qt, k, vt, gsub_row, q_s, cache_kt, cache_vi, k_new, v_new)


def _mix_out_kernel(x_ref, or_ref, od_ref, gate1_ref, shift_ref, scale_ref, gate2_ref, g_ref,
                    wout_ref, wup_ref, wdn_ref, o_ref):
    n_seq, lt, d = x_ref.shape
    m = n_seq * lt
    rw = or_ref.shape[-1]
    mix = (_dot(or_ref[...].reshape(m, rw), wout_ref[:rw, :])
           + _dot(od_ref[...].reshape(m, od_ref.shape[-1]), wout_ref[rw:, :]))
    x3 = x_ref[...] + gate1_ref[...] * mix.reshape(n_seq, lt, d)
    o_ref[...] = _ffn_apply(x3, g_ref[...], shift_ref[...], scale_ref[...], gate2_ref[...], wup_ref, wdn_ref)


def _mix_out(x, o_r, o_d, mods, g_norm, wout, wup, wdn, layer):
    n_b, seq, d = x.shape
    n_seq, lt = _token_tiles(n_b, seq, OUT_TILE)

    def tok(width):
        return pl.BlockSpec((n_seq, lt, width), lambda b, i: (b, i, 0))

    return pl.pallas_call(
        _mix_out_kernel,
        grid=(n_b // n_seq, seq // lt),
        in_specs=[tok(d), tok(o_r.shape[-1]), tok(o_d.shape[-1]),
                  _mod_spec(layer, 5, n_seq, d), _mod_spec(layer, 6, n_seq, d),
                  _mod_spec(layer, 7, n_seq, d), _mod_spec(layer, 8, n_seq, d),
                  pl.BlockSpec((None, None, 1, d), lambda b, i: (layer, 2, 0, 0)),
                  _layer_weight(wout, layer), _layer_weight(wup, layer, 1), _layer_weight(wdn, layer, 1)],
        out_specs=tok(d),
        out_shape=jax.ShapeDtypeStruct(x.shape, F32),
        compiler_params=_cparams(2),
        name="mix_out",
    )(x, o_r, o_d, mods, mods, mods, mods, g_norm, wout, wup, wdn)


def kernel(x_prompt, x_sample, cache_k, cache_v, state_ret, c_prompt, c_sample, w_ada, b_ada, g_norm,
           w_ff_up, w_ff_down, w_in, w_out, g_qk, w_lambda, g_sub):
    depth = w_in.shape[0]
    n_p, seq_p, d = x_prompt.shape
    n_s, seq_s, _ = x_sample.shape
    past = cache_k.shape[2]
    dk_h, dv_h = state_ret.shape[-2], state_ret.shape[-1]
    sub = cache_k.shape[-1]
    hv = cache_v.shape[-1]
    ret_qk, ret_v = RET_HEADS * dk_h, RET_HEADS * dv_h
    diff_qk, diff_v = DIFF_HEADS * 2 * sub, DIFF_HEADS * hv
    sizes = (ret_qk, ret_v, diff_qk, diff_v)

    mods = _ada_mods(jnp.concatenate([c_prompt, c_sample], axis=0), w_ada, b_ada)
    mods = mods[:, :, :, None, :]
    mods_p, mods_s = mods[:, :, :n_p], mods[:, :, n_p:]

    rope_p = _rope_tables(np.arange(seq_p), dk_h, sub)
    rope_s = _rope_tables(past + np.arange(seq_s), dk_h, sub)
    rc_p, rc_s = min(RET_CHUNK, seq_p), min(RET_CHUNK, seq_s)
    tabs_p = _retention_tables(rc_p, dk_h, dv_h)
    tabs_s = _retention_tables(rc_s, dk_h, dv_h)
    group = np.arange(diff_qk) // sub
    gmat = jnp.asarray(group[:, None] == group[None, :], BF16)
    s0_p = jnp.zeros((n_p, RET_HEADS, dk_h, dv_h), F32)
    cache_kt = jnp.transpose(cache_k, (0, 1, 3, 4, 5, 2)).reshape(depth, n_s, diff_qk, past)
    cache_vi = cache_v.reshape(depth, n_s, past * DIFF_HEADS, hv)

    wup = w_ff_up.astype(BF16)
    wdn = w_ff_down.astype(BF16)
    win = w_in.astype(BF16)
    wout = w_out.astype(BF16)
    g_norm4 = g_norm[:, :, None, :]
    gqk = jnp.tile(g_qk, (1, 1, diff_qk // sub))[:, :, None, :]
    wl = w_lambda.astype(F32)
    gsub_row = g_sub[:, None, :]
    gsub_col = g_sub[:, :, None]

    yp, ys = x_prompt, x_sample
    kv_p, kv_s, sp, ssl = None, None, [], []
    for l in range(depth):
        lam_init = 0.8 - 0.6 * math.exp(-0.3 * l)

        yp = _ffn1(yp, mods_p, g_norm4, wup, wdn, l)
        *kv_p, q_t, k_p, v_t, or_p, s_new = _mix_in(yp, mods_p, g_norm4, win, gqk, gmat, rope_p, tabs_p,
                                                     s0_p, None, sizes, rc_p, l, True, kv_p)
        sp.append(s_new)
        ys = _ffn1(ys, mods_s, g_norm4, wup, wdn, l)
        *kv_s, q_a, k_a, v_a, or_s, s_new = _mix_in(ys, mods_s, g_norm4, win, gqk, gmat, rope_s, tabs_s,
                                                     state_ret, l, sizes, rc_s, l, False, kv_s)
        ssl.append(s_new)
        if n_s == n_p * DIFF_HEADS and past <= CACHE_TK:
            od_p, od_s = _attn_both(q_t, k_p, v_t, q_a, k_a, v_a, cache_kt, cache_vi, wl, gsub_col, gsub_row,
                                    lam_init, l)
        else:
            od_p = _attn_prompt(q_t, k_p, v_t, wl, gsub_col, lam_init, l)
            od_s = _attn_sample(q_a, k_a, v_a, cache_kt, cache_vi, wl, gsub_row, lam_init, l)
        yp = _mix_out(yp, or_p, od_p, mods_p, g_norm4, wout, wup, wdn, l)
        ys = _mix_out(ys, or_s, od_s, mods_s, g_norm4, wout, wup, wdn, l)

    k_prompt = jnp.transpose(kv_p[0].reshape(depth, n_p, DIFF_HEADS, 2, sub, seq_p), (0, 1, 5, 2, 3, 4))
    v_prompt = kv_p[1].reshape(depth, n_p, seq_p, DIFF_HEADS, hv)
    k_sample = kv_s[0].reshape(depth, n_s, seq_s, DIFF_HEADS, 2, sub)
    v_sample = kv_s[1].reshape(depth, n_s, seq_s, DIFF_HEADS, hv)
    return (yp, ys, k_prompt, v_prompt, jnp.stack(sp), k_sample, v_sample, jnp.stack(ssl))
```

```python
import functools
import math

import numpy as np
import jax
import jax.numpy as jnp
from jax import lax
from jax.experimental import pallas as pl
from jax.experimental.pallas import tpu as pltpu

F32 = jnp.float32
BF16 = jnp.bfloat16

CHUNK = 64
RET_HEADS = 4
DIFF_HEADS = 4
ROPE_THETA = 10000.0
EPS = 1e-6
LOG2E = math.log2(math.e)
MIX_TILE = 512
FFN_TILE = 1024
OUT_TILE = 512
FF_CHUNK = 256
RET_CHUNK = 256
ATTN_GROUP = 256
ATTN_LOOKAHEAD = 3
CACHE_TK = 4096
VMEM_LIMIT = 56 * 1024 * 1024


def _cparams(n_axes, flags=None):
    return pltpu.CompilerParams(dimension_semantics=("arbitrary",) * n_axes,
                                vmem_limit_bytes=VMEM_LIMIT, flags=flags)


def _resident(block_shape, index_map):
    return pl.BlockSpec(block_shape, index_map, pipeline_mode=pl.Buffered(1))


def _rms(x):
    return x * lax.rsqrt(jnp.mean(x * x, axis=-1, keepdims=True) + EPS)


def _dot(a, b):
    return jnp.dot(a, b, preferred_element_type=F32)


def _dot_nt(a, b):
    return lax.dot_general(a, b, (((1,), (1,)), ((), ())), preferred_element_type=F32)


def _split_bf16(x):
    hi = x.astype(BF16)
    return hi, (x - hi.astype(F32)).astype(BF16)


def _ada_kernel(c_ref, w_ref, b_ref, o_ref):
    n = c_ref.shape[0]
    c = c_ref[...]
    a_hi, a_lo = _split_bf16(c * jax.nn.sigmoid(c))
    w_hi, w_lo = _split_bf16(w_ref[...])
    a_both = jnp.concatenate([a_hi.astype(F32), a_lo.astype(F32)], axis=0).astype(BF16)
    first = _dot(a_both, w_hi)
    o_ref[...] = first[:n] + first[n:] + _dot(a_hi, w_lo) + b_ref[...]


def _ada_mods(c_all, w_ada, b_ada):
    depth, d, n_out = w_ada.shape
    n_seq = c_all.shape[0]
    n_blk = n_out // d
    return pl.pallas_call(
        _ada_kernel,
        grid=(depth, n_blk),
        in_specs=[
            pl.BlockSpec((n_seq, d), lambda l, j: (0, 0)),
            pl.BlockSpec((None, d, d), lambda l, j: (l, 0, j)),
            pl.BlockSpec((None, None, 1, d), lambda l, j: (l, j, 0, 0)),
        ],
        out_specs=pl.BlockSpec((None, None, n_seq, d), lambda l, j: (l, j, 0, 0)),
        out_shape=jax.ShapeDtypeStruct((depth, n_blk, n_seq, d), F32),
        compiler_params=_cparams(2),
        name="ada_mods",
    )(c_all, w_ada, b_ada.reshape(depth, n_blk, 1, d))


def _pre(x3, g, shift, scale):
    return _rms(x3) * g * (1.0 + scale) + shift


def _ffn_apply(x3, g, shift, scale, gate, wup_ref, wdn_ref):
    n_seq, lt, d = x3.shape
    d_ff = wdn_ref.shape[0]
    hb = _pre(x3, g, shift, scale).reshape(n_seq * lt, d).astype(BF16)
    acc = jnp.zeros((n_seq * lt, d), F32)
    for c in range(d_ff // FF_CHUNK):
        lo = c * FF_CHUNK
        a = _dot(hb, wup_ref[:, lo:lo + FF_CHUNK])
        b = _dot(hb, wup_ref[:, d_ff + lo:d_ff + lo + FF_CHUNK])
        act = (a * jax.nn.sigmoid(a) * b).astype(BF16)
        acc = acc + _dot(act, wdn_ref[lo:lo + FF_CHUNK, :])
    return x3 + (0.5 * gate) * acc.reshape(n_seq, lt, d)


def _cast_riders(srcs, n_outer, n_inner):
    n_steps = n_outer * n_inner
    in_specs, operands, out_specs, out_shapes = [], [], [], []
    for w, lead in srcs:
        rows, cols = w.shape[-2:]
        n_blk = n_steps
        while rows % n_blk or (rows // n_blk) % 16:
            n_blk //= 2
        rep, blk = n_steps // n_blk, rows // n_blk
        in_specs.append(pl.BlockSpec((None,) * len(lead) + (blk, cols),
                                     lambda b, i, lead=lead, rep=rep: lead + ((b * n_inner + i) // rep, 0)))
        out_specs.append(pl.BlockSpec((blk, cols), lambda b, i, rep=rep: ((b * n_inner + i) // rep, 0)))
        out_shapes.append(jax.ShapeDtypeStruct((rows, cols), BF16))
        operands.append(w)
    return in_specs, operands, out_specs, out_shapes


def _run_riders(rest):
    n_r = (len(rest) - 1) // 2
    for src, dst in zip(rest[:n_r], rest[n_r + 1:]):
        dst[...] = src[...].astype(BF16)
    return rest[n_r]


def _ffn_kernel(x_ref, shift_ref, scale_ref, gate_ref, g_ref, wup_ref, wdn_ref, *rest):
    o_ref = _run_riders(rest)
    o_ref[...] = _ffn_apply(x_ref[...], g_ref[...], shift_ref[...], scale_ref[...], gate_ref[...],
                            wup_ref, wdn_ref)


def _token_tiles(n_b, seq, tile):
    lt = min(seq, tile)
    n_seq = max(1, min(n_b, tile // lt))
    assert seq % lt == 0 and n_b % n_seq == 0 and lt % 8 == 0
    return n_seq, lt


def _mod_spec(layer, idx, n_seq, d):
    return pl.BlockSpec((None, None, n_seq, 1, d), lambda b, i: (layer, idx, b, 0, 0))


def _weight(w):
    return _resident(w.shape, lambda b, i: (0, 0))


def _ffn1(x, mods, g_norm, wup, wdn, layer, cast_srcs=()):
    n_b, seq, d = x.shape
    n_seq, lt = _token_tiles(n_b, seq, FFN_TILE)
    grid = (n_b // n_seq, seq // lt)
    x_spec = pl.BlockSpec((n_seq, lt, d), lambda b, i: (b, i, 0))
    r_in, r_ops, r_out, r_shapes = _cast_riders(cast_srcs, *grid)
    return pl.pallas_call(
        _ffn_kernel,
        grid=grid,
        in_specs=[x_spec, _mod_spec(layer, 0, n_seq, d), _mod_spec(layer, 1, n_seq, d),
                  _mod_spec(layer, 2, n_seq, d),
                  pl.BlockSpec((None, None, 1, d), lambda b, i: (layer, 0, 0, 0)),
                  _weight(wup), _weight(wdn)] + r_in,
        out_specs=[x_spec] + r_out,
        out_shape=[jax.ShapeDtypeStruct(x.shape, F32)] + r_shapes,
        compiler_params=_cparams(2),
        name="ffn1",
    )(x, mods, mods, mods, g_norm, wup, wdn, *r_ops)


def _swap_pairs(x):
    w = x.shape[-1]
    lane = lax.broadcasted_iota(jnp.int32, (1, w), 1)
    return jnp.where((lane & 1) == 0, pltpu.roll(x, w - 1, 1), pltpu.roll(x, 1, 1))


def _swap_halves(x, half):
    w = x.shape[-1]
    lane = lax.broadcasted_iota(jnp.int32, (1, w), 1)
    return jnp.where((lane & (2 * half - 1)) < half, pltpu.roll(x, w - half, 1), pltpu.roll(x, half, 1))


def _mix_in_kernel(x_ref, shift_ref, scale_ref, g_ref, win_ref, gqk_ref, gmat_ref,
                   cosr_ref, sinr_ref, cosd_ref, sind_ref, dmask_ref, qdec_ref, kdec_ref, cdec_ref, s0_ref,
                   *rest, rc, sizes, prompt, n_prev):
    if n_prev:
        dkp_ref, dvp_ref = rest[:2]
        rest = rest[2:]
    dk_ref, dv_ref, qa_ref, ka_ref, va_ref, or_ref, snew_ref, s_scr = rest
    @pl.when(pl.program_id(1) == 0)
    def _():
        s_scr[...] = s0_ref[...]

    if n_prev:
        dk_ref[:n_prev] = dkp_ref[...]
        dv_ref[:n_prev] = dvp_ref[...]
    ret_qk, ret_v, diff_qk, diff_v = sizes
    n_seq, lt, d = x_ref.shape
    m = n_seq * lt
    dk_h = ret_qk // RET_HEADS
    dv_h = ret_v // RET_HEADS
    hv = diff_v // DIFF_HEADS
    hb = _pre(x_ref[...], g_ref[...], shift_ref[...], scale_ref[...]).reshape(m, d).astype(BF16)

    widths = (ret_qk, ret_qk, ret_v, ret_v, diff_qk, diff_qk, diff_v)

    def proj(n):
        lo = sum(widths[:n])
        return _dot(hb, win_ref[:, lo:lo + widths[n]])

    def rope(x, partner, cos_ref, sin_ref):
        w = x.shape[-1]
        y = x.reshape(n_seq, lt, w) * cos_ref[...] + partner.reshape(n_seq, lt, w) * sin_ref[...]
        return y.reshape(m, w)

    rq, rk = proj(0), proj(1)
    dq, dk = proj(4), proj(5)
    dv = proj(6)

    rq = rope(rq, _swap_pairs(rq), cosr_ref, sinr_ref)
    rk = rope(rk, _swap_pairs(rk), cosr_ref, sinr_ref) * (dk_h ** -0.5)
    chunks = [(s, c, s * lt + c * rc) for s in range(n_seq) for c in range(lt // rc)]
    kd_t = {(s, c): jnp.transpose(rk[r0:r0 + rc] * kdec_ref[...]).astype(BF16)
            for s, c, r0 in chunks}

    sub = diff_qk // (2 * DIFF_HEADS)
    qk = jnp.concatenate([dq, dk], axis=0)
    gmean = _dot((qk * qk).astype(BF16), gmat_ref[...]) * (1.0 / sub)
    rv, rg = proj(2), proj(3)
    qk = qk * lax.rsqrt(gmean + EPS)
    dq = qk[:m] * gqk_ref[0]
    dk = qk[m:] * gqk_ref[1]
    dq = rope(dq, _swap_halves(dq, sub // 2), cosd_ref, sind_ref) * (sub ** -0.5 * LOG2E)
    dk = rope(dk, _swap_halves(dk, sub // 2), cosd_ref, sind_ref)

    for s in range(n_seq):
        rows = slice(s * lt, (s + 1) * lt)
        for h in range(DIFF_HEADS):
            dv_ref[n_prev, s, pl.ds(h, lt, stride=DIFF_HEADS), :] = dv[rows, h * hv:(h + 1) * hv]
        if prompt:
            dk_t = jnp.transpose(dk[rows])
            dk_ref[n_prev, s] = dk_t
            ka_ref[s] = dk[rows].astype(BF16)
            qa_ref[s] = jnp.transpose(dq[rows]).astype(BF16)
            va_ref[s] = jnp.transpose(dv[rows]).astype(BF16)
        else:
            for j in range(diff_qk // sub):
                dk_ref[n_prev, s, :, j, :] = dk[rows, j * sub:(j + 1) * sub]
            ka_ref[s] = dk[rows].astype(BF16)
            qa_ref[s] = dq[rows].astype(BF16)
            va_ref[s] = dv[rows].astype(BF16)

    lane_q =lax.broadcasted_iota(jnp.int32, (1, ret_qk), 1)
    gate = rg * jax.nn.sigmoid(rg)
    rv_b = rv.astype(BF16)
    zero_blk = jnp.zeros((dk_h, dv_h), BF16)
    def head_cols(h, width):
        return slice(h * width, (h + 1) * width)

    att, upd = {}, {}
    for s, c, r0 in chunks:
        q = rq[r0:r0 + rc]
        q_heads = jnp.concatenate(
            [jnp.where((lane_q >= h * dk_h) & (lane_q < (h + 1) * dk_h), q, 0.0) for h in range(RET_HEADS)],
            axis=0).astype(BF16)
        att[s, c] = (_dot_nt(q_heads, rk[r0:r0 + rc].astype(BF16)) * dmask_ref[...]).astype(BF16)
    for s, c, r0 in chunks:
        upd[s, c] = [_dot(kd_t[s, c][head_cols(h, dk_h)], rv_b[r0:r0 + rc, head_cols(h, dv_h)])
                     for h in range(RET_HEADS)]
    for s in range(n_seq):
        state = [s_scr[s, h] for h in range(RET_HEADS)]
        for c in range(lt // rc):
            r0 = s * lt + c * rc
            s_diag = jnp.concatenate(
                [jnp.concatenate([state[h].astype(BF16) if g == h else zero_blk for g in range(RET_HEADS)], axis=1)
                 for h in range(RET_HEADS)], axis=0)
            inter = _dot((rq[r0:r0 + rc] * qdec_ref[...]).astype(BF16), s_diag)
            for h in range(RET_HEADS):
                cols = head_cols(h, dv_h)
                o = _dot(att[s, c][h * rc:(h + 1) * rc], rv_b[r0:r0 + rc, cols]) + inter[:, cols]
                or_ref[s, c * rc:(c + 1) * rc, cols] = (_rms(o) * gate[r0:r0 + rc, cols]).astype(BF16)
                state[h] = cdec_ref[h] * state[h] + upd[s, c][h]
        for h in range(RET_HEADS):
            s_scr[s, h] = state[h]
            snew_ref[s, h] = state[h]


def _retention_tables(rc, dk_h, dv_h):
    n_h = RET_HEADS
    log_g = np.log1p(-np.exp2(-5.0 - np.arange(n_h, dtype=np.float64)))
    idx = np.arange(rc, dtype=np.float64)
    rel = idx[:, None] - idx[None, :]
    dmask = np.where(rel >= 0, np.exp(log_g[:, None, None] * np.maximum(rel, 0.0)), 0.0)
    q_dec = np.exp(log_g[None, :] * (idx[:, None] + 1.0))
    k_dec = np.exp(log_g[None, :] * (rc - 1.0 - idx[:, None]))
    c_dec = np.exp(log_g * rc)
    tabs = (dmask.reshape(n_h * rc, rc), np.repeat(q_dec, dk_h, axis=1), np.repeat(k_dec, dk_h, axis=1),
            np.broadcast_to(c_dec[:, None, None], (n_h, dk_h, dv_h)))
    return tuple(jnp.asarray(t, F32) for t in tabs)


def _rope_tables(pos, dk_h, sub):
    posf = np.asarray(pos, np.float64)[:, None]
    ret_freq = 1.0 / (ROPE_THETA ** np.linspace(0.0, 1.0, dk_h // 2))
    ang = posf * ret_freq[None, :]
    sign = np.tile(np.array([-1.0, 1.0]), dk_h // 2)
    cos_r = np.tile(np.repeat(np.cos(ang), 2, axis=1), (1, RET_HEADS))
    sin_r = np.tile(np.repeat(np.sin(ang), 2, axis=1) * sign, (1, RET_HEADS))
    rope_freq = 1.0 / (ROPE_THETA ** (np.arange(0, sub, 2, dtype=np.float64) / sub))
    ang = posf * rope_freq[None, :]
    sign = np.concatenate([-np.ones(sub // 2), np.ones(sub // 2)])
    cos_d = np.tile(np.tile(np.cos(ang), (1, 2)), (1, 2 * DIFF_HEADS))
    sin_d = np.tile(np.tile(np.sin(ang), (1, 2)) * sign, (1, 2 * DIFF_HEADS))
    return tuple(jnp.asarray(t, F32) for t in (cos_r, sin_r, cos_d, sin_d))


def _mix_in(x, mods, g_norm, win, gqk, gmat, rope, ret_tabs, s0, s0_layer, sizes, rc, layer, prompt, prev_kv):
    n_b, seq, d = x.shape
    n_prev = 0 if prev_kv is None else prev_kv[0].shape[0]
    ret_qk, ret_v, diff_qk, diff_v = sizes
    n_seq, lt = _token_tiles(n_b, seq, MIX_TILE)
    assert lt % rc == 0
    dk_h, dv_h = ret_qk // RET_HEADS, ret_v // RET_HEADS
    hv = diff_v // DIFF_HEADS

    def tok(width):
        return pl.BlockSpec((n_seq, lt, width), lambda b, i: (b, i, 0))

    def feat(width):
        return pl.BlockSpec((n_seq, width, lt), lambda b, i: (b, 0, i))

    def full(a):
        return _resident(a.shape, lambda b, i, _n=a.ndim: (0,) * _n)

    def pos_tab(a):
        return pl.BlockSpec((lt, a.shape[1]), lambda b, i: (i, 0))

    state_blk = (n_seq, RET_HEADS, dk_h, dv_h)
    if s0_layer is None:
        s0_spec = pl.BlockSpec(state_blk, lambda b, i: (b, 0, 0, 0))
    else:
        s0_spec = pl.BlockSpec((None,) + state_blk, lambda b, i: (s0_layer, b, 0, 0, 0))
    dv_blk, dv_idx = (n_seq, lt * DIFF_HEADS, hv), (lambda b, i: (b, i, 0))
    dv_full = (n_b, seq * DIFF_HEADS, hv)
    if prompt:
        dk_blk, dk_idx, dk_full = (n_seq, diff_qk, lt), (lambda b, i: (b, 0, i)), (n_b, diff_qk, seq)
    else:
        sub = diff_qk // (2 * DIFF_HEADS)
        dk_blk, dk_idx = (n_seq, lt, diff_qk // sub, sub), (lambda b, i: (b, i, 0, 0))
        dk_full = (n_b, seq, diff_qk // sub, sub)

    def stacked_spec(n, blk, idx):
        return pl.BlockSpec((n,) + blk, lambda b, i: (0,) + idx(b, i))

    kv_specs = [stacked_spec(n_prev + 1, dk_blk, dk_idx), stacked_spec(n_prev + 1, dv_blk, dv_idx)]
    kv_shapes = [jax.ShapeDtypeStruct((n_prev + 1,) + dk_full, F32),
                 jax.ShapeDtypeStruct((n_prev + 1,) + dv_full, F32)]
    prev_specs = [stacked_spec(n_prev, dk_blk, dk_idx), stacked_spec(n_prev, dv_blk, dv_idx)] if n_prev else []
    if prompt:
        att_specs = kv_specs + [feat(diff_qk), tok(diff_qk),
                                pl.BlockSpec((n_seq, None, diff_v, lt), lambda b, i: (b, i, 0, 0))]
        att_shapes = kv_shapes + [jax.ShapeDtypeStruct((n_b, diff_qk, seq), BF16),
                                  jax.ShapeDtypeStruct((n_b, seq, diff_qk), BF16),
                                  jax.ShapeDtypeStruct((n_b, seq // lt, diff_v, lt), BF16)]
    else:
        att_specs = kv_specs + [tok(diff_qk), tok(diff_qk), tok(diff_v)]
        att_shapes = kv_shapes + [jax.ShapeDtypeStruct((n_b, seq, diff_qk), BF16),
                                  jax.ShapeDtypeStruct((n_b, seq, diff_qk), BF16),
                                  jax.ShapeDtypeStruct((n_b, seq, diff_v), BF16)]
    return pl.pallas_call(
        functools.partial(_mix_in_kernel, rc=rc, sizes=sizes, prompt=prompt, n_prev=n_prev),
        grid=(n_b // n_seq, seq // lt),
        in_specs=[tok(d), _mod_spec(layer, 3, n_seq, d), _mod_spec(layer, 4, n_seq, d),
                  pl.BlockSpec((None, None, 1, d), lambda b, i: (layer, 1, 0, 0)),
                  _weight(win),
                  pl.BlockSpec((None, 2, 1, diff_qk), lambda b, i: (layer, 0, 0, 0)),
                  full(gmat),
                  pos_tab(rope[0]), pos_tab(rope[1]), pos_tab(rope[2]), pos_tab(rope[3]),
                  full(ret_tabs[0]), full(ret_tabs[1]), full(ret_tabs[2]), full(ret_tabs[3]),
                  s0_spec] + prev_specs,
        out_specs=att_specs + [tok(ret_v), pl.BlockSpec(state_blk, lambda b, i: (b, 0, 0, 0))],
        out_shape=att_shapes + [jax.ShapeDtypeStruct((n_b, seq, ret_v), BF16),
                                jax.ShapeDtypeStruct((n_b,) + state_blk[1:], F32)],
        scratch_shapes=[pltpu.VMEM(state_blk, F32)],
        compiler_params=_cparams(2),
        name="mix_in",
    )(x, mods, mods, g_norm, win, gqk, gmat, *rope, *ret_tabs, s0, *(prev_kv or ()))


def _lambda(wl_ref, lam_init):
    wl = wl_ref[...]
    a = jnp.sum(wl[0:1] * wl[1:2], axis=-1, keepdims=True)
    b = jnp.sum(wl[2:3] * wl[3:4], axis=-1, keepdims=True)
    return jnp.exp(a) - jnp.exp(b) + lam_init


def _softmax_step_t(s_t, v_t, m_ref, l_ref, acc_ref):
    m_prev = m_ref[...]
    m_new = jnp.maximum(m_prev, jnp.max(s_t, axis=0, keepdims=True))
    alpha = jnp.exp2(m_prev - m_new)
    p = jnp.exp2(s_t - m_new)
    l_ref[...] = alpha * l_ref[...] + jnp.sum(p, axis=0, keepdims=True)
    acc_ref[...] = alpha * acc_ref[...] + _dot(v_t, p.astype(BF16))
    m_ref[...] = m_new


def _attn_prompt_kernel(wl_ref, gsub_ref, qt_ref, k_ref, vt_ref, o_ref, *stats, lam_init, extra=()):
    hw, seq = qt_ref.shape
    n_kt, _, tk = vt_ref.shape
    tq = tk
    gw = min(tq, ATTN_GROUP)
    n_g = tq // gw
    feat = lax.broadcasted_iota(jnp.int32, (hw, 1), 0)
    lam = _lambda(wl_ref, lam_init)
    k_chunk = lax.broadcasted_iota(jnp.int32, (gw, gw), 0) // CHUNK
    q_chunk = lax.broadcasted_iota(jnp.int32, (gw, gw), 1) // CHUNK
    diag_mask = jnp.where(k_chunk <= q_chunk, 0.0, -jnp.inf).astype(F32)

    def refs(sub, g):
        return stats[3 * (sub * n_g + g):3 * (sub * n_g + g) + 3]

    units = [(i, j, sub, g, tk if j < i else min(tk, (g + 1) * gw))
             for i in range(seq // tq) for j in range(i + 1) for sub in range(2) for g in range(n_g)]

    def scores(unit):
        i, j, sub, g, n_keys = unit
        q = qt_ref[:, i * tq + g * gw:i * tq + (g + 1) * gw]
        q = jnp.where(feat < hw // 2 if sub == 0 else feat >= hw // 2, q, jnp.zeros_like(q))
        s_t = _dot(k_ref[j * tk:j * tk + n_keys, :], q)
        if j == i:
            tail = s_t[n_keys - gw:] + diag_mask
            s_t = tail if n_keys == gw else jnp.concatenate([s_t[:n_keys - gw], tail], axis=0)
        return s_t

    extra_at = {e * len(units) // max(len(extra), 1): fn for e, fn in enumerate(extra)}
    assert len(extra_at) == len(extra)
    pending = [scores(u) for u in units[:ATTN_LOOKAHEAD]]
    for n, (i, j, sub, g, n_keys) in enumerate(units):
        if n in extra_at:
            extra_at[n]()
        if n + ATTN_LOOKAHEAD < len(units):
            pending.append(scores(units[n + ATTN_LOOKAHEAD]))
        m_ref, l_ref, acc_ref = refs(sub, g)
        if j == 0:
            m_ref[...] = jnp.full(m_ref.shape, -jnp.inf, F32)
            l_ref[...] = jnp.zeros(l_ref.shape, F32)
            acc_ref[...] = jnp.zeros(acc_ref.shape, F32)
        _softmax_step_t(pending.pop(0), vt_ref[j, :, :n_keys], m_ref, l_ref, acc_ref)
        if j == i and sub == 1 and g == n_g - 1:
            o_t = jnp.concatenate(
                [refs(0, c)[2][...] / refs(0, c)[1][...] - lam * (refs(1, c)[2][...] / refs(1, c)[1][...])
                 for c in range(n_g)], axis=1)
            o_t = (o_t * lax.rsqrt(jnp.mean(o_t * o_t, axis=0, keepdims=True) + EPS)
                   * gsub_ref[...] * (1.0 - lam_init))
            o_ref[i * tq:(i + 1) * tq, :] = jnp.transpose(o_t).astype(BF16)


def _attn_prompt(qt, k, vt, wl, gsub_col, lam_init, layer):
    n_b, width, seq = qt.shape
    n_kt, diff_v, tk = vt.shape[1:]
    hw, hv = width // DIFF_HEADS, diff_v // DIFF_HEADS
    gw = min(tk, ATTN_GROUP)
    assert seq == n_kt * tk and tk % CHUNK == 0 and tk % gw == 0
    return pl.pallas_call(
        functools.partial(_attn_prompt_kernel, lam_init=lam_init),
        grid=(n_b, DIFF_HEADS),
        in_specs=[pl.BlockSpec((None,) + wl.shape[1:], lambda b, h: (layer, 0, 0)),
                  pl.BlockSpec((None,) + gsub_col.shape[1:], lambda b, h: (layer, 0, 0)),
                  pl.BlockSpec((None, hw, seq), lambda b, h: (b, h, 0)),
                  pl.BlockSpec((None, seq, hw), lambda b, h: (b, 0, h)),
                  pl.BlockSpec((None, n_kt, hv, tk), lambda b, h: (b, 0, h, 0))],
        out_specs=pl.BlockSpec((None, seq, hv), lambda b, h: (b, 0, h)),
        out_shape=jax.ShapeDtypeStruct((n_b, seq, diff_v), BF16),
        scratch_shapes=[pltpu.VMEM((1, gw), F32), pltpu.VMEM((1, gw), F32), pltpu.VMEM((hv, gw), F32)]
        * (2 * (tk // gw)),
        compiler_params=_cparams(2),
        name="attn_prompt",
    )(wl, gsub_col, qt, k, vt)


def _stack_subheads(q):
    w = q.shape[-1]
    lane = lax.broadcasted_iota(jnp.int32, (1, w), 1)
    zero = jnp.zeros_like(q)
    return jnp.concatenate([jnp.where(lane < w // 2, q, zero), jnp.where(lane >= w // 2, q, zero)], axis=0)


def _softmax_step(s, v, m_ref, l_ref, acc_ref):
    m_prev = m_ref[...]
    m_new = jnp.maximum(m_prev, jnp.max(s, axis=-1, keepdims=True))
    alpha = jnp.exp2(m_prev - m_new)
    p = jnp.exp2(s - m_new)
    l_ref[...] = alpha * l_ref[...] + jnp.sum(p, axis=-1, keepdims=True)
    acc_ref[...] = alpha * acc_ref[...] + _dot(p.astype(BF16), v)
    m_ref[...] = m_new


class _SampleAttn:
    def __init__(self, wl_ref, gsub_ref, q_ref, kct_ref, vc_ref, kn_ref, vn_ref, o_ref, lam_init):
        self.refs = (wl_ref, gsub_ref, q_ref, kct_ref, vc_ref, kn_ref, vn_ref, o_ref)
        self.lam_init = lam_init
        self.hw = q_ref.shape[-1] // DIFF_HEADS
        self.hv = vc_ref.shape[-1]
        self.tk = kct_ref.shape[-1]

    def head_q(self, h):
        return _stack_subheads(self.refs[2][:, h * self.hw:(h + 1) * self.hw])

    def cache_scores(self, h):
        return _dot(self.head_q(h), self.refs[3][h * self.hw:(h + 1) * self.hw, :].astype(BF16))

    def cache_v(self, h):
        return self.refs[4][pl.ds(h, self.tk, stride=DIFF_HEADS), :].astype(BF16)

    def new_scores(self, h):
        return _dot_nt(self.head_q(h), self.refs[5][:, h * self.hw:(h + 1) * self.hw])

    def new_v(self, h):
        return self.refs[6][:, h * self.hv:(h + 1) * self.hv]

    def finish(self, h, acc, l):
        rows = acc.shape[0] // 2
        o = acc[:rows] / l[:rows] - _lambda(self.refs[0], self.lam_init) * (acc[rows:] / l[rows:])
        self.refs[7][:, h * self.hv:(h + 1) * self.hv] = (
            _rms(o) * self.refs[1][...] * (1.0 - self.lam_init)).astype(BF16)

    def whole_head(self, h, s_cache=None):
        s_cache = self.cache_scores(h) if s_cache is None else s_cache
        s_new = self.new_scores(h)
        m = jnp.maximum(jnp.max(s_cache, axis=-1, keepdims=True), jnp.max(s_new, axis=-1, keepdims=True))
        p_c = jnp.exp2(s_cache - m)
        p_n = jnp.exp2(s_new - m)
        l = jnp.sum(p_c, axis=-1, keepdims=True) + jnp.sum(p_n, axis=-1, keepdims=True)
        acc = _dot(p_c.astype(BF16), self.cache_v(h)) + _dot(p_n.astype(BF16), self.new_v(h))
        self.finish(h, acc, l)


def _attn_sample_kernel(wl_ref, gsub_ref, q_ref, kct_ref, vc_ref, kn_ref, vn_ref, o_ref,
                        m_ref, l_ref, acc_ref, *, lam_init, single_tile):
    j = pl.program_id(1)
    att = _SampleAttn(wl_ref, gsub_ref, q_ref, kct_ref, vc_ref, kn_ref, vn_ref, o_ref, lam_init)
    scores = [att.cache_scores(h) for h in range(DIFF_HEADS)]

    if single_tile:
        for h in range(DIFF_HEADS):
            att.whole_head(h, scores[h])
        return

    @pl.when(j == 0)
    def _():
        m_ref[...] = jnp.full(m_ref.shape, -jnp.inf, F32)
        l_ref[...] = jnp.zeros(l_ref.shape, F32)
        acc_ref[...] = jnp.zeros(acc_ref.shape, F32)

    for h in range(DIFF_HEADS):
        _softmax_step(scores[h], att.cache_v(h), m_ref.at[h], l_ref.at[h], acc_ref.at[h])

    @pl.when(j == pl.num_programs(1) - 1)
    def _():
        for h in range(DIFF_HEADS):
            _softmax_step(att.new_scores(h), att.new_v(h), m_ref.at[h], l_ref.at[h], acc_ref.at[h])
            att.finish(h, acc_ref[h], l_ref[h])


def _attn_sample(q, k_new, v_new, cache_kt, cache_vi, wl, gsub_row, lam_init, layer):
    n_b, seq, width = q.shape
    past = cache_kt.shape[-1]
    hv = cache_vi.shape[-1]
    vw = v_new.shape[-1]
    assert past % CHUNK == 0 and seq <= CHUNK
    tk = min(past, CACHE_TK)
    assert past % tk == 0
    return pl.pallas_call(
        functools.partial(_attn_sample_kernel, lam_init=lam_init, single_tile=past == tk),
        grid=(n_b, past // tk),
        in_specs=[pl.BlockSpec((None,) + wl.shape[1:], lambda b, j: (layer, 0, 0)),
                  pl.BlockSpec((None,) + gsub_row.shape[1:], lambda b, j: (layer, 0, 0)),
                  pl.BlockSpec((None, seq, width), lambda b, j: (b, 0, 0)),
                  pl.BlockSpec((None, None, width, tk), lambda b, j: (layer, b, 0, j)),
                  pl.BlockSpec((None, None, tk * DIFF_HEADS, hv), lambda b, j: (layer, b, j, 0)),
                  pl.BlockSpec((None, seq, width), lambda b, j: (b, 0, 0)),
                  pl.BlockSpec((None, seq, vw), lambda b, j: (b, 0, 0))],
        out_specs=pl.BlockSpec((None, seq, vw), lambda b, j: (b, 0, 0)),
        out_shape=jax.ShapeDtypeStruct((n_b, seq, vw), BF16),
        scratch_shapes=[pltpu.VMEM((DIFF_HEADS, 2 * seq, 1), F32), pltpu.VMEM((DIFF_HEADS, 2 * seq, 1), F32),
                        pltpu.VMEM((DIFF_HEADS, 2 * seq, hv), F32)],
        compiler_params=_cparams(2),
        name="attn_sample",
    )(wl, gsub_row, q, cache_kt, cache_vi, k_new, v_new)


def _attn_both_kernel(wl_ref, gcol_ref, qt_ref, k_ref, vt_ref, grow_ref, q_ref, kct_ref, vc_ref, kn_ref, vn_ref,
                      op_ref, os_ref, *stats, lam_init):
    att = _SampleAttn(wl_ref, grow_ref, q_ref, kct_ref, vc_ref, kn_ref, vn_ref, os_ref, lam_init)
    s_cache = {}
    pieces = []
    for h in range(DIFF_HEADS):
        pieces.append(lambda h=h: s_cache.__setitem__(h, att.cache_scores(h)))
        if h > 0:
            pieces.append(lambda h=h: att.whole_head(h - 1, s_cache.pop(h - 1)))
    pieces.append(lambda: att.whole_head(DIFF_HEADS - 1, s_cache.pop(DIFF_HEADS - 1)))
    _attn_prompt_kernel(wl_ref, gcol_ref, qt_ref, k_ref, vt_ref, op_ref, *stats, lam_init=lam_init, extra=pieces)


def _attn_both(qt, k, vt, q_s, k_new, v_new, cache_kt, cache_vi, wl, gsub_col, gsub_row, lam_init, layer):
    n_b, width, seq = qt.shape
    n_kt, diff_v, tk = vt.shape[1:]
    n_s, seq_s, _ = q_s.shape
    past = cache_kt.shape[-1]
    hw, hv = width // DIFF_HEADS, diff_v // DIFF_HEADS
    gw = min(tk, ATTN_GROUP)
    assert seq == n_kt * tk and tk % CHUNK == 0 and tk % gw == 0
    assert n_s == n_b * DIFF_HEADS and past % CHUNK == 0 and seq_s <= CHUNK

    def sample(blk):
        return pl.BlockSpec((None,) + blk, lambda b, h: (b * DIFF_HEADS + h, 0, 0))

    return pl.pallas_call(
        functools.partial(_attn_both_kernel, lam_init=lam_init),
        grid=(n_b, DIFF_HEADS),
        in_specs=[pl.BlockSpec((None,) + wl.shape[1:], lambda b, h: (layer, 0, 0)),
                  pl.BlockSpec((None,) + gsub_col.shape[1:], lambda b, h: (layer, 0, 0)),
                  pl.BlockSpec((None, hw, seq), lambda b, h: (b, h, 0)),
                  pl.BlockSpec((None, seq, hw), lambda b, h: (b, 0, h)),
                  pl.BlockSpec((None, n_kt, hv, tk), lambda b, h: (b, 0, h, 0)),
                  pl.BlockSpec((None,) + gsub_row.shape[1:], lambda b, h: (layer, 0, 0)),
                  sample((seq_s, width)),
                  pl.BlockSpec((None, None, width, past), lambda b, h: (layer, b * DIFF_HEADS + h, 0, 0)),
                  pl.BlockSpec((None, None, past * DIFF_HEADS, hv), lambda b, h: (layer, b * DIFF_HEADS + h, 0, 0)),
                  sample((seq_s, width)), sample((seq_s, diff_v))],
        out_specs=[pl.BlockSpec((None, seq, hv), lambda b, h: (b, 0, h)), sample((seq_s, diff_v))],
        out_shape=[jax.ShapeDtypeStruct((n_b, seq, diff_v), BF16),
                   jax.ShapeDtypeStruct((n_s, seq_s, diff_v), BF16)],
        scratch_shapes=[pltpu.VMEM((1, gw), F32), pltpu.VMEM((1, gw), F32), pltpu.VMEM((hv, gw), F32)]
        * (2 * (tk // gw)),
        compiler_params=_cparams(2),
        name="attn_both",
    )(wl, gsub_col, qt, k, vt, gsub_row, q_s, cache_kt, cache_vi, k_new, v_new)


def _mix_out_kernel(x_ref, or_ref, od_ref, gate1_ref, shift_ref, scale_ref, gate2_ref, g_ref,
                    wout_ref, wup_ref, wdn_ref, *rest):
    o_ref = _run_riders(rest)
    n_seq, lt, d = x_ref.shape
    m = n_seq * lt
    rw = or_ref.shape[-1]
    mix = (_dot(or_ref[...].reshape(m, rw), wout_ref[:rw, :])
           + _dot(od_ref[...].reshape(m, od_ref.shape[-1]), wout_ref[rw:, :]))
    x3 = x_ref[...] + gate1_ref[...] * mix.reshape(n_seq, lt, d)
    o_ref[...] = _ffn_apply(x3, g_ref[...], shift_ref[...], scale_ref[...], gate2_ref[...], wup_ref, wdn_ref)


def _mix_out(x, o_r, o_d, mods, g_norm, wout, wup, wdn, layer, cast_srcs=()):
    n_b, seq, d = x.shape
    n_seq, lt = _token_tiles(n_b, seq, OUT_TILE)
    grid = (n_b // n_seq, seq // lt)

    def tok(width):
        return pl.BlockSpec((n_seq, lt, width), lambda b, i: (b, i, 0))

    r_in, r_ops, r_out, r_shapes = _cast_riders(cast_srcs, *grid)
    return pl.pallas_call(
        _mix_out_kernel,
        grid=grid,
        in_specs=[tok(d), tok(o_r.shape[-1]), tok(o_d.shape[-1]),
                  _mod_spec(layer, 5, n_seq, d), _mod_spec(layer, 6, n_seq, d),
                  _mod_spec(layer, 7, n_seq, d), _mod_spec(layer, 8, n_seq, d),
                  pl.BlockSpec((None, None, 1, d), lambda b, i: (layer, 2, 0, 0)),
                  _weight(wout), _weight(wup), _weight(wdn)] + r_in,
        out_specs=[tok(d)] + r_out,
        out_shape=[jax.ShapeDtypeStruct(x.shape, F32)] + r_shapes,
        compiler_params=_cparams(2),
        name="mix_out",
    )(x, o_r, o_d, mods, mods, mods, mods, g_norm, wout, wup, wdn, *r_ops)


def kernel(x_prompt, x_sample, cache_k, cache_v, state_ret, c_prompt, c_sample, w_ada, b_ada, g_norm,
           w_ff_up, w_ff_down, w_in, w_out, g_qk, w_lambda, g_sub):
    depth = w_in.shape[0]
    n_p, seq_p, d = x_prompt.shape
    n_s, seq_s, _ = x_sample.shape
    past = cache_k.shape[2]
    dk_h, dv_h = state_ret.shape[-2], state_ret.shape[-1]
    sub = cache_k.shape[-1]
    hv = cache_v.shape[-1]
    ret_qk, ret_v = RET_HEADS * dk_h, RET_HEADS * dv_h
    diff_qk, diff_v = DIFF_HEADS * 2 * sub, DIFF_HEADS * hv
    sizes = (ret_qk, ret_v, diff_qk, diff_v)

    mods = _ada_mods(jnp.concatenate([c_prompt, c_sample], axis=0), w_ada, b_ada)
    mods = mods[:, :, :, None, :]
    mods_p, mods_s = mods[:, :, :n_p], mods[:, :, n_p:]

    rope_p = _rope_tables(np.arange(seq_p), dk_h, sub)
    rope_s = _rope_tables(past + np.arange(seq_s), dk_h, sub)
    rc_p, rc_s = min(RET_CHUNK, seq_p), min(RET_CHUNK, seq_s)
    tabs_p = _retention_tables(rc_p, dk_h, dv_h)
    tabs_s = _retention_tables(rc_s, dk_h, dv_h)
    group = np.arange(diff_qk) // sub
    gmat = jnp.asarray(group[:, None] == group[None, :], BF16)
    s0_p = jnp.zeros((n_p, RET_HEADS, dk_h, dv_h), F32)
    cache_kt = jnp.transpose(cache_k, (0, 1, 3, 4, 5, 2)).reshape(depth, n_s, diff_qk, past)
    cache_vi = cache_v.reshape(depth, n_s, past * DIFF_HEADS, hv)

    def first_ffn(l):
        return [(w_ff_up, (l, 0)), (w_ff_down, (l, 0))]

    def rest_of_layer(l):
        return [(w_ff_up, (l, 1)), (w_ff_down, (l, 1)), (w_in, (l,)), (w_out, (l,))]

    w_names = ("up0", "dn0", "up1", "dn1", "win", "wout")
    weights = {0: {"up0": w_ff_up[0, 0].astype(BF16), "dn0": w_ff_down[0, 0].astype(BF16)}}
    g_norm4 = g_norm[:, :, None, :]
    gqk = jnp.tile(g_qk, (1, 1, diff_qk // sub))[:, :, None, :]
    wl = w_lambda.astype(F32)
    gsub_row = g_sub[:, None, :]
    gsub_col = g_sub[:, :, None]

    yp, ys = x_prompt, x_sample
    kv_p, kv_s, sp, ssl = None, None, [], []
    for l in range(depth):
        lam_init = 0.8 - 0.6 * math.exp(-0.3 * l)

        w = weights[l]
        yp, *casts = _ffn1(yp, mods_p, g_norm4, w["up0"], w["dn0"], l,
                           rest_of_layer(l) if len(w) < len(w_names) else ())
        w.update(zip(w_names[2:], casts))
        ys, = _ffn1(ys, mods_s, g_norm4, w["up0"], w["dn0"], l)
        *kv_p, q_t, k_p, v_t, or_p, s_new = _mix_in(yp, mods_p, g_norm4, w["win"], gqk, gmat, rope_p, tabs_p,
                                                     s0_p, None, sizes, rc_p, l, True, kv_p)
        sp.append(s_new)
        *kv_s, q_a, k_a, v_a, or_s, s_new = _mix_in(ys, mods_s, g_norm4, w["win"], gqk, gmat, rope_s, tabs_s,
                                                     state_ret, l, sizes, rc_s, l, False, kv_s)
        ssl.append(s_new)
        if n_s == n_p * DIFF_HEADS and past <= CACHE_TK:
            od_p, od_s = _attn_both(q_t, k_p, v_t, q_a, k_a, v_a, cache_kt, cache_vi, wl, gsub_col, gsub_row,
                                    lam_init, l)
        else:
            od_p = _attn_prompt(q_t, k_p, v_t, wl, gsub_col, lam_init, l)
            od_s = _attn_sample(q_a, k_a, v_a, cache_kt, cache_vi, wl, gsub_row, lam_init, l)
        nxt = first_ffn(l + 1) + rest_of_layer(l + 1) if l + 1 < depth else ()
        yp, *casts = _mix_out(yp, or_p, od_p, mods_p, g_norm4, w["wout"], w["up1"], w["dn1"], l, nxt)
        if nxt:
            weights[l + 1] = dict(zip(w_names, casts))
        ys, = _mix_out(ys, or_s, od_s, mods_s, g_norm4, w["wout"], w["up1"], w["dn1"], l)

    k_prompt = jnp.transpose(kv_p[0].reshape(depth, n_p, DIFF_HEADS, 2, sub, seq_p), (0, 1, 5, 2, 3, 4))
    v_prompt = kv_p[1].reshape(depth, n_p, seq_p, DIFF_HEADS, hv)
    k_sample = kv_s[0].reshape(depth, n_s, seq_s, DIFF_HEADS, 2, sub)
    v_sample = kv_s[1].reshape(depth, n_s, seq_s, DIFF_HEADS, hv)
    return (yp, ys, k_prompt, v_prompt, jnp.stack(sp), k_sample, v_sample, jnp.stack(ssl))
```

```python
import functools
import math

import numpy as np
import jax
import jax.numpy as jnp
from jax import lax
from jax.experimental import pallas as pl
from jax.experimental.pallas import tpu as pltpu

F32 = jnp.float32
BF16 = jnp.bfloat16

CHUNK = 64
RET_HEADS = 4
DIFF_HEADS = 4
ROPE_THETA = 10000.0
EPS = 1e-6
LOG2E = math.log2(math.e)
MIX_TILE = 512
FFN_TILE = 1024
OUT_TILE = 512
FF_CHUNK = 256
RET_CHUNK = 256
ATTN_GROUP = 256
ONES_ROWS = 16
ATTN_LOOKAHEAD = 3
CACHE_TK = 4096
VMEM_LIMIT = 56 * 1024 * 1024


def _cparams(n_axes):
    return pltpu.CompilerParams(dimension_semantics=("arbitrary",) * n_axes,
                                vmem_limit_bytes=VMEM_LIMIT)


def _resident(block_shape, index_map):
    return pl.BlockSpec(block_shape, index_map, pipeline_mode=pl.Buffered(1))


def _rms(x):
    return x * lax.rsqrt(jnp.mean(x * x, axis=-1, keepdims=True) + EPS)


def _dot(a, b):
    return jnp.dot(a, b, preferred_element_type=F32)


def _dot_nt(a, b):
    return lax.dot_general(a, b, (((1,), (1,)), ((), ())), preferred_element_type=F32)


def _split_bf16(x):
    hi = x.astype(BF16)
    return hi, (x - hi.astype(F32)).astype(BF16)


def _ada_kernel(c_ref, w_ref, b_ref, o_ref):
    n = c_ref.shape[0]
    c = c_ref[...]
    a_hi, a_lo = _split_bf16(c * jax.nn.sigmoid(c))
    w_hi, w_lo = _split_bf16(w_ref[...])
    a_both = jnp.concatenate([a_hi.astype(F32), a_lo.astype(F32)], axis=0).astype(BF16)
    first = _dot(a_both, w_hi)
    o_ref[...] = first[:n] + first[n:] + _dot(a_hi, w_lo) + b_ref[...]


def _ada_mods(c_all, w_ada, b_ada):
    depth, d, n_out = w_ada.shape
    n_seq = c_all.shape[0]
    n_blk = n_out // d
    return pl.pallas_call(
        _ada_kernel,
        grid=(depth, n_blk),
        in_specs=[
            pl.BlockSpec((n_seq, d), lambda l, j: (0, 0)),
            pl.BlockSpec((None, d, d), lambda l, j: (l, 0, j)),
            pl.BlockSpec((None, None, 1, d), lambda l, j: (l, j, 0, 0)),
        ],
        out_specs=pl.BlockSpec((None, None, n_seq, d), lambda l, j: (l, j, 0, 0)),
        out_shape=jax.ShapeDtypeStruct((depth, n_blk, n_seq, d), F32),
        compiler_params=_cparams(2),
        name="ada_mods",
    )(c_all, w_ada, b_ada.reshape(depth, n_blk, 1, d))


def _pre(x3, g, shift, scale):
    return _rms(x3) * g * (1.0 + scale) + shift


def _ffn_apply(x3, g, shift, scale, gate, wup_ref, wdn_ref):
    n_seq, lt, d = x3.shape
    d_ff = wdn_ref.shape[0]
    hb = _pre(x3, g, shift, scale).reshape(n_seq * lt, d).astype(BF16)
    acc = jnp.zeros((n_seq * lt, d), F32)
    for c in range(d_ff // FF_CHUNK):
        lo = c * FF_CHUNK
        a = _dot(hb, wup_ref[:, lo:lo + FF_CHUNK])
        b = _dot(hb, wup_ref[:, d_ff + lo:d_ff + lo + FF_CHUNK])
        act = (a * jax.nn.sigmoid(a) * b).astype(BF16)
        acc = acc + _dot(act, wdn_ref[lo:lo + FF_CHUNK, :])
    return x3 + (0.5 * gate) * acc.reshape(n_seq, lt, d)


def _cast_riders(srcs, n_outer, n_inner):
    n_steps = n_outer * n_inner
    in_specs, operands, out_specs, out_shapes = [], [], [], []
    for w, lead in srcs:
        rows, cols = w.shape[-2:]
        n_blk = n_steps
        while rows % n_blk or (rows // n_blk) % 16:
            n_blk //= 2
        rep, blk = n_steps // n_blk, rows // n_blk
        in_specs.append(pl.BlockSpec((None,) * len(lead) + (blk, cols),
                                     lambda b, i, lead=lead, rep=rep: lead + ((b * n_inner + i) // rep, 0)))
        out_specs.append(pl.BlockSpec((blk, cols), lambda b, i, rep=rep: ((b * n_inner + i) // rep, 0)))
        out_shapes.append(jax.ShapeDtypeStruct((rows, cols), BF16))
        operands.append(w)
    return in_specs, operands, out_specs, out_shapes


def _run_riders(rest):
    n_r = (len(rest) - 1) // 2
    for src, dst in zip(rest[:n_r], rest[n_r + 1:]):
        dst[...] = src[...].astype(BF16)
    return rest[n_r]


def _ffn_kernel(x_ref, shift_ref, scale_ref, gate_ref, g_ref, wup_ref, wdn_ref, *rest):
    o_ref = _run_riders(rest)
    o_ref[...] = _ffn_apply(x_ref[...], g_ref[...], shift_ref[...], scale_ref[...], gate_ref[...],
                            wup_ref, wdn_ref)


def _token_tiles(n_b, seq, tile):
    lt = min(seq, tile)
    n_seq = max(1, min(n_b, tile // lt))
    assert seq % lt == 0 and n_b % n_seq == 0 and lt % 8 == 0
    return n_seq, lt


def _mod_spec(layer, idx, n_seq, d):
    return pl.BlockSpec((None, None, n_seq, 1, d), lambda b, i: (layer, idx, b, 0, 0))


def _weight(w):
    return _resident(w.shape, lambda b, i: (0, 0))


def _ffn1(x, mods, g_norm, wup, wdn, layer, cast_srcs=()):
    n_b, seq, d = x.shape
    n_seq, lt = _token_tiles(n_b, seq, FFN_TILE)
    grid = (n_b // n_seq, seq // lt)
    x_spec = pl.BlockSpec((n_seq, lt, d), lambda b, i: (b, i, 0))
    r_in, r_ops, r_out, r_shapes = _cast_riders(cast_srcs, *grid)
    return pl.pallas_call(
        _ffn_kernel,
        grid=grid,
        in_specs=[x_spec, _mod_spec(layer, 0, n_seq, d), _mod_spec(layer, 1, n_seq, d),
                  _mod_spec(layer, 2, n_seq, d),
                  pl.BlockSpec((None, None, 1, d), lambda b, i: (layer, 0, 0, 0)),
                  _weight(wup), _weight(wdn)] + r_in,
        out_specs=[x_spec] + r_out,
        out_shape=[jax.ShapeDtypeStruct(x.shape, F32)] + r_shapes,
        compiler_params=_cparams(2),
        name="ffn1",
    )(x, mods, mods, mods, g_norm, wup, wdn, *r_ops)


def _swap_pairs(x):
    w = x.shape[-1]
    lane = lax.broadcasted_iota(jnp.int32, (1, w), 1)
    return jnp.where((lane & 1) == 0, pltpu.roll(x, w - 1, 1), pltpu.roll(x, 1, 1))


def _swap_halves(x, half):
    w = x.shape[-1]
    lane = lax.broadcasted_iota(jnp.int32, (1, w), 1)
    return jnp.where((lane & (2 * half - 1)) < half, pltpu.roll(x, w - half, 1), pltpu.roll(x, half, 1))


def _mix_in_kernel(x_ref, shift_ref, scale_ref, g_ref, win_ref, gqk_ref, gmat_ref,
                   cosr_ref, sinr_ref, cosd_ref, sind_ref, dmask_ref, qdec_ref, kdec_ref, cdec_ref, s0_ref,
                   *rest, rc, sizes, prompt, n_prev):
    if n_prev:
        dkp_ref, dvp_ref, sp_ref = rest[:3]
        rest = rest[3:]
    dk_ref, dv_ref, snew_ref, qa_ref, ka_ref, va_ref, or_ref, s_scr = rest
    @pl.when(pl.program_id(1) == 0)
    def _():
        s_scr[...] = s0_ref[...]

    if n_prev:
        dk_ref[:n_prev] = dkp_ref[...]
        dv_ref[:n_prev] = dvp_ref[...]
        snew_ref[:n_prev] = sp_ref[...]
    ret_qk, ret_v, diff_qk, diff_v = sizes
    n_seq, lt, d = x_ref.shape
    m = n_seq * lt
    dk_h = ret_qk // RET_HEADS
    dv_h = ret_v // RET_HEADS
    hv = diff_v // DIFF_HEADS
    hb = _pre(x_ref[...], g_ref[...], shift_ref[...], scale_ref[...]).reshape(m, d).astype(BF16)

    widths = (ret_qk, ret_qk, ret_v, ret_v, diff_qk, diff_qk, diff_v)

    def proj(n):
        lo = sum(widths[:n])
        return _dot(hb, win_ref[:, lo:lo + widths[n]])

    def rope(x, partner, cos_ref, sin_ref):
        w = x.shape[-1]
        y = x.reshape(n_seq, lt, w) * cos_ref[...] + partner.reshape(n_seq, lt, w) * sin_ref[...]
        return y.reshape(m, w)

    rq, rk = proj(0), proj(1)
    dq, dk = proj(4), proj(5)
    dv = proj(6)

    rq = rope(rq, _swap_pairs(rq), cosr_ref, sinr_ref)
    rk = rope(rk, _swap_pairs(rk), cosr_ref, sinr_ref) * (dk_h ** -0.5)
    chunks = [(s, c, s * lt + c * rc) for s in range(n_seq) for c in range(lt // rc)]
    kd_t = {(s, c): jnp.transpose(rk[r0:r0 + rc] * kdec_ref[...]).astype(BF16)
            for s, c, r0 in chunks}

    sub = diff_qk // (2 * DIFF_HEADS)
    qk = jnp.concatenate([dq, dk], axis=0)
    gmean = _dot((qk * qk).astype(BF16), gmat_ref[...]) * (1.0 / sub)
    rv, rg = proj(2), proj(3)
    qk = qk * lax.rsqrt(gmean + EPS)
    dq = qk[:m] * gqk_ref[0]
    dk = qk[m:] * gqk_ref[1]
    dq = rope(dq, _swap_halves(dq, sub // 2), cosd_ref, sind_ref) * (sub ** -0.5 * LOG2E)
    dk = rope(dk, _swap_halves(dk, sub // 2), cosd_ref, sind_ref)

    for s in range(n_seq):
        rows = slice(s * lt, (s + 1) * lt)
        for h in range(DIFF_HEADS):
            dv_ref[n_prev, s, pl.ds(h, lt, stride=DIFF_HEADS), :] = dv[rows, h * hv:(h + 1) * hv]
        if prompt:
            dk_t = jnp.transpose(dk[rows])
            dk_ref[n_prev, s] = dk_t
            ka_ref[s] = dk[rows].astype(BF16)
            qa_ref[s] = jnp.transpose(dq[rows]).astype(BF16)
            v_t = jnp.transpose(dv[rows]).astype(BF16)
            ones = jnp.ones((ONES_ROWS, lt), BF16)
            va_ref[s] = jnp.concatenate(
                [part for h in range(DIFF_HEADS) for part in (v_t[h * hv:(h + 1) * hv], ones)], axis=0)
        else:
            for j in range(diff_qk // sub):
                dk_ref[n_prev, s, :, j, :] = dk[rows, j * sub:(j + 1) * sub]
            ka_ref[s] = dk[rows].astype(BF16)
            qa_ref[s] = dq[rows].astype(BF16)
            va_ref[s] = dv[rows].astype(BF16)

    lane_q =lax.broadcasted_iota(jnp.int32, (1, ret_qk), 1)
    gate = rg * jax.nn.sigmoid(rg)
    rv_b = rv.astype(BF16)
    zero_blk = jnp.zeros((dk_h, dv_h), BF16)
    def head_cols(h, width):
        return slice(h * width, (h + 1) * width)

    att, upd = {}, {}
    for s, c, r0 in chunks:
        q = rq[r0:r0 + rc]
        q_heads = jnp.concatenate(
            [jnp.where((lane_q >= h * dk_h) & (lane_q < (h + 1) * dk_h), q, 0.0) for h in range(RET_HEADS)],
            axis=0).astype(BF16)
        att[s, c] = (_dot_nt(q_heads, rk[r0:r0 + rc].astype(BF16)) * dmask_ref[...]).astype(BF16)
    for s, c, r0 in chunks:
        upd[s, c] = [_dot(kd_t[s, c][head_cols(h, dk_h)], rv_b[r0:r0 + rc, head_cols(h, dv_h)])
                     for h in range(RET_HEADS)]
    for s in range(n_seq):
        state = [s_scr[s, h] for h in range(RET_HEADS)]
        for c in range(lt // rc):
            r0 = s * lt + c * rc
            s_diag = jnp.concatenate(
                [jnp.concatenate([state[h].astype(BF16) if g == h else zero_blk for g in range(RET_HEADS)], axis=1)
                 for h in range(RET_HEADS)], axis=0)
            inter = _dot((rq[r0:r0 + rc] * qdec_ref[...]).astype(BF16), s_diag)
            for h in range(RET_HEADS):
                cols = head_cols(h, dv_h)
                o = _dot(att[s, c][h * rc:(h + 1) * rc], rv_b[r0:r0 + rc, cols]) + inter[:, cols]
                or_ref[s, c * rc:(c + 1) * rc, cols] = (_rms(o) * gate[r0:r0 + rc, cols]).astype(BF16)
                state[h] = cdec_ref[h] * state[h] + upd[s, c][h]
        for h in range(RET_HEADS):
            s_scr[s, h] = state[h]
            snew_ref[n_prev, s, h] = state[h]


def _retention_tables(rc, dk_h, dv_h):
    n_h = RET_HEADS
    log_g = np.log1p(-np.exp2(-5.0 - np.arange(n_h, dtype=np.float64)))
    idx = np.arange(rc, dtype=np.float64)
    rel = idx[:, None] - idx[None, :]
    dmask = np.where(rel >= 0, np.exp(log_g[:, None, None] * np.maximum(rel, 0.0)), 0.0)
    q_dec = np.exp(log_g[None, :] * (idx[:, None] + 1.0))
    k_dec = np.exp(log_g[None, :] * (rc - 1.0 - idx[:, None]))
    c_dec = np.exp(log_g * rc)
    tabs = (dmask.reshape(n_h * rc, rc), np.repeat(q_dec, dk_h, axis=1), np.repeat(k_dec, dk_h, axis=1),
            np.broadcast_to(c_dec[:, None, None], (n_h, dk_h, dv_h)))
    return tuple(jnp.asarray(t, F32) for t in tabs)


def _rope_tables(pos, dk_h, sub):
    posf = np.asarray(pos, np.float64)[:, None]
    ret_freq = 1.0 / (ROPE_THETA ** np.linspace(0.0, 1.0, dk_h // 2))
    ang = posf * ret_freq[None, :]
    sign = np.tile(np.array([-1.0, 1.0]), dk_h // 2)
    cos_r = np.tile(np.repeat(np.cos(ang), 2, axis=1), (1, RET_HEADS))
    sin_r = np.tile(np.repeat(np.sin(ang), 2, axis=1) * sign, (1, RET_HEADS))
    rope_freq = 1.0 / (ROPE_THETA ** (np.arange(0, sub, 2, dtype=np.float64) / sub))
    ang = posf * rope_freq[None, :]
    sign = np.concatenate([-np.ones(sub // 2), np.ones(sub // 2)])
    cos_d = np.tile(np.tile(np.cos(ang), (1, 2)), (1, 2 * DIFF_HEADS))
    sin_d = np.tile(np.tile(np.sin(ang), (1, 2)) * sign, (1, 2 * DIFF_HEADS))
    return tuple(jnp.asarray(t, F32) for t in (cos_r, sin_r, cos_d, sin_d))


def _mix_in(x, mods, g_norm, win, gqk, gmat, rope, ret_tabs, s0, s0_layer, sizes, rc, layer, prompt, prev_kv):
    n_b, seq, d = x.shape
    n_prev = 0 if prev_kv is None else prev_kv[0].shape[0]
    ret_qk, ret_v, diff_qk, diff_v = sizes
    n_seq, lt = _token_tiles(n_b, seq, MIX_TILE)
    assert lt % rc == 0
    dk_h, dv_h = ret_qk // RET_HEADS, ret_v // RET_HEADS
    hv = diff_v // DIFF_HEADS

    def tok(width):
        return pl.BlockSpec((n_seq, lt, width), lambda b, i: (b, i, 0))

    def feat(width):
        return pl.BlockSpec((n_seq, width, lt), lambda b, i: (b, 0, i))

    def full(a):
        return _resident(a.shape, lambda b, i, _n=a.ndim: (0,) * _n)

    def pos_tab(a):
        return pl.BlockSpec((lt, a.shape[1]), lambda b, i: (i, 0))

    state_blk = (n_seq, RET_HEADS, dk_h, dv_h)
    if s0_layer is None:
        s0_spec = pl.BlockSpec(state_blk, lambda b, i: (b, 0, 0, 0))
    else:
        s0_spec = pl.BlockSpec((None,) + state_blk, lambda b, i: (s0_layer, b, 0, 0, 0))
    dv_blk, dv_idx = (n_seq, lt * DIFF_HEADS, hv), (lambda b, i: (b, i, 0))
    dv_full = (n_b, seq * DIFF_HEADS, hv)
    if prompt:
        dk_blk, dk_idx, dk_full = (n_seq, diff_qk, lt), (lambda b, i: (b, 0, i)), (n_b, diff_qk, seq)
    else:
        sub = diff_qk // (2 * DIFF_HEADS)
        dk_blk, dk_idx = (n_seq, lt, diff_qk // sub, sub), (lambda b, i: (b, i, 0, 0))
        dk_full = (n_b, seq, diff_qk // sub, sub)

    def stacked_spec(n, blk, idx):
        return pl.BlockSpec((n,) + blk, lambda b, i: (0,) + idx(b, i))

    st_idx = lambda b, i: (b, 0, 0, 0)
    kv_specs = [stacked_spec(n_prev + 1, dk_blk, dk_idx), stacked_spec(n_prev + 1, dv_blk, dv_idx),
                stacked_spec(n_prev + 1, state_blk, st_idx)]
    kv_shapes = [jax.ShapeDtypeStruct((n_prev + 1,) + dk_full, F32),
                 jax.ShapeDtypeStruct((n_prev + 1,) + dv_full, F32),
                 jax.ShapeDtypeStruct((n_prev + 1, n_b) + state_blk[1:], F32)]
    prev_specs = [stacked_spec(n_prev, dk_blk, dk_idx), stacked_spec(n_prev, dv_blk, dv_idx),
                  stacked_spec(n_prev, state_blk, st_idx)] if n_prev else []
    if prompt:
        v_rows = diff_v + DIFF_HEADS * ONES_ROWS
        att_specs = kv_specs + [feat(diff_qk), tok(diff_qk),
                                pl.BlockSpec((n_seq, None, v_rows, lt), lambda b, i: (b, i, 0, 0))]
        att_shapes = kv_shapes + [jax.ShapeDtypeStruct((n_b, diff_qk, seq), BF16),
                                  jax.ShapeDtypeStruct((n_b, seq, diff_qk), BF16),
                                  jax.ShapeDtypeStruct((n_b, seq // lt, v_rows, lt), BF16)]
    else:
        att_specs = kv_specs + [tok(diff_qk), tok(diff_qk), tok(diff_v)]
        att_shapes = kv_shapes + [jax.ShapeDtypeStruct((n_b, seq, diff_qk), BF16),
                                  jax.ShapeDtypeStruct((n_b, seq, diff_qk), BF16),
                                  jax.ShapeDtypeStruct((n_b, seq, diff_v), BF16)]
    return pl.pallas_call(
        functools.partial(_mix_in_kernel, rc=rc, sizes=sizes, prompt=prompt, n_prev=n_prev),
        grid=(n_b // n_seq, seq // lt),
        in_specs=[tok(d), _mod_spec(layer, 3, n_seq, d), _mod_spec(layer, 4, n_seq, d),
                  pl.BlockSpec((None, None, 1, d), lambda b, i: (layer, 1, 0, 0)),
                  _weight(win),
                  pl.BlockSpec((None, 2, 1, diff_qk), lambda b, i: (layer, 0, 0, 0)),
                  full(gmat),
                  pos_tab(rope[0]), pos_tab(rope[1]), pos_tab(rope[2]), pos_tab(rope[3]),
                  full(ret_tabs[0]), full(ret_tabs[1]), full(ret_tabs[2]), full(ret_tabs[3]),
                  s0_spec] + prev_specs,
        out_specs=att_specs + [tok(ret_v)],
        out_shape=att_shapes + [jax.ShapeDtypeStruct((n_b, seq, ret_v), BF16)],
        scratch_shapes=[pltpu.VMEM(state_blk, F32)],
        compiler_params=_cparams(2),
        name="mix_in",
    )(x, mods, mods, g_norm, win, gqk, gmat, *rope, *ret_tabs, s0, *(prev_kv or ()))


def _lambda(wl_ref, lam_init):
    wl = wl_ref[...]
    a = jnp.sum(wl[0:1] * wl[1:2], axis=-1, keepdims=True)
    b = jnp.sum(wl[2:3] * wl[3:4], axis=-1, keepdims=True)
    return jnp.exp(a) - jnp.exp(b) + lam_init


def _softmax_step_t(s_t, v_t, m_ref, acc_ref):
    m_prev = m_ref[...]
    m_new = jnp.maximum(m_prev, jnp.max(s_t, axis=0, keepdims=True))
    p = jnp.exp2(s_t - m_new)
    acc_ref[...] = jnp.exp2(m_prev - m_new) * acc_ref[...] + _dot(v_t, p.astype(BF16))
    m_ref[...] = m_new


def _attn_prompt_kernel(wl_ref, gsub_ref, qt_ref, k_ref, vt_ref, o_ref, *stats, lam_init, extra=()):
    hw, seq = qt_ref.shape
    n_kt, _, tk = vt_ref.shape
    tq = tk
    gw = min(tq, ATTN_GROUP)
    n_g = tq // gw
    feat = lax.broadcasted_iota(jnp.int32, (hw, 1), 0)
    lam = _lambda(wl_ref, lam_init)
    k_chunk = lax.broadcasted_iota(jnp.int32, (gw, gw), 0) // CHUNK
    q_chunk = lax.broadcasted_iota(jnp.int32, (gw, gw), 1) // CHUNK
    diag_mask = jnp.where(k_chunk <= q_chunk, 0.0, -jnp.inf).astype(F32)

    hv = o_ref.shape[-1]

    def refs(sub, g):
        return stats[2 * (sub * n_g + g):2 * (sub * n_g + g) + 2]

    def ratio(sub, g):
        acc = refs(sub, g)[1][...]
        return acc[:hv] / acc[hv:hv + 1]

    units = [(i, j, sub, g, tk if j < i else min(tk, (g + 1) * gw))
             for i in range(seq // tq) for j in range(i + 1) for sub in range(2) for g in range(n_g)]

    def scores(unit):
        i, j, sub, g, n_keys = unit
        q = qt_ref[:, i * tq + g * gw:i * tq + (g + 1) * gw]
        q = jnp.where(feat < hw // 2 if sub == 0 else feat >= hw // 2, q, jnp.zeros_like(q))
        s_t = _dot(k_ref[j * tk:j * tk + n_keys, :], q)
        if j == i:
            tail = s_t[n_keys - gw:] + diag_mask
            s_t = tail if n_keys == gw else jnp.concatenate([s_t[:n_keys - gw], tail], axis=0)
        return s_t

    extra_at = {e * len(units) // max(len(extra), 1): fn for e, fn in enumerate(extra)}
    assert len(extra_at) == len(extra)
    pending = [scores(u) for u in units[:ATTN_LOOKAHEAD]]
    for n, (i, j, sub, g, n_keys) in enumerate(units):
        if n in extra_at:
            extra_at[n]()
        if n + ATTN_LOOKAHEAD < len(units):
            pending.append(scores(units[n + ATTN_LOOKAHEAD]))
        m_ref, acc_ref = refs(sub, g)
        if j == 0:
            m_ref[...] = jnp.full(m_ref.shape, -jnp.inf, F32)
            acc_ref[...] = jnp.zeros(acc_ref.shape, F32)
        _softmax_step_t(pending.pop(0), vt_ref[j, :, :n_keys], m_ref, acc_ref)
        if j == i and sub == 1 and g == n_g - 1:
            o_t = jnp.concatenate([ratio(0, c) - lam * ratio(1, c) for c in range(n_g)], axis=1)
            o_t = (o_t * lax.rsqrt(jnp.mean(o_t * o_t, axis=0, keepdims=True) + EPS)
                   * gsub_ref[...] * (1.0 - lam_init))
            o_ref[i * tq:(i + 1) * tq, :] = jnp.transpose(o_t).astype(BF16)


def _attn_prompt(qt, k, vt, wl, gsub_col, lam_init, layer):
    n_b, width, seq = qt.shape
    n_kt, v_rows, tk = vt.shape[1:]
    hw, hva = width // DIFF_HEADS, v_rows // DIFF_HEADS
    hv = hva - ONES_ROWS
    diff_v = hv * DIFF_HEADS
    gw = min(tk, ATTN_GROUP)
    assert seq == n_kt * tk and tk % CHUNK == 0 and tk % gw == 0
    return pl.pallas_call(
        functools.partial(_attn_prompt_kernel, lam_init=lam_init),
        grid=(n_b, DIFF_HEADS),
        in_specs=[pl.BlockSpec((None,) + wl.shape[1:], lambda b, h: (layer, 0, 0)),
                  pl.BlockSpec((None,) + gsub_col.shape[1:], lambda b, h: (layer, 0, 0)),
                  pl.BlockSpec((None, hw, seq), lambda b, h: (b, h, 0)),
                  pl.BlockSpec((None, seq, hw), lambda b, h: (b, 0, h)),
                  pl.BlockSpec((None, n_kt, hva, tk), lambda b, h: (b, 0, h, 0))],
        out_specs=pl.BlockSpec((None, seq, hv), lambda b, h: (b, 0, h)),
        out_shape=jax.ShapeDtypeStruct((n_b, seq, diff_v), BF16),
        scratch_shapes=[pltpu.VMEM((1, gw), F32), pltpu.VMEM((hva, gw), F32)] * (2 * (tk // gw)),
        compiler_params=_cparams(2),
        name="attn_prompt",
    )(wl, gsub_col, qt, k, vt)


def _stack_subheads(q):
    w = q.shape[-1]
    lane = lax.broadcasted_iota(jnp.int32, (1, w), 1)
    zero = jnp.zeros_like(q)
    return jnp.concatenate([jnp.where(lane < w // 2, q, zero), jnp.where(lane >= w // 2, q, zero)], axis=0)


def _softmax_step(s, v, m_ref, l_ref, acc_ref):
    m_prev = m_ref[...]
    m_new = jnp.maximum(m_prev, jnp.max(s, axis=-1, keepdims=True))
    alpha = jnp.exp2(m_prev - m_new)
    p = jnp.exp2(s - m_new)
    l_ref[...] = alpha * l_ref[...] + jnp.sum(p, axis=-1, keepdims=True)
    acc_ref[...] = alpha * acc_ref[...] + _dot(p.astype(BF16), v)
    m_ref[...] = m_new


class _SampleAttn:
    def __init__(self, wl_ref, gsub_ref, q_ref, kct_ref, vc_ref, kn_ref, vn_ref, o_ref, lam_init):
        self.refs = (wl_ref, gsub_ref, q_ref, kct_ref, vc_ref, kn_ref, vn_ref, o_ref)
        self.lam_init = lam_init
        self.hw = q_ref.shape[-1] // DIFF_HEADS
        self.hv = vc_ref.shape[-1]
        self.tk = kct_ref.shape[-1]

    def head_q(self, h):
        return _stack_subheads(self.refs[2][:, h * self.hw:(h + 1) * self.hw])

    def cache_scores(self, h):
        return _dot(self.head_q(h), self.refs[3][h * self.hw:(h + 1) * self.hw, :].astype(BF16))

    def cache_v(self, h):
        return self.refs[4][pl.ds(h, self.tk, stride=DIFF_HEADS), :].astype(BF16)

    def new_scores(self, h):
        return _dot_nt(self.head_q(h), self.refs[5][:, h * self.hw:(h + 1) * self.hw])

    def new_v(self, h):
        return self.refs[6][:, h * self.hv:(h + 1) * self.hv]

    def finish(self, h, acc, l):
        rows = acc.shape[0] // 2
        o = acc[:rows] / l[:rows] - _lambda(self.refs[0], self.lam_init) * (acc[rows:] / l[rows:])
        self.refs[7][:, h * self.hv:(h + 1) * self.hv] = (
            _rms(o) * self.refs[1][...] * (1.0 - self.lam_init)).astype(BF16)

    def whole_head(self, h, s_cache=None):
        s_cache = self.cache_scores(h) if s_cache is None else s_cache
        s_new = self.new_scores(h)
        m = jnp.maximum(jnp.max(s_cache, axis=-1, keepdims=True), jnp.max(s_new, axis=-1, keepdims=True))
        p_c = jnp.exp2(s_cache - m)
        p_n = jnp.exp2(s_new - m)
        l = jnp.sum(p_c, axis=-1, keepdims=True) + jnp.sum(p_n, axis=-1, keepdims=True)
        acc = _dot(p_c.astype(BF16), self.cache_v(h)) + _dot(p_n.astype(BF16), self.new_v(h))
        self.finish(h, acc, l)


def _attn_sample_kernel(wl_ref, gsub_ref, q_ref, kct_ref, vc_ref, kn_ref, vn_ref, o_ref,
                        m_ref, l_ref, acc_ref, *, lam_init, single_tile):
    j = pl.program_id(1)
    att = _SampleAttn(wl_ref, gsub_ref, q_ref, kct_ref, vc_ref, kn_ref, vn_ref, o_ref, lam_init)
    scores = [att.cache_scores(h) for h in range(DIFF_HEADS)]

    if single_tile:
        for h in range(DIFF_HEADS):
            att.whole_head(h, scores[h])
        return

    @pl.when(j == 0)
    def _():
        m_ref[...] = jnp.full(m_ref.shape, -jnp.inf, F32)
        l_ref[...] = jnp.zeros(l_ref.shape, F32)
        acc_ref[...] = jnp.zeros(acc_ref.shape, F32)

    for h in range(DIFF_HEADS):
        _softmax_step(scores[h], att.cache_v(h), m_ref.at[h], l_ref.at[h], acc_ref.at[h])

    @pl.when(j == pl.num_programs(1) - 1)
    def _():
        for h in range(DIFF_HEADS):
            _softmax_step(att.new_scores(h), att.new_v(h), m_ref.at[h], l_ref.at[h], acc_ref.at[h])
            att.finish(h, acc_ref[h], l_ref[h])


def _attn_sample(q, k_new, v_new, cache_kt, cache_vi, wl, gsub_row, lam_init, layer):
    n_b, seq, width = q.shape
    past = cache_kt.shape[-1]
    hv = cache_vi.shape[-1]
    vw = v_new.shape[-1]
    assert past % CHUNK == 0 and seq <= CHUNK
    tk = min(past, CACHE_TK)
    assert past % tk == 0
    return pl.pallas_call(
        functools.partial(_attn_sample_kernel, lam_init=lam_init, single_tile=past == tk),
        grid=(n_b, past // tk),
        in_specs=[pl.BlockSpec((None,) + wl.shape[1:], lambda b, j: (layer, 0, 0)),
                  pl.BlockSpec((None,) + gsub_row.shape[1:], lambda b, j: (layer, 0, 0)),
                  pl.BlockSpec((None, seq, width), lambda b, j: (b, 0, 0)),
                  pl.BlockSpec((None, None, width, tk), lambda b, j: (layer, b, 0, j)),
                  pl.BlockSpec((None, None, tk * DIFF_HEADS, hv), lambda b, j: (layer, b, j, 0)),
                  pl.BlockSpec((None, seq, width), lambda b, j: (b, 0, 0)),
                  pl.BlockSpec((None, seq, vw), lambda b, j: (b, 0, 0))],
        out_specs=pl.BlockSpec((None, seq, vw), lambda b, j: (b, 0, 0)),
        out_shape=jax.ShapeDtypeStruct((n_b, seq, vw), BF16),
        scratch_shapes=[pltpu.VMEM((DIFF_HEADS, 2 * seq, 1), F32), pltpu.VMEM((DIFF_HEADS, 2 * seq, 1), F32),
                        pltpu.VMEM((DIFF_HEADS, 2 * seq, hv), F32)],
        compiler_params=_cparams(2),
        name="attn_sample",
    )(wl, gsub_row, q, cache_kt, cache_vi, k_new, v_new)


def _attn_both_kernel(wl_ref, gcol_ref, qt_ref, k_ref, vt_ref, grow_ref, q_ref, kct_ref, vc_ref, kn_ref, vn_ref,
                      op_ref, os_ref, *stats, lam_init):
    att = _SampleAttn(wl_ref, grow_ref, q_ref, kct_ref, vc_ref, kn_ref, vn_ref, os_ref, lam_init)
    s_cache = {}
    pieces = []
    for h in range(DIFF_HEADS):
        pieces.append(lambda h=h: s_cache.__setitem__(h, att.cache_scores(h)))
        if h > 0:
            pieces.append(lambda h=h: att.whole_head(h - 1, s_cache.pop(h - 1)))
    pieces.append(lambda: att.whole_head(DIFF_HEADS - 1, s_cache.pop(DIFF_HEADS - 1)))
    _attn_prompt_kernel(wl_ref, gcol_ref, qt_ref, k_ref, vt_ref, op_ref, *stats, lam_init=lam_init, extra=pieces)


def _attn_both(qt, k, vt, q_s, k_new, v_new, cache_kt, cache_vi, wl, gsub_col, gsub_row, lam_init, layer):
    n_b, width, seq = qt.shape
    n_kt, v_rows, tk = vt.shape[1:]
    n_s, seq_s, _ = q_s.shape
    past = cache_kt.shape[-1]
    hw, hva = width // DIFF_HEADS, v_rows // DIFF_HEADS
    hv = hva - ONES_ROWS
    diff_v = hv * DIFF_HEADS
    gw = min(tk, ATTN_GROUP)
    assert seq == n_kt * tk and tk % CHUNK == 0 and tk % gw == 0
    assert n_s == n_b * DIFF_HEADS and past % CHUNK == 0 and seq_s <= CHUNK

    def sample(blk):
        return pl.BlockSpec((None,) + blk, lambda b, h: (b * DIFF_HEADS + h, 0, 0))

    return pl.pallas_call(
        functools.partial(_attn_both_kernel, lam_init=lam_init),
        grid=(n_b, DIFF_HEADS),
        in_specs=[pl.BlockSpec((None,) + wl.shape[1:], lambda b, h: (layer, 0, 0)),
                  pl.BlockSpec((None,) + gsub_col.shape[1:], lambda b, h: (layer, 0, 0)),
                  pl.BlockSpec((None, hw, seq), lambda b, h: (b, h, 0)),
                  pl.BlockSpec((None, seq, hw), lambda b, h: (b, 0, h)),
                  pl.BlockSpec((None, n_kt, hva, tk), lambda b, h: (b, 0, h, 0)),
                  pl.BlockSpec((None,) + gsub_row.shape[1:], lambda b, h: (layer, 0, 0)),
                  sample((seq_s, width)),
                  pl.BlockSpec((None, None, width, past), lambda b, h: (layer, b * DIFF_HEADS + h, 0, 0)),
                  pl.BlockSpec((None, None, past * DIFF_HEADS, hv), lambda b, h: (layer, b * DIFF_HEADS + h, 0, 0)),
                  sample((seq_s, width)), sample((seq_s, diff_v))],
        out_specs=[pl.BlockSpec((None, seq, hv), lambda b, h: (b, 0, h)), sample((seq_s, diff_v))],
        out_shape=[jax.ShapeDtypeStruct((n_b, seq, diff_v), BF16),
                   jax.ShapeDtypeStruct((n_s, seq_s, diff_v), BF16)],
        scratch_shapes=[pltpu.VMEM((1, gw), F32), pltpu.VMEM((hva, gw), F32)] * (2 * (tk // gw)),
        compiler_params=_cparams(2),
        name="attn_both",
    )(wl, gsub_col, qt, k, vt, gsub_row, q_s, cache_kt, cache_vi, k_new, v_new)


def _mix_out_kernel(x_ref, or_ref, od_ref, gate1_ref, shift_ref, scale_ref, gate2_ref, g_ref,
                    wout_ref, wup_ref, wdn_ref, *rest):
    o_ref = _run_riders(rest)
    n_seq, lt, d = x_ref.shape
    m = n_seq * lt
    rw = or_ref.shape[-1]
    mix = (_dot(or_ref[...].reshape(m, rw), wout_ref[:rw, :])
           + _dot(od_ref[...].reshape(m, od_ref.shape[-1]), wout_ref[rw:, :]))
    x3 = x_ref[...] + gate1_ref[...] * mix.reshape(n_seq, lt, d)
    o_ref[...] = _ffn_apply(x3, g_ref[...], shift_ref[...], scale_ref[...], gate2_ref[...], wup_ref, wdn_ref)


def _mix_out(x, o_r, o_d, mods, g_norm, wout, wup, wdn, layer, cast_srcs=()):
    n_b, seq, d = x.shape
    n_seq, lt = _token_tiles(n_b, seq, OUT_TILE)
    grid = (n_b // n_seq, seq // lt)

    def tok(width):
        return pl.BlockSpec((n_seq, lt, width), lambda b, i: (b, i, 0))

    r_in, r_ops, r_out, r_shapes = _cast_riders(cast_srcs, *grid)
    return pl.pallas_call(
        _mix_out_kernel,
        grid=grid,
        in_specs=[tok(d), tok(o_r.shape[-1]), tok(o_d.shape[-1]),
                  _mod_spec(layer, 5, n_seq, d), _mod_spec(layer, 6, n_seq, d),
                  _mod_spec(layer, 7, n_seq, d), _mod_spec(layer, 8, n_seq, d),
                  pl.BlockSpec((None, None, 1, d), lambda b, i: (layer, 2, 0, 0)),
                  _weight(wout), _weight(wup), _weight(wdn)] + r_in,
        out_specs=[tok(d)] + r_out,
        out_shape=[jax.ShapeDtypeStruct(x.shape, F32)] + r_shapes,
        compiler_params=_cparams(2),
        name="mix_out",
    )(x, o_r, o_d, mods, mods, mods, mods, g_norm, wout, wup, wdn, *r_ops)


def kernel(x_prompt, x_sample, cache_k, cache_v, state_ret, c_prompt, c_sample, w_ada, b_ada, g_norm,
           w_ff_up, w_ff_down, w_in, w_out, g_qk, w_lambda, g_sub):
    depth = w_in.shape[0]
    n_p, seq_p, d = x_prompt.shape
    n_s, seq_s, _ = x_sample.shape
    past = cache_k.shape[2]
    dk_h, dv_h = state_ret.shape[-2], state_ret.shape[-1]
    sub = cache_k.shape[-1]
    hv = cache_v.shape[-1]
    ret_qk, ret_v = RET_HEADS * dk_h, RET_HEADS * dv_h
    diff_qk, diff_v = DIFF_HEADS * 2 * sub, DIFF_HEADS * hv
    sizes = (ret_qk, ret_v, diff_qk, diff_v)

    mods = _ada_mods(jnp.concatenate([c_prompt, c_sample], axis=0), w_ada, b_ada)
    mods = mods[:, :, :, None, :]
    mods_p, mods_s = mods[:, :, :n_p], mods[:, :, n_p:]

    rope_p = _rope_tables(np.arange(seq_p), dk_h, sub)
    rope_s = _rope_tables(past + np.arange(seq_s), dk_h, sub)
    rc_p, rc_s = min(RET_CHUNK, seq_p), min(RET_CHUNK, seq_s)
    tabs_p = _retention_tables(rc_p, dk_h, dv_h)
    tabs_s = _retention_tables(rc_s, dk_h, dv_h)
    group = np.arange(diff_qk) // sub
    gmat = jnp.asarray(group[:, None] == group[None, :], BF16)
    s0_p = jnp.zeros((n_p, RET_HEADS, dk_h, dv_h), F32)
    cache_kt = jnp.transpose(cache_k, (0, 1, 3, 4, 5, 2)).reshape(depth, n_s, diff_qk, past)
    cache_vi = cache_v.reshape(depth, n_s, past * DIFF_HEADS, hv)

    def first_ffn(l):
        return [(w_ff_up, (l, 0)), (w_ff_down, (l, 0))]

    def rest_of_layer(l):
        return [(w_ff_up, (l, 1)), (w_ff_down, (l, 1)), (w_in, (l,)), (w_out, (l,))]

    w_names = ("up0", "dn0", "up1", "dn1", "win", "wout")
    weights = {0: {"up0": w_ff_up[0, 0].astype(BF16), "dn0": w_ff_down[0, 0].astype(BF16)}}
    g_norm4 = g_norm[:, :, None, :]
    gqk = jnp.tile(g_qk, (1, 1, diff_qk // sub))[:, :, None, :]
    wl = w_lambda.astype(F32)
    gsub_row = g_sub[:, None, :]
    gsub_col = g_sub[:, :, None]

    yp, ys = x_prompt, x_sample
    kv_p, kv_s = None, None
    for l in range(depth):
        lam_init = 0.8 - 0.6 * math.exp(-0.3 * l)

        w = weights[l]
        yp, *casts = _ffn1(yp, mods_p, g_norm4, w["up0"], w["dn0"], l,
                           rest_of_layer(l) if len(w) < len(w_names) else ())
        w.update(zip(w_names[2:], casts))
        ys, = _ffn1(ys, mods_s, g_norm4, w["up0"], w["dn0"], l)
        *kv_p, q_t, k_p, v_t, or_p = _mix_in(yp, mods_p, g_norm4, w["win"], gqk, gmat, rope_p, tabs_p,
                                              s0_p, None, sizes, rc_p, l, True, kv_p)
        *kv_s, q_a, k_a, v_a, or_s = _mix_in(ys, mods_s, g_norm4, w["win"], gqk, gmat, rope_s, tabs_s,
                                              state_ret, l, sizes, rc_s, l, False, kv_s)
        if n_s == n_p * DIFF_HEADS and past <= CACHE_TK:
            od_p, od_s = _attn_both(q_t, k_p, v_t, q_a, k_a, v_a, cache_kt, cache_vi, wl, gsub_col, gsub_row,
                                    lam_init, l)
        else:
            od_p = _attn_prompt(q_t, k_p, v_t, wl, gsub_col, lam_init, l)
            od_s = _attn_sample(q_a, k_a, v_a, cache_kt, cache_vi, wl, gsub_row, lam_init, l)
        nxt = first_ffn(l + 1) + rest_of_layer(l + 1) if l + 1 < depth else ()
        yp, *casts = _mix_out(yp, or_p, od_p, mods_p, g_norm4, w["wout"], w["up1"], w["dn1"], l, nxt)
        if nxt:
            weights[l + 1] = dict(zip(w_names, casts))
        ys, = _mix_out(ys, or_s, od_s, mods_s, g_norm4, w["wout"], w["up1"], w["dn1"], l)

    k_prompt = jnp.transpose(kv_p[0].reshape(depth, n_p, DIFF_HEADS, 2, sub, seq_p), (0, 1, 5, 2, 3, 4))
    v_prompt = kv_p[1].reshape(depth, n_p, seq_p, DIFF_HEADS, hv)
    k_sample = kv_s[0].reshape(depth, n_s, seq_s, DIFF_HEADS, 2, sub)
    v_sample = kv_s[1].reshape(depth, n_s, seq_s, DIFF_HEADS, hv)
    return (yp, ys, k_prompt, v_prompt, kv_p[2], k_sample, v_sample, kv_s[2])
```

```python
import functools
import math

import numpy as np
import jax
import jax.numpy as jnp
from jax import lax
from jax.experimental import pallas as pl
from jax.experimental.pallas import tpu as pltpu

F32 = jnp.float32
BF16 = jnp.bfloat16

CHUNK = 64
RET_HEADS = 4
DIFF_HEADS = 4
ROPE_THETA = 10000.0
EPS = 1e-6
LOG2E = math.log2(math.e)
MIX_TILE = 512
FFN_TILE = 1024
OUT_TILE = 512
FF_CHUNK = 256
RET_CHUNK = 256
ATTN_GROUP = 256
ONES_ROWS = 16
ATTN_LOOKAHEAD = 3
CACHE_TK = 4096
VMEM_LIMIT = 56 * 1024 * 1024


def _cparams(n_axes):
    return pltpu.CompilerParams(dimension_semantics=("arbitrary",) * n_axes,
                                vmem_limit_bytes=VMEM_LIMIT)


def _resident(block_shape, index_map):
    return pl.BlockSpec(block_shape, index_map, pipeline_mode=pl.Buffered(1))


def _rms(x):
    return x * lax.rsqrt(jnp.mean(x * x, axis=-1, keepdims=True) + EPS)


def _dot(a, b):
    return jnp.dot(a, b, preferred_element_type=F32)


def _dot_nt(a, b):
    return lax.dot_general(a, b, (((1,), (1,)), ((), ())), preferred_element_type=F32)


def _split_bf16(x):
    hi = x.astype(BF16)
    return hi, (x - hi.astype(F32)).astype(BF16)


def _ada_kernel(c_ref, w_ref, b_ref, o_ref):
    n = c_ref.shape[0]
    c = c_ref[...]
    a_hi, a_lo = _split_bf16(c * jax.nn.sigmoid(c))
    w_hi, w_lo = _split_bf16(w_ref[...])
    a_both = jnp.concatenate([a_hi.astype(F32), a_lo.astype(F32)], axis=0).astype(BF16)
    first = _dot(a_both, w_hi)
    o_ref[...] = first[:n] + first[n:] + _dot(a_hi, w_lo) + b_ref[...]


def _ada_mods(c_all, w_ada, b_ada):
    depth, d, n_out = w_ada.shape
    n_seq = c_all.shape[0]
    n_blk = n_out // d
    return pl.pallas_call(
        _ada_kernel,
        grid=(depth, n_blk),
        in_specs=[
            pl.BlockSpec((n_seq, d), lambda l, j: (0, 0)),
            pl.BlockSpec((None, d, d), lambda l, j: (l, 0, j)),
            pl.BlockSpec((None, None, 1, d), lambda l, j: (l, j, 0, 0)),
        ],
        out_specs=pl.BlockSpec((None, None, n_seq, d), lambda l, j: (l, j, 0, 0)),
        out_shape=jax.ShapeDtypeStruct((depth, n_blk, n_seq, d), F32),
        compiler_params=_cparams(2),
        name="ada_mods",
    )(c_all, w_ada, b_ada.reshape(depth, n_blk, 1, d))


def _pre(x3, g, shift, scale):
    return _rms(x3) * g * (1.0 + scale) + shift


def _ffn_apply(x3, g, shift, scale, gate, wup_ref, wdn_ref):
    n_seq, lt, d = x3.shape
    d_ff = wdn_ref.shape[0]
    hb = _pre(x3, g, shift, scale).reshape(n_seq * lt, d).astype(BF16)
    acc = jnp.zeros((n_seq * lt, d), F32)
    for c in range(d_ff // FF_CHUNK):
        lo = c * FF_CHUNK
        a = _dot(hb, wup_ref[:, lo:lo + FF_CHUNK])
        b = _dot(hb, wup_ref[:, d_ff + lo:d_ff + lo + FF_CHUNK])
        act = (a * jax.nn.sigmoid(a) * b).astype(BF16)
        acc = acc + _dot(act, wdn_ref[lo:lo + FF_CHUNK, :])
    return x3 + (0.5 * gate) * acc.reshape(n_seq, lt, d)


def _cast_riders(srcs, n_outer, n_inner):
    n_steps = n_outer * n_inner
    in_specs, operands, out_specs, out_shapes = [], [], [], []
    for w, lead in srcs:
        rows, cols = w.shape[-2:]
        n_blk = n_steps
        while rows % n_blk or (rows // n_blk) % 16:
            n_blk //= 2
        rep, blk = n_steps // n_blk, rows // n_blk
        in_specs.append(pl.BlockSpec((None,) * len(lead) + (blk, cols),
                                     lambda b, i, lead=lead, rep=rep: lead + ((b * n_inner + i) // rep, 0)))
        out_specs.append(pl.BlockSpec((blk, cols), lambda b, i, rep=rep: ((b * n_inner + i) // rep, 0)))
        out_shapes.append(jax.ShapeDtypeStruct((rows, cols), BF16))
        operands.append(w)
    return in_specs, operands, out_specs, out_shapes


def _run_riders(rest):
    n_r = (len(rest) - 1) // 2
    for src, dst in zip(rest[:n_r], rest[n_r + 1:]):
        dst[...] = src[...].astype(BF16)
    return rest[n_r]


def _ffn_kernel(x_ref, shift_ref, scale_ref, gate_ref, g_ref, wup_ref, wdn_ref, *rest):
    o_ref = _run_riders(rest)
    o_ref[...] = _ffn_apply(x_ref[...], g_ref[...], shift_ref[...], scale_ref[...], gate_ref[...],
                            wup_ref, wdn_ref)


def _token_tiles(n_b, seq, tile):
    lt = min(seq, tile)
    n_seq = max(1, min(n_b, tile // lt))
    assert seq % lt == 0 and n_b % n_seq == 0 and lt % 8 == 0
    return n_seq, lt


def _mod_spec(layer, idx, n_seq, d):
    return pl.BlockSpec((None, None, n_seq, 1, d), lambda b, i: (layer, idx, b, 0, 0))


def _weight(w):
    return _resident(w.shape, lambda b, i: (0, 0))


def _ffn1(x, mods, g_norm, wup, wdn, layer, cast_srcs=()):
    n_b, seq, d = x.shape
    n_seq, lt = _token_tiles(n_b, seq, FFN_TILE)
    grid = (n_b // n_seq, seq // lt)
    x_spec = pl.BlockSpec((n_seq, lt, d), lambda b, i: (b, i, 0))
    r_in, r_ops, r_out, r_shapes = _cast_riders(cast_srcs, *grid)
    return pl.pallas_call(
        _ffn_kernel,
        grid=grid,
        in_specs=[x_spec, _mod_spec(layer, 0, n_seq, d), _mod_spec(layer, 1, n_seq, d),
                  _mod_spec(layer, 2, n_seq, d),
                  pl.BlockSpec((None, None, 1, d), lambda b, i: (layer, 0, 0, 0)),
                  _weight(wup), _weight(wdn)] + r_in,
        out_specs=[x_spec] + r_out,
        out_shape=[jax.ShapeDtypeStruct(x.shape, F32)] + r_shapes,
        compiler_params=_cparams(2),
        name="ffn1",
    )(x, mods, mods, mods, g_norm, wup, wdn, *r_ops)


def _swap_pairs(x):
    w = x.shape[-1]
    lane = lax.broadcasted_iota(jnp.int32, (1, w), 1)
    return jnp.where((lane & 1) == 0, pltpu.roll(x, w - 1, 1), pltpu.roll(x, 1, 1))


def _swap_halves(x, half):
    w = x.shape[-1]
    lane = lax.broadcasted_iota(jnp.int32, (1, w), 1)
    return jnp.where((lane & (2 * half - 1)) < half, pltpu.roll(x, w - half, 1), pltpu.roll(x, half, 1))


def _mix_in_kernel(x_ref, shift_ref, scale_ref, g_ref, win_ref, gqk_ref, gmat_ref,
                   cosr_ref, sinr_ref, cosd_ref, sind_ref, dmask_ref, qdec_ref, kdec_ref, cdec_ref, s0_ref,
                   *rest, rc, sizes, prompt, n_prev):
    if n_prev:
        dkp_ref, dvp_ref, sp_ref = rest[:3]
        rest = rest[3:]
    dk_ref, dv_ref, snew_ref, qa_ref, ka_ref, va_ref, or_ref, s_scr = rest
    @pl.when(pl.program_id(1) == 0)
    def _():
        s_scr[...] = s0_ref[...]

    if n_prev:
        dk_ref[:n_prev] = dkp_ref[...]
        dv_ref[:n_prev] = dvp_ref[...]
        snew_ref[:n_prev] = sp_ref[...]
    ret_qk, ret_v, diff_qk, diff_v = sizes
    n_seq, lt, d = x_ref.shape
    m = n_seq * lt
    dk_h = ret_qk // RET_HEADS
    dv_h = ret_v // RET_HEADS
    hv = diff_v // DIFF_HEADS
    hb = _pre(x_ref[...], g_ref[...], shift_ref[...], scale_ref[...]).reshape(m, d).astype(BF16)

    widths = (ret_qk, ret_qk, ret_v, ret_v, diff_qk, diff_qk, diff_v)

    def proj(n):
        lo = sum(widths[:n])
        return _dot(hb, win_ref[:, lo:lo + widths[n]])

    def rope(x, partner, cos_ref, sin_ref):
        w = x.shape[-1]
        y = x.reshape(n_seq, lt, w) * cos_ref[...] + partner.reshape(n_seq, lt, w) * sin_ref[...]
        return y.reshape(m, w)

    rq, rk = proj(0), proj(1)
    dq, dk = proj(4), proj(5)
    dv = proj(6)

    rq = rope(rq, _swap_pairs(rq), cosr_ref, sinr_ref)
    rk = rope(rk, _swap_pairs(rk), cosr_ref, sinr_ref) * (dk_h ** -0.5)
    chunks = [(s, c, s * lt + c * rc) for s in range(n_seq) for c in range(lt // rc)]
    kd_t = {(s, c): jnp.transpose(rk[r0:r0 + rc] * kdec_ref[...]).astype(BF16)
            for s, c, r0 in chunks}

    sub = diff_qk // (2 * DIFF_HEADS)
    qk = jnp.concatenate([dq, dk], axis=0)
    gmean = _dot((qk * qk).astype(BF16), gmat_ref[...]) * (1.0 / sub)
    rv, rg = proj(2), proj(3)
    qk = qk * lax.rsqrt(gmean + EPS)
    dq = qk[:m] * gqk_ref[0]
    dk = qk[m:] * gqk_ref[1]
    dq = rope(dq, _swap_halves(dq, sub // 2), cosd_ref, sind_ref) * (sub ** -0.5 * LOG2E)
    dk = rope(dk, _swap_halves(dk, sub // 2), cosd_ref, sind_ref)

    for s in range(n_seq):
        rows = slice(s * lt, (s + 1) * lt)
        for h in range(DIFF_HEADS):
            dv_ref[n_prev, s, pl.ds(h, lt, stride=DIFF_HEADS), :] = dv[rows, h * hv:(h + 1) * hv]
        if prompt:
            dk_t = jnp.transpose(dk[rows])
            dk_ref[n_prev, s] = dk_t
            ka_ref[s] = dk[rows].astype(BF16)
            qa_ref[s] = jnp.transpose(dq[rows]).astype(BF16)
            v_t = jnp.transpose(dv[rows]).astype(BF16)
            ones = jnp.ones((ONES_ROWS, lt), BF16)
            va_ref[s] = jnp.concatenate(
                [part for h in range(DIFF_HEADS) for part in (v_t[h * hv:(h + 1) * hv], ones)], axis=0)
        else:
            for j in range(diff_qk // sub):
                dk_ref[n_prev, s, :, j, :] = dk[rows, j * sub:(j + 1) * sub]
            ka_ref[s] = dk[rows].astype(BF16)
            qa_ref[s] = dq[rows].astype(BF16)
            va_ref[s] = dv[rows].astype(BF16)

    lane_q =lax.broadcasted_iota(jnp.int32, (1, ret_qk), 1)
    gate = rg * jax.nn.sigmoid(rg)
    rv_b = rv.astype(BF16)
    zero_blk = jnp.zeros((dk_h, dv_h), BF16)
    def head_cols(h, width):
        return slice(h * width, (h + 1) * width)

    att, upd = {}, {}
    for s, c, r0 in chunks:
        q = rq[r0:r0 + rc]
        q_heads = jnp.concatenate(
            [jnp.where((lane_q >= h * dk_h) & (lane_q < (h + 1) * dk_h), q, 0.0) for h in range(RET_HEADS)],
            axis=0).astype(BF16)
        att[s, c] = (_dot_nt(q_heads, rk[r0:r0 + rc].astype(BF16)) * dmask_ref[...]).astype(BF16)
    for s, c, r0 in chunks:
        upd[s, c] = [_dot(kd_t[s, c][head_cols(h, dk_h)], rv_b[r0:r0 + rc, head_cols(h, dv_h)])
                     for h in range(RET_HEADS)]
    for s in range(n_seq):
        state = [s_scr[s, h] for h in range(RET_HEADS)]
        for c in range(lt // rc):
            r0 = s * lt + c * rc
            s_diag = jnp.concatenate(
                [jnp.concatenate([state[h].astype(BF16) if g == h else zero_blk for g in range(RET_HEADS)], axis=1)
                 for h in range(RET_HEADS)], axis=0)
            inter = _dot((rq[r0:r0 + rc] * qdec_ref[...]).astype(BF16), s_diag)
            for h in range(RET_HEADS):
                cols = head_cols(h, dv_h)
                o = _dot(att[s, c][h * rc:(h + 1) * rc], rv_b[r0:r0 + rc, cols]) + inter[:, cols]
                or_ref[s, c * rc:(c + 1) * rc, cols] = (_rms(o) * gate[r0:r0 + rc, cols]).astype(BF16)
                state[h] = cdec_ref[h] * state[h] + upd[s, c][h]
        for h in range(RET_HEADS):
            s_scr[s, h] = state[h]
            snew_ref[n_prev, s, h] = state[h]


def _retention_tables(rc, dk_h, dv_h):
    n_h = RET_HEADS
    log_g = np.log1p(-np.exp2(-5.0 - np.arange(n_h, dtype=np.float64)))
    idx = np.arange(rc, dtype=np.float64)
    rel = idx[:, None] - idx[None, :]
    dmask = np.where(rel >= 0, np.exp(log_g[:, None, None] * np.maximum(rel, 0.0)), 0.0)
    q_dec = np.exp(log_g[None, :] * (idx[:, None] + 1.0))
    k_dec = np.exp(log_g[None, :] * (rc - 1.0 - idx[:, None]))
    c_dec = np.exp(log_g * rc)
    tabs = (dmask.reshape(n_h * rc, rc), np.repeat(q_dec, dk_h, axis=1), np.repeat(k_dec, dk_h, axis=1),
            np.broadcast_to(c_dec[:, None, None], (n_h, dk_h, dv_h)))
    return tuple(jnp.asarray(t, F32) for t in tabs)


def _rope_tables(pos, dk_h, sub):
    posf = np.asarray(pos, np.float64)[:, None]
    ret_freq = 1.0 / (ROPE_THETA ** np.linspace(0.0, 1.0, dk_h // 2))
    ang = posf * ret_freq[None, :]
    sign = np.tile(np.array([-1.0, 1.0]), dk_h // 2)
    cos_r = np.tile(np.repeat(np.cos(ang), 2, axis=1), (1, RET_HEADS))
    sin_r = np.tile(np.repeat(np.sin(ang), 2, axis=1) * sign, (1, RET_HEADS))
    rope_freq = 1.0 / (ROPE_THETA ** (np.arange(0, sub, 2, dtype=np.float64) / sub))
    ang = posf * rope_freq[None, :]
    sign = np.concatenate([-np.ones(sub // 2), np.ones(sub // 2)])
    cos_d = np.tile(np.tile(np.cos(ang), (1, 2)), (1, 2 * DIFF_HEADS))
    sin_d = np.tile(np.tile(np.sin(ang), (1, 2)) * sign, (1, 2 * DIFF_HEADS))
    return tuple(jnp.asarray(t, F32) for t in (cos_r, sin_r, cos_d, sin_d))


def _mix_in(x, mods, g_norm, win, gqk, gmat, rope, ret_tabs, s0, s0_layer, sizes, rc, layer, prompt, prev_kv):
    n_b, seq, d = x.shape
    n_prev = 0 if prev_kv is None else prev_kv[0].shape[0]
    ret_qk, ret_v, diff_qk, diff_v = sizes
    n_seq, lt = _token_tiles(n_b, seq, MIX_TILE)
    assert lt % rc == 0
    dk_h, dv_h = ret_qk // RET_HEADS, ret_v // RET_HEADS
    hv = diff_v // DIFF_HEADS

    def tok(width):
        return pl.BlockSpec((n_seq, lt, width), lambda b, i: (b, i, 0))

    def feat(width):
        return pl.BlockSpec((n_seq, width, lt), lambda b, i: (b, 0, i))

    def full(a):
        return _resident(a.shape, lambda b, i, _n=a.ndim: (0,) * _n)

    def pos_tab(a):
        return pl.BlockSpec((lt, a.shape[1]), lambda b, i: (i, 0))

    state_blk = (n_seq, RET_HEADS, dk_h, dv_h)
    if s0_layer is None:
        s0_spec = pl.BlockSpec(state_blk, lambda b, i: (b, 0, 0, 0))
    else:
        s0_spec = pl.BlockSpec((None,) + state_blk, lambda b, i: (s0_layer, b, 0, 0, 0))
    dv_blk, dv_idx = (n_seq, lt * DIFF_HEADS, hv), (lambda b, i: (b, i, 0))
    dv_full = (n_b, seq * DIFF_HEADS, hv)
    if prompt:
        dk_blk, dk_idx, dk_full = (n_seq, diff_qk, lt), (lambda b, i: (b, 0, i)), (n_b, diff_qk, seq)
    else:
        sub = diff_qk // (2 * DIFF_HEADS)
        dk_blk, dk_idx = (n_seq, lt, diff_qk // sub, sub), (lambda b, i: (b, i, 0, 0))
        dk_full = (n_b, seq, diff_qk // sub, sub)

    def stacked_spec(n, blk, idx):
        return pl.BlockSpec((n,) + blk, lambda b, i: (0,) + idx(b, i))

    st_idx = lambda b, i: (b, 0, 0, 0)
    kv_specs = [stacked_spec(n_prev + 1, dk_blk, dk_idx), stacked_spec(n_prev + 1, dv_blk, dv_idx),
                stacked_spec(n_prev + 1, state_blk, st_idx)]
    kv_shapes = [jax.ShapeDtypeStruct((n_prev + 1,) + dk_full, F32),
                 jax.ShapeDtypeStruct((n_prev + 1,) + dv_full, F32),
                 jax.ShapeDtypeStruct((n_prev + 1, n_b) + state_blk[1:], F32)]
    prev_specs = [stacked_spec(n_prev, dk_blk, dk_idx), stacked_spec(n_prev, dv_blk, dv_idx),
                  stacked_spec(n_prev, state_blk, st_idx)] if n_prev else []
    if prompt:
        v_rows = diff_v + DIFF_HEADS * ONES_ROWS
        att_specs = kv_specs + [feat(diff_qk), tok(diff_qk),
                                pl.BlockSpec((n_seq, None, v_rows, lt), lambda b, i: (b, i, 0, 0))]
        att_shapes = kv_shapes + [jax.ShapeDtypeStruct((n_b, diff_qk, seq), BF16),
                                  jax.ShapeDtypeStruct((n_b, seq, diff_qk), BF16),
                                  jax.ShapeDtypeStruct((n_b, seq // lt, v_rows, lt), BF16)]
    else:
        att_specs = kv_specs + [tok(diff_qk), tok(diff_qk), tok(diff_v)]
        att_shapes = kv_shapes + [jax.ShapeDtypeStruct((n_b, seq, diff_qk), BF16),
                                  jax.ShapeDtypeStruct((n_b, seq, diff_qk), BF16),
                                  jax.ShapeDtypeStruct((n_b, seq, diff_v), BF16)]
    return pl.pallas_call(
        functools.partial(_mix_in_kernel, rc=rc, sizes=sizes, prompt=prompt, n_prev=n_prev),
        grid=(n_b // n_seq, seq // lt),
        in_specs=[tok(d), _mod_spec(layer, 3, n_seq, d), _mod_spec(layer, 4, n_seq, d),
                  pl.BlockSpec((None, None, 1, d), lambda b, i: (layer, 1, 0, 0)),
                  _weight(win),
                  pl.BlockSpec((None, 2, 1, diff_qk), lambda b, i: (layer, 0, 0, 0)),
                  full(gmat),
                  pos_tab(rope[0]), pos_tab(rope[1]), pos_tab(rope[2]), pos_tab(rope[3]),
                  full(ret_tabs[0]), full(ret_tabs[1]), full(ret_tabs[2]), full(ret_tabs[3]),
                  s0_spec] + prev_specs,
        out_specs=att_specs + [tok(ret_v)],
        out_shape=att_shapes + [jax.ShapeDtypeStruct((n_b, seq, ret_v), BF16)],
        scratch_shapes=[pltpu.VMEM(state_blk, F32)],
        compiler_params=_cparams(2),
        name="mix_in",
    )(x, mods, mods, g_norm, win, gqk, gmat, *rope, *ret_tabs, s0, *(prev_kv or ()))


def _lambda(wl_ref, lam_init):
    wl = wl_ref[...]
    a = jnp.sum(wl[0:1] * wl[1:2], axis=-1, keepdims=True)
    b = jnp.sum(wl[2:3] * wl[3:4], axis=-1, keepdims=True)
    return jnp.exp(a) - jnp.exp(b) + lam_init


def _softmax_step_t(s_t, v_t, m_ref, acc_ref):
    m_prev = m_ref[...]
    m_new = jnp.maximum(m_prev, jnp.max(s_t, axis=0, keepdims=True))
    p = jnp.exp2(s_t - m_new)
    acc_ref[...] = jnp.exp2(m_prev - m_new) * acc_ref[...] + _dot(v_t, p.astype(BF16))
    m_ref[...] = m_new


def _attn_prompt_kernel(wl_ref, gsub_ref, qt_ref, k_ref, vt_ref, o_ref, *stats, lam_init):
    hw, seq = qt_ref.shape
    n_kt, _, tk = vt_ref.shape
    tq = tk
    gw = min(tq, ATTN_GROUP)
    n_g = tq // gw
    feat = lax.broadcasted_iota(jnp.int32, (hw, 1), 0)
    lam = _lambda(wl_ref, lam_init)
    k_chunk = lax.broadcasted_iota(jnp.int32, (gw, gw), 0) // CHUNK
    q_chunk = lax.broadcasted_iota(jnp.int32, (gw, gw), 1) // CHUNK
    diag_mask = jnp.where(k_chunk <= q_chunk, 0.0, -jnp.inf).astype(F32)

    hv = o_ref.shape[-1]

    def refs(sub, g):
        return stats[2 * (sub * n_g + g):2 * (sub * n_g + g) + 2]

    def ratio(sub, g):
        acc = refs(sub, g)[1][...]
        return acc[:hv] / acc[hv:hv + 1]

    units = [(i, j, sub, g, tk if j < i else min(tk, (g + 1) * gw))
             for i in range(seq // tq) for j in range(i + 1) for sub in range(2) for g in range(n_g)]

    def scores(unit):
        i, j, sub, g, n_keys = unit
        q = qt_ref[:, i * tq + g * gw:i * tq + (g + 1) * gw]
        q = jnp.where(feat < hw // 2 if sub == 0 else feat >= hw // 2, q, jnp.zeros_like(q))
        s_t = _dot(k_ref[j * tk:j * tk + n_keys, :], q)
        if j == i:
            tail = s_t[n_keys - gw:] + diag_mask
            s_t = tail if n_keys == gw else jnp.concatenate([s_t[:n_keys - gw], tail], axis=0)
        return s_t

    pending = [scores(u) for u in units[:ATTN_LOOKAHEAD]]
    for n, (i, j, sub, g, n_keys) in enumerate(units):
        if n + ATTN_LOOKAHEAD < len(units):
            pending.append(scores(units[n + ATTN_LOOKAHEAD]))
        m_ref, acc_ref = refs(sub, g)
        if j == 0:
            m_ref[...] = jnp.full(m_ref.shape, -jnp.inf, F32)
            acc_ref[...] = jnp.zeros(acc_ref.shape, F32)
        _softmax_step_t(pending.pop(0), vt_ref[j, :, :n_keys], m_ref, acc_ref)
        if j == i and sub == 1 and g == n_g - 1:
            o_t = jnp.concatenate([ratio(0, c) - lam * ratio(1, c) for c in range(n_g)], axis=1)
            o_t = (o_t * lax.rsqrt(jnp.mean(o_t * o_t, axis=0, keepdims=True) + EPS)
                   * gsub_ref[...] * (1.0 - lam_init))
            o_ref[i * tq:(i + 1) * tq, :] = jnp.transpose(o_t).astype(BF16)


def _attn_prompt(qt, k, vt, wl, gsub_col, lam_init, layer):
    n_b, width, seq = qt.shape
    n_kt, v_rows, tk = vt.shape[1:]
    hw, hva = width // DIFF_HEADS, v_rows // DIFF_HEADS
    hv = hva - ONES_ROWS
    diff_v = hv * DIFF_HEADS
    gw = min(tk, ATTN_GROUP)
    assert seq == n_kt * tk and tk % CHUNK == 0 and tk % gw == 0
    return pl.pallas_call(
        functools.partial(_attn_prompt_kernel, lam_init=lam_init),
        grid=(n_b, DIFF_HEADS),
        in_specs=[pl.BlockSpec((None,) + wl.shape[1:], lambda b, h: (layer, 0, 0)),
                  pl.BlockSpec((None,) + gsub_col.shape[1:], lambda b, h: (layer, 0, 0)),
                  pl.BlockSpec((None, hw, seq), lambda b, h: (b, h, 0)),
                  pl.BlockSpec((None, seq, hw), lambda b, h: (b, 0, h)),
                  pl.BlockSpec((None, n_kt, hva, tk), lambda b, h: (b, 0, h, 0))],
        out_specs=pl.BlockSpec((None, seq, hv), lambda b, h: (b, 0, h)),
        out_shape=jax.ShapeDtypeStruct((n_b, seq, diff_v), BF16),
        scratch_shapes=[pltpu.VMEM((1, gw), F32), pltpu.VMEM((hva, gw), F32)] * (2 * (tk // gw)),
        compiler_params=_cparams(2),
        name="attn_prompt",
    )(wl, gsub_col, qt, k, vt)


def _stack_subheads(q):
    w = q.shape[-1]
    lane = lax.broadcasted_iota(jnp.int32, (1, w), 1)
    zero = jnp.zeros_like(q)
    return jnp.concatenate([jnp.where(lane < w // 2, q, zero), jnp.where(lane >= w // 2, q, zero)], axis=0)


def _softmax_step(s, v, m_ref, l_ref, acc_ref):
    m_prev = m_ref[...]
    m_new = jnp.maximum(m_prev, jnp.max(s, axis=-1, keepdims=True))
    alpha = jnp.exp2(m_prev - m_new)
    p = jnp.exp2(s - m_new)
    l_ref[...] = alpha * l_ref[...] + jnp.sum(p, axis=-1, keepdims=True)
    acc_ref[...] = alpha * acc_ref[...] + _dot(p.astype(BF16), v)
    m_ref[...] = m_new


class _SampleAttn:
    def __init__(self, wl_ref, gsub_ref, q_ref, kct_ref, vc_ref, kn_ref, vn_ref, o_ref, lam_init):
        self.refs = (wl_ref, gsub_ref, q_ref, kct_ref, vc_ref, kn_ref, vn_ref, o_ref)
        self.lam_init = lam_init
        self.hw = q_ref.shape[-1] // DIFF_HEADS
        self.hv = vc_ref.shape[-1]
        self.tk = kct_ref.shape[-1]

    def head_q(self, h):
        return _stack_subheads(self.refs[2][:, h * self.hw:(h + 1) * self.hw])

    def cache_scores(self, h):
        return _dot(self.head_q(h), self.refs[3][h * self.hw:(h + 1) * self.hw, :].astype(BF16))

    def cache_v(self, h):
        return self.refs[4][pl.ds(h, self.tk, stride=DIFF_HEADS), :].astype(BF16)

    def new_scores(self, h):
        return _dot_nt(self.head_q(h), self.refs[5][:, h * self.hw:(h + 1) * self.hw])

    def new_v(self, h):
        return self.refs[6][:, h * self.hv:(h + 1) * self.hv]

    def finish(self, h, acc, l):
        rows = acc.shape[0] // 2
        o = acc[:rows] / l[:rows] - _lambda(self.refs[0], self.lam_init) * (acc[rows:] / l[rows:])
        self.refs[7][:, h * self.hv:(h + 1) * self.hv] = (
            _rms(o) * self.refs[1][...] * (1.0 - self.lam_init)).astype(BF16)

    def whole_head(self, h, s_cache=None):
        s_cache = self.cache_scores(h) if s_cache is None else s_cache
        s_new = self.new_scores(h)
        m = jnp.maximum(jnp.max(s_cache, axis=-1, keepdims=True), jnp.max(s_new, axis=-1, keepdims=True))
        p_c = jnp.exp2(s_cache - m)
        p_n = jnp.exp2(s_new - m)
        l = jnp.sum(p_c, axis=-1, keepdims=True) + jnp.sum(p_n, axis=-1, keepdims=True)
        acc = _dot(p_c.astype(BF16), self.cache_v(h)) + _dot(p_n.astype(BF16), self.new_v(h))
        self.finish(h, acc, l)


def _attn_sample_kernel(wl_ref, gsub_ref, q_ref, kct_ref, vc_ref, kn_ref, vn_ref, o_ref,
                        m_ref, l_ref, acc_ref, *, lam_init, single_tile):
    j = pl.program_id(1)
    att = _SampleAttn(wl_ref, gsub_ref, q_ref, kct_ref, vc_ref, kn_ref, vn_ref, o_ref, lam_init)
    scores = [att.cache_scores(h) for h in range(DIFF_HEADS)]

    if single_tile:
        for h in range(DIFF_HEADS):
            att.whole_head(h, scores[h])
        return

    @pl.when(j == 0)
    def _():
        m_ref[...] = jnp.full(m_ref.shape, -jnp.inf, F32)
        l_ref[...] = jnp.zeros(l_ref.shape, F32)
        acc_ref[...] = jnp.zeros(acc_ref.shape, F32)

    for h in range(DIFF_HEADS):
        _softmax_step(scores[h], att.cache_v(h), m_ref.at[h], l_ref.at[h], acc_ref.at[h])

    @pl.when(j == pl.num_programs(1) - 1)
    def _():
        for h in range(DIFF_HEADS):
            _softmax_step(att.new_scores(h), att.new_v(h), m_ref.at[h], l_ref.at[h], acc_ref.at[h])
            att.finish(h, acc_ref[h], l_ref[h])


def _attn_sample(q, k_new, v_new, cache_kt, cache_vi, wl, gsub_row, lam_init, layer):
    n_b, seq, width = q.shape
    past = cache_kt.shape[-1]
    hv = cache_vi.shape[-1]
    vw = v_new.shape[-1]
    assert past % CHUNK == 0 and seq <= CHUNK
    tk = min(past, CACHE_TK)
    assert past % tk == 0
    return pl.pallas_call(
        functools.partial(_attn_sample_kernel, lam_init=lam_init, single_tile=past == tk),
        grid=(n_b, past // tk),
        in_specs=[pl.BlockSpec((None,) + wl.shape[1:], lambda b, j: (layer, 0, 0)),
                  pl.BlockSpec((None,) + gsub_row.shape[1:], lambda b, j: (layer, 0, 0)),
                  pl.BlockSpec((None, seq, width), lambda b, j: (b, 0, 0)),
                  pl.BlockSpec((None, None, width, tk), lambda b, j: (layer, b, 0, j)),
                  pl.BlockSpec((None, None, tk * DIFF_HEADS, hv), lambda b, j: (layer, b, j, 0)),
                  pl.BlockSpec((None, seq, width), lambda b, j: (b, 0, 0)),
                  pl.BlockSpec((None, seq, vw), lambda b, j: (b, 0, 0))],
        out_specs=pl.BlockSpec((None, seq, vw), lambda b, j: (b, 0, 0)),
        out_shape=jax.ShapeDtypeStruct((n_b, seq, vw), BF16),
        scratch_shapes=[pltpu.VMEM((DIFF_HEADS, 2 * seq, 1), F32), pltpu.VMEM((DIFF_HEADS, 2 * seq, 1), F32),
                        pltpu.VMEM((DIFF_HEADS, 2 * seq, hv), F32)],
        compiler_params=_cparams(2),
        name="attn_sample",
    )(wl, gsub_row, q, cache_kt, cache_vi, k_new, v_new)


def _attn_both_kernel(wl_ref, gcol_ref, qt_ref, k_ref, vt_ref, grow_ref, q_ref, kct_ref, vc_ref, kn_ref, vn_ref,
                      op_ref, os_ref, *stats, lam_init):
    att = _SampleAttn(wl_ref, grow_ref, q_ref, kct_ref, vc_ref, kn_ref, vn_ref, os_ref, lam_init)
    s_cache = att.cache_scores(0)
    for h in range(DIFF_HEADS):
        s_next = att.cache_scores(h + 1) if h + 1 < DIFF_HEADS else None
        att.whole_head(h, s_cache)
        s_cache = s_next
    _attn_prompt_kernel(wl_ref, gcol_ref, qt_ref, k_ref, vt_ref, op_ref, *stats, lam_init=lam_init)


def _attn_both(qt, k, vt, q_s, k_new, v_new, cache_kt, cache_vi, wl, gsub_col, gsub_row, lam_init, layer):
    n_b, width, seq = qt.shape
    n_kt, v_rows, tk = vt.shape[1:]
    n_s, seq_s, _ = q_s.shape
    past = cache_kt.shape[-1]
    hw, hva = width // DIFF_HEADS, v_rows // DIFF_HEADS
    hv = hva - ONES_ROWS
    diff_v = hv * DIFF_HEADS
    gw = min(tk, ATTN_GROUP)
    assert seq == n_kt * tk and tk % CHUNK == 0 and tk % gw == 0
    assert n_s == n_b * DIFF_HEADS and past % CHUNK == 0 and seq_s <= CHUNK

    def sample(blk):
        return pl.BlockSpec((None,) + blk, lambda b, h: (b * DIFF_HEADS + h, 0, 0))

    return pl.pallas_call(
        functools.partial(_attn_both_kernel, lam_init=lam_init),
        grid=(n_b, DIFF_HEADS),
        in_specs=[pl.BlockSpec((None,) + wl.shape[1:], lambda b, h: (layer, 0, 0)),
                  pl.BlockSpec((None,) + gsub_col.shape[1:], lambda b, h: (layer, 0, 0)),
                  pl.BlockSpec((None, hw, seq), lambda b, h: (b, h, 0)),
                  pl.BlockSpec((None, seq, hw), lambda b, h: (b, 0, h)),
                  pl.BlockSpec((None, n_kt, hva, tk), lambda b, h: (b, 0, h, 0)),
                  pl.BlockSpec((None,) + gsub_row.shape[1:], lambda b, h: (layer, 0, 0)),
                  sample((seq_s, width)),
                  pl.BlockSpec((None, None, width, past), lambda b, h: (layer, b * DIFF_HEADS + h, 0, 0)),
                  pl.BlockSpec((None, None, past * DIFF_HEADS, hv), lambda b, h: (layer, b * DIFF_HEADS + h, 0, 0)),
                  sample((seq_s, width)), sample((seq_s, diff_v))],
        out_specs=[pl.BlockSpec((None, seq, hv), lambda b, h: (b, 0, h)), sample((seq_s, diff_v))],
        out_shape=[jax.ShapeDtypeStruct((n_b, seq, diff_v), BF16),
                   jax.ShapeDtypeStruct((n_s, seq_s, diff_v), BF16)],
        scratch_shapes=[pltpu.VMEM((1, gw), F32), pltpu.VMEM((hva, gw), F32)] * (2 * (tk // gw)),
        compiler_params=_cparams(2),
        name="attn_both",
    )(wl, gsub_col, qt, k, vt, gsub_row, q_s, cache_kt, cache_vi, k_new, v_new)


def _mix_out_kernel(x_ref, or_ref, od_ref, gate1_ref, shift_ref, scale_ref, gate2_ref, g_ref,
                    wout_ref, wup_ref, wdn_ref, *rest):
    o_ref = _run_riders(rest)
    n_seq, lt, d = x_ref.shape
    m = n_seq * lt
    rw = or_ref.shape[-1]
    mix = (_dot(or_ref[...].reshape(m, rw), wout_ref[:rw, :])
           + _dot(od_ref[...].reshape(m, od_ref.shape[-1]), wout_ref[rw:, :]))
    x3 = x_ref[...] + gate1_ref[...] * mix.reshape(n_seq, lt, d)
    o_ref[...] = _ffn_apply(x3, g_ref[...], shift_ref[...], scale_ref[...], gate2_ref[...], wup_ref, wdn_ref)


def _mix_out(x, o_r, o_d, mods, g_norm, wout, wup, wdn, layer, cast_srcs=()):
    n_b, seq, d = x.shape
    n_seq, lt = _token_tiles(n_b, seq, OUT_TILE)
    grid = (n_b // n_seq, seq // lt)

    def tok(width):
        return pl.BlockSpec((n_seq, lt, width), lambda b, i: (b, i, 0))

    r_in, r_ops, r_out, r_shapes = _cast_riders(cast_srcs, *grid)
    return pl.pallas_call(
        _mix_out_kernel,
        grid=grid,
        in_specs=[tok(d), tok(o_r.shape[-1]), tok(o_d.shape[-1]),
                  _mod_spec(layer, 5, n_seq, d), _mod_spec(layer, 6, n_seq, d),
                  _mod_spec(layer, 7, n_seq, d), _mod_spec(layer, 8, n_seq, d),
                  pl.BlockSpec((None, None, 1, d), lambda b, i: (layer, 2, 0, 0)),
                  _weight(wout), _weight(wup), _weight(wdn)] + r_in,
        out_specs=[tok(d)] + r_out,
        out_shape=[jax.ShapeDtypeStruct(x.shape, F32)] + r_shapes,
        compiler_params=_cparams(2),
        name="mix_out",
    )(x, o_r, o_d, mods, mods, mods, mods, g_norm, wout, wup, wdn, *r_ops)


def kernel(x_prompt, x_sample, cache_k, cache_v, state_ret, c_prompt, c_sample, w_ada, b_ada, g_norm,
           w_ff_up, w_ff_down, w_in, w_out, g_qk, w_lambda, g_sub):
    depth = w_in.shape[0]
    n_p, seq_p, d = x_prompt.shape
    n_s, seq_s, _ = x_sample.shape
    past = cache_k.shape[2]
    dk_h, dv_h = state_ret.shape[-2], state_ret.shape[-1]
    sub = cache_k.shape[-1]
    hv = cache_v.shape[-1]
    ret_qk, ret_v = RET_HEADS * dk_h, RET_HEADS * dv_h
    diff_qk, diff_v = DIFF_HEADS * 2 * sub, DIFF_HEADS * hv
    sizes = (ret_qk, ret_v, diff_qk, diff_v)

    mods = _ada_mods(jnp.concatenate([c_prompt, c_sample], axis=0), w_ada, b_ada)
    mods = mods[:, :, :, None, :]
    mods_p, mods_s = mods[:, :, :n_p], mods[:, :, n_p:]

    rope_p = _rope_tables(np.arange(seq_p), dk_h, sub)
    rope_s = _rope_tables(past + np.arange(seq_s), dk_h, sub)
    rc_p, rc_s = min(RET_CHUNK, seq_p), min(RET_CHUNK, seq_s)
    tabs_p = _retention_tables(rc_p, dk_h, dv_h)
    tabs_s = _retention_tables(rc_s, dk_h, dv_h)
    group = np.arange(diff_qk) // sub
    gmat = jnp.asarray(group[:, None] == group[None, :], BF16)
    s0_p = jnp.zeros((n_p, RET_HEADS, dk_h, dv_h), F32)
    cache_kt = jnp.transpose(cache_k, (0, 1, 3, 4, 5, 2)).reshape(depth, n_s, diff_qk, past)
    cache_vi = cache_v.reshape(depth, n_s, past * DIFF_HEADS, hv)

    def first_ffn(l):
        return [(w_ff_up, (l, 0)), (w_ff_down, (l, 0))]

    def rest_of_layer(l):
        return [(w_ff_up, (l, 1)), (w_ff_down, (l, 1)), (w_in, (l,)), (w_out, (l,))]

    w_names = ("up0", "dn0", "up1", "dn1", "win", "wout")
    weights = {0: {"up0": w_ff_up[0, 0].astype(BF16), "dn0": w_ff_down[0, 0].astype(BF16)}}
    g_norm4 = g_norm[:, :, None, :]
    gqk = jnp.tile(g_qk, (1, 1, diff_qk // sub))[:, :, None, :]
    wl = w_lambda.astype(F32)
    gsub_row = g_sub[:, None, :]
    gsub_col = g_sub[:, :, None]

    yp, ys = x_prompt, x_sample
    kv_p, kv_s = None, None
    for l in range(depth):
        lam_init = 0.8 - 0.6 * math.exp(-0.3 * l)

        w = weights[l]
        yp, *casts = _ffn1(yp, mods_p, g_norm4, w["up0"], w["dn0"], l,
                           rest_of_layer(l) if len(w) < len(w_names) else ())
        w.update(zip(w_names[2:], casts))
        ys, = _ffn1(ys, mods_s, g_norm4, w["up0"], w["dn0"], l)
        *kv_p, q_t, k_p, v_t, or_p = _mix_in(yp, mods_p, g_norm4, w["win"], gqk, gmat, rope_p, tabs_p,
                                              s0_p, None, sizes, rc_p, l, True, kv_p)
        *kv_s, q_a, k_a, v_a, or_s = _mix_in(ys, mods_s, g_norm4, w["win"], gqk, gmat, rope_s, tabs_s,
                                              state_ret, l, sizes, rc_s, l, False, kv_s)
        if n_s == n_p * DIFF_HEADS and past <= CACHE_TK:
            od_p, od_s = _attn_both(q_t, k_p, v_t, q_a, k_a, v_a, cache_kt, cache_vi, wl, gsub_col, gsub_row,
                                    lam_init, l)
        else:
            od_p = _attn_prompt(q_t, k_p, v_t, wl, gsub_col, lam_init, l)
            od_s = _attn_sample(q_a, k_a, v_a, cache_kt, cache_vi, wl, gsub_row, lam_init, l)
        nxt = first_ffn(l + 1) + rest_of_layer(l + 1) if l + 1 < depth else ()
        yp, *casts = _mix_out(yp, or_p, od_p, mods_p, g_norm4, w["wout"], w["up1"], w["dn1"], l, nxt)
        if nxt:
            weights[l + 1] = dict(zip(w_names, casts))
        ys, = _mix_out(ys, or_s, od_s, mods_s, g_norm4, w["wout"], w["up1"], w["dn1"], l)

    k_prompt = jnp.transpose(kv_p[0].reshape(depth, n_p, DIFF_HEADS, 2, sub, seq_p), (0, 1, 5, 2, 3, 4))
    v_prompt = kv_p[1].reshape(depth, n_p, seq_p, DIFF_HEADS, hv)
    k_sample = kv_s[0].reshape(depth, n_s, seq_s, DIFF_HEADS, 2, sub)
    v_sample = kv_s[1].reshape(depth, n_s, seq_s, DIFF_HEADS, hv)
    return (yp, ys, k_prompt, v_prompt, kv_p[2], k_sample, v_sample, kv_s[2])
```

```python
import functools
import math

import numpy as np
import jax
import jax.numpy as jnp
from jax import lax
from jax.experimental import pallas as pl
from jax.experimental.pallas import tpu as pltpu

F32 = jnp.float32
BF16 = jnp.bfloat16

CHUNK = 64
RET_HEADS = 4
DIFF_HEADS = 4
ROPE_THETA = 10000.0
EPS = 1e-6
LOG2E = math.log2(math.e)
MIX_TILE = 512
FFN_TILE = 1024
OUT_TILE = 512
FF_CHUNK = 256
RET_CHUNK = 256
ATTN_GROUP = 256
ONES_ROWS = 16
ATTN_LOOKAHEAD = 3
CACHE_TK = 4096
VMEM_LIMIT = 56 * 1024 * 1024


def _cparams(n_axes):
    return pltpu.CompilerParams(dimension_semantics=("arbitrary",) * n_axes,
                                vmem_limit_bytes=VMEM_LIMIT)


def _resident(block_shape, index_map):
    return pl.BlockSpec(block_shape, index_map, pipeline_mode=pl.Buffered(1))


def _rms(x):
    return x * lax.rsqrt(jnp.mean(x * x, axis=-1, keepdims=True) + EPS)


def _dot(a, b):
    return jnp.dot(a, b, preferred_element_type=F32)


def _dot_nt(a, b):
    return lax.dot_general(a, b, (((1,), (1,)), ((), ())), preferred_element_type=F32)


def _split_bf16(x):
    hi = x.astype(BF16)
    return hi, (x - hi.astype(F32)).astype(BF16)


def _ada_kernel(c_ref, w_ref, b_ref, o_ref):
    n = c_ref.shape[0]
    c = c_ref[...]
    a_hi, a_lo = _split_bf16(c * jax.nn.sigmoid(c))
    w_hi, w_lo = _split_bf16(w_ref[...])
    a_both = jnp.concatenate([a_hi.astype(F32), a_lo.astype(F32)], axis=0).astype(BF16)
    first = _dot(a_both, w_hi)
    o_ref[...] = first[:n] + first[n:] + _dot(a_hi, w_lo) + b_ref[...]


def _ada_mods(c_all, w_ada, b_ada):
    depth, d, n_out = w_ada.shape
    n_seq = c_all.shape[0]
    n_blk = n_out // d
    return pl.pallas_call(
        _ada_kernel,
        grid=(depth, n_blk),
        in_specs=[
            pl.BlockSpec((n_seq, d), lambda l, j: (0, 0)),
            pl.BlockSpec((None, d, d), lambda l, j: (l, 0, j)),
            pl.BlockSpec((None, None, 1, d), lambda l, j: (l, j, 0, 0)),
        ],
        out_specs=pl.BlockSpec((None, None, n_seq, d), lambda l, j: (l, j, 0, 0)),
        out_shape=jax.ShapeDtypeStruct((depth, n_blk, n_seq, d), F32),
        compiler_params=_cparams(2),
        name="ada_mods",
    )(c_all, w_ada, b_ada.reshape(depth, n_blk, 1, d))


def _pre(x3, g, shift, scale):
    return _rms(x3) * g * (1.0 + scale) + shift


def _ffn_apply(x3, g, shift, scale, gate, wup_ref, wdn_ref):
    n_seq, lt, d = x3.shape
    d_ff = wdn_ref.shape[0]
    hb = _pre(x3, g, shift, scale).reshape(n_seq * lt, d).astype(BF16)
    acc = jnp.zeros((n_seq * lt, d), F32)
    for c in range(d_ff // FF_CHUNK):
        lo = c * FF_CHUNK
        a = _dot(hb, wup_ref[:, lo:lo + FF_CHUNK])
        b = _dot(hb, wup_ref[:, d_ff + lo:d_ff + lo + FF_CHUNK])
        act = (a * jax.nn.sigmoid(a) * b).astype(BF16)
        acc = acc + _dot(act, wdn_ref[lo:lo + FF_CHUNK, :])
    return x3 + (0.5 * gate) * acc.reshape(n_seq, lt, d)


def _cast_riders(srcs, n_outer, n_inner):
    n_steps = n_outer * n_inner
    in_specs, operands, out_specs, out_shapes = [], [], [], []
    for w, lead in srcs:
        rows, cols = w.shape[-2:]
        n_blk = n_steps
        while rows % n_blk or (rows // n_blk) % 16:
            n_blk //= 2
        rep, blk = n_steps // n_blk, rows // n_blk
        in_specs.append(pl.BlockSpec((None,) * len(lead) + (blk, cols),
                                     lambda b, i, lead=lead, rep=rep: lead + ((b * n_inner + i) // rep, 0)))
        out_specs.append(pl.BlockSpec((blk, cols), lambda b, i, rep=rep: ((b * n_inner + i) // rep, 0)))
        out_shapes.append(jax.ShapeDtypeStruct((rows, cols), BF16))
        operands.append(w)
    return in_specs, operands, out_specs, out_shapes


def _run_riders(rest):
    n_r = (len(rest) - 1) // 2
    for src, dst in zip(rest[:n_r], rest[n_r + 1:]):
        dst[...] = src[...].astype(BF16)
    return rest[n_r]


def _ffn_kernel(x_ref, shift_ref, scale_ref, gate_ref, g_ref, wup_ref, wdn_ref, *rest):
    o_ref = _run_riders(rest)
    o_ref[...] = _ffn_apply(x_ref[...], g_ref[...], shift_ref[...], scale_ref[...], gate_ref[...],
                            wup_ref, wdn_ref)


def _token_tiles(n_b, seq, tile):
    lt = min(seq, tile)
    n_seq = max(1, min(n_b, tile // lt))
    assert seq % lt == 0 and n_b % n_seq == 0 and lt % 8 == 0
    return n_seq, lt


def _mod_spec(layer, idx, n_seq, d):
    return pl.BlockSpec((None, None, n_seq, 1, d), lambda b, i: (layer, idx, b, 0, 0))


def _weight(w):
    return _resident(w.shape, lambda b, i: (0, 0))


def _ffn1(x, mods, g_norm, wup, wdn, layer, cast_srcs=()):
    n_b, seq, d = x.shape
    n_seq, lt = _token_tiles(n_b, seq, FFN_TILE)
    grid = (n_b // n_seq, seq // lt)
    x_spec = pl.BlockSpec((n_seq, lt, d), lambda b, i: (b, i, 0))
    r_in, r_ops, r_out, r_shapes = _cast_riders(cast_srcs, *grid)
    return pl.pallas_call(
        _ffn_kernel,
        grid=grid,
        in_specs=[x_spec, _mod_spec(layer, 0, n_seq, d), _mod_spec(layer, 1, n_seq, d),
                  _mod_spec(layer, 2, n_seq, d),
                  pl.BlockSpec((None, None, 1, d), lambda b, i: (layer, 0, 0, 0)),
                  _weight(wup), _weight(wdn)] + r_in,
        out_specs=[x_spec] + r_out,
        out_shape=[jax.ShapeDtypeStruct(x.shape, F32)] + r_shapes,
        compiler_params=_cparams(2),
        name="ffn1",
    )(x, mods, mods, mods, g_norm, wup, wdn, *r_ops)


def _swap_pairs(x):
    w = x.shape[-1]
    lane = lax.broadcasted_iota(jnp.int32, (1, w), 1)
    return jnp.where((lane & 1) == 0, pltpu.roll(x, w - 1, 1), pltpu.roll(x, 1, 1))


def _swap_halves(x, half):
    w = x.shape[-1]
    lane = lax.broadcasted_iota(jnp.int32, (1, w), 1)
    return jnp.where((lane & (2 * half - 1)) < half, pltpu.roll(x, w - half, 1), pltpu.roll(x, half, 1))


def _mix_in_kernel(x_ref, shift_ref, scale_ref, g_ref, win_ref, gqk_ref, gmat_ref,
                   cosr_ref, sinr_ref, cosd_ref, sind_ref, dmask_ref, qdec_ref, kdec_ref, cdec_ref, s0_ref,
                   *rest, rc, sizes, prompt, n_prev):
    if n_prev:
        dkp_ref, dvp_ref, sp_ref = rest[:3]
        rest = rest[3:]
    dk_ref, dv_ref, snew_ref, qa_ref, ka_ref, va_ref, or_ref, s_scr = rest
    @pl.when(pl.program_id(1) == 0)
    def _():
        s_scr[...] = s0_ref[...]

    if n_prev:
        dk_ref[:n_prev] = dkp_ref[...]
        dv_ref[:n_prev] = dvp_ref[...]
        snew_ref[:n_prev] = sp_ref[...]
    ret_qk, ret_v, diff_qk, diff_v = sizes
    n_seq, lt, d = x_ref.shape
    m = n_seq * lt
    dk_h = ret_qk // RET_HEADS
    dv_h = ret_v // RET_HEADS
    hv = diff_v // DIFF_HEADS
    hb = _pre(x_ref[...], g_ref[...], shift_ref[...], scale_ref[...]).reshape(m, d).astype(BF16)

    widths = (ret_qk, ret_qk, ret_v, ret_v, diff_qk, diff_qk, diff_v)

    def proj(n):
        lo = sum(widths[:n])
        return _dot(hb, win_ref[:, lo:lo + widths[n]])

    def rope(x, partner, cos_ref, sin_ref):
        w = x.shape[-1]
        y = x.reshape(n_seq, lt, w) * cos_ref[...] + partner.reshape(n_seq, lt, w) * sin_ref[...]
        return y.reshape(m, w)

    rq, rk = proj(0), proj(1)
    dq, dk = proj(4), proj(5)
    dv = proj(6)

    rq = rope(rq, _swap_pairs(rq), cosr_ref, sinr_ref)
    rk = rope(rk, _swap_pairs(rk), cosr_ref, sinr_ref) * (dk_h ** -0.5)
    chunks = [(s, c, s * lt + c * rc) for s in range(n_seq) for c in range(lt // rc)]
    kd_t = {(s, c): jnp.transpose(rk[r0:r0 + rc] * kdec_ref[...]).astype(BF16)
            for s, c, r0 in chunks}

    sub = diff_qk // (2 * DIFF_HEADS)
    qk = jnp.concatenate([dq, dk], axis=0)
    gmean = _dot((qk * qk).astype(BF16), gmat_ref[...]) * (1.0 / sub)
    rv, rg = proj(2), proj(3)
    qk = qk * lax.rsqrt(gmean + EPS)
    dq = qk[:m] * gqk_ref[0]
    dk = qk[m:] * gqk_ref[1]
    dq = rope(dq, _swap_halves(dq, sub // 2), cosd_ref, sind_ref) * (sub ** -0.5 * LOG2E)
    dk = rope(dk, _swap_halves(dk, sub // 2), cosd_ref, sind_ref)

    for s in range(n_seq):
        rows = slice(s * lt, (s + 1) * lt)
        for h in range(DIFF_HEADS):
            dv_ref[n_prev, s, pl.ds(h, lt, stride=DIFF_HEADS), :] = dv[rows, h * hv:(h + 1) * hv]
        if prompt:
            dk_t = jnp.transpose(dk[rows])
            dk_ref[n_prev, s] = dk_t
            ka_ref[s] = dk[rows].astype(BF16)
            qa_ref[s] = jnp.transpose(dq[rows]).astype(BF16)
            v_t = jnp.transpose(dv[rows]).astype(BF16)
            ones = jnp.ones((ONES_ROWS, lt), BF16)
            va_ref[s] = jnp.concatenate(
                [part for h in range(DIFF_HEADS) for part in (v_t[h * hv:(h + 1) * hv], ones)], axis=0)
        else:
            for j in range(diff_qk // sub):
                dk_ref[n_prev, s, :, j, :] = dk[rows, j * sub:(j + 1) * sub]
            ka_ref[s] = dk[rows].astype(BF16)
            qa_ref[s] = dq[rows].astype(BF16)
            va_ref[s] = dv[rows].astype(BF16)

    lane_q = lax.broadcasted_iota(jnp.int32, (1, ret_qk), 1)
    gate = rg * jax.nn.sigmoid(rg)
    rv_b = rv.astype(BF16)
    zero_blk = jnp.zeros((dk_h, dv_h), BF16)

    def head_cols(h, width):
        return slice(h * width, (h + 1) * width)

    att, upd = {}, {}
    for s, c, r0 in chunks:
        q = rq[r0:r0 + rc]
        q_heads = jnp.concatenate(
            [jnp.where((lane_q >= h * dk_h) & (lane_q < (h + 1) * dk_h), q, 0.0) for h in range(RET_HEADS)],
            axis=0).astype(BF16)
        att[s, c] = (_dot_nt(q_heads, rk[r0:r0 + rc].astype(BF16)) * dmask_ref[...]).astype(BF16)
    for s, c, r0 in chunks:
        upd[s, c] = [_dot(kd_t[s, c][head_cols(h, dk_h)], rv_b[r0:r0 + rc, head_cols(h, dv_h)])
                     for h in range(RET_HEADS)]
    for s in range(n_seq):
        state = [s_scr[s, h] for h in range(RET_HEADS)]
        for c in range(lt // rc):
            r0 = s * lt + c * rc
            s_diag = jnp.concatenate(
                [jnp.concatenate([state[h].astype(BF16) if g == h else zero_blk for g in range(RET_HEADS)], axis=1)
                 for h in range(RET_HEADS)], axis=0)
            inter = _dot((rq[r0:r0 + rc] * qdec_ref[...]).astype(BF16), s_diag)
            for h in range(RET_HEADS):
                cols = head_cols(h, dv_h)
                o = _dot(att[s, c][h * rc:(h + 1) * rc], rv_b[r0:r0 + rc, cols]) + inter[:, cols]
                or_ref[s, c * rc:(c + 1) * rc, cols] = (_rms(o) * gate[r0:r0 + rc, cols]).astype(BF16)
                state[h] = cdec_ref[h] * state[h] + upd[s, c][h]
        for h in range(RET_HEADS):
            s_scr[s, h] = state[h]
            snew_ref[n_prev, s, h] = state[h]


def _retention_tables(rc, dk_h, dv_h):
    n_h = RET_HEADS
    log_g = np.log1p(-np.exp2(-5.0 - np.arange(n_h, dtype=np.float64)))
    idx = np.arange(rc, dtype=np.float64)
    rel = idx[:, None] - idx[None, :]
    dmask = np.where(rel >= 0, np.exp(log_g[:, None, None] * np.maximum(rel, 0.0)), 0.0)
    q_dec = np.exp(log_g[None, :] * (idx[:, None] + 1.0))
    k_dec = np.exp(log_g[None, :] * (rc - 1.0 - idx[:, None]))
    c_dec = np.exp(log_g * rc)
    tabs = (dmask.reshape(n_h * rc, rc), np.repeat(q_dec, dk_h, axis=1), np.repeat(k_dec, dk_h, axis=1),
            np.broadcast_to(c_dec[:, None, None], (n_h, dk_h, dv_h)))
    return tuple(jnp.asarray(t, F32) for t in tabs)


def _rope_tables(pos, dk_h, sub):
    posf = np.asarray(pos, np.float64)[:, None]
    ret_freq = 1.0 / (ROPE_THETA ** np.linspace(0.0, 1.0, dk_h // 2))
    ang = posf * ret_freq[None, :]
    sign = np.tile(np.array([-1.0, 1.0]), dk_h // 2)
    cos_r = np.tile(np.repeat(np.cos(ang), 2, axis=1), (1, RET_HEADS))
    sin_r = np.tile(np.repeat(np.sin(ang), 2, axis=1) * sign, (1, RET_HEADS))
    rope_freq = 1.0 / (ROPE_THETA ** (np.arange(0, sub, 2, dtype=np.float64) / sub))
    ang = posf * rope_freq[None, :]
    sign = np.concatenate([-np.ones(sub // 2), np.ones(sub // 2)])
    cos_d = np.tile(np.tile(np.cos(ang), (1, 2)), (1, 2 * DIFF_HEADS))
    sin_d = np.tile(np.tile(np.sin(ang), (1, 2)) * sign, (1, 2 * DIFF_HEADS))
    return tuple(jnp.asarray(t, F32) for t in (cos_r, sin_r, cos_d, sin_d))


def _mix_in(x, mods, g_norm, win, gqk, gmat, rope, ret_tabs, s0, s0_layer, sizes, rc, layer, prompt, prev_kv):
    n_b, seq, d = x.shape
    n_prev = 0 if prev_kv is None else prev_kv[0].shape[0]
    ret_qk, ret_v, diff_qk, diff_v = sizes
    n_seq, lt = _token_tiles(n_b, seq, MIX_TILE)
    assert lt % rc == 0
    dk_h, dv_h = ret_qk // RET_HEADS, ret_v // RET_HEADS
    hv = diff_v // DIFF_HEADS

    def tok(width):
        return pl.BlockSpec((n_seq, lt, width), lambda b, i: (b, i, 0))

    def feat(width):
        return pl.BlockSpec((n_seq, width, lt), lambda b, i: (b, 0, i))

    def full(a):
        return _resident(a.shape, lambda b, i, _n=a.ndim: (0,) * _n)

    def pos_tab(a):
        return pl.BlockSpec((lt, a.shape[1]), lambda b, i: (i, 0))

    state_blk = (n_seq, RET_HEADS, dk_h, dv_h)
    if s0_layer is None:
        s0_spec = pl.BlockSpec(state_blk, lambda b, i: (b, 0, 0, 0))
    else:
        s0_spec = pl.BlockSpec((None,) + state_blk, lambda b, i: (s0_layer, b, 0, 0, 0))
    dv_blk, dv_idx = (n_seq, lt * DIFF_HEADS, hv), (lambda b, i: (b, i, 0))
    dv_full = (n_b, seq * DIFF_HEADS, hv)
    if prompt:
        dk_blk, dk_idx, dk_full = (n_seq, diff_qk, lt), (lambda b, i: (b, 0, i)), (n_b, diff_qk, seq)
    else:
        sub = diff_qk // (2 * DIFF_HEADS)
        dk_blk, dk_idx = (n_seq, lt, diff_qk // sub, sub), (lambda b, i: (b, i, 0, 0))
        dk_full = (n_b, seq, diff_qk // sub, sub)

    def stacked_spec(n, blk, idx):
        return pl.BlockSpec((n,) + blk, lambda b, i: (0,) + idx(b, i))

    st_idx = lambda b, i: (b, 0, 0, 0)
    kv_specs = [stacked_spec(n_prev + 1, dk_blk, dk_idx), stacked_spec(n_prev + 1, dv_blk, dv_idx),
                stacked_spec(n_prev + 1, state_blk, st_idx)]
    kv_shapes = [jax.ShapeDtypeStruct((n_prev + 1,) + dk_full, F32),
                 jax.ShapeDtypeStruct((n_prev + 1,) + dv_full, F32),
                 jax.ShapeDtypeStruct((n_prev + 1, n_b) + state_blk[1:], F32)]
    prev_specs = [stacked_spec(n_prev, dk_blk, dk_idx), stacked_spec(n_prev, dv_blk, dv_idx),
                  stacked_spec(n_prev, state_blk, st_idx)] if n_prev else []
    if prompt:
        v_rows = diff_v + DIFF_HEADS * ONES_ROWS
        att_specs = kv_specs + [feat(diff_qk), tok(diff_qk),
                                pl.BlockSpec((n_seq, None, v_rows, lt), lambda b, i: (b, i, 0, 0))]
        att_shapes = kv_shapes + [jax.ShapeDtypeStruct((n_b, diff_qk, seq), BF16),
                                  jax.ShapeDtypeStruct((n_b, seq, diff_qk), BF16),
                                  jax.ShapeDtypeStruct((n_b, seq // lt, v_rows, lt), BF16)]
    else:
        att_specs = kv_specs + [tok(diff_qk), tok(diff_qk), tok(diff_v)]
        att_shapes = kv_shapes + [jax.ShapeDtypeStruct((n_b, seq, diff_qk), BF16),
                                  jax.ShapeDtypeStruct((n_b, seq, diff_qk), BF16),
                                  jax.ShapeDtypeStruct((n_b, seq, diff_v), BF16)]
    return pl.pallas_call(
        functools.partial(_mix_in_kernel, rc=rc, sizes=sizes, prompt=prompt, n_prev=n_prev),
        grid=(n_b // n_seq, seq // lt),
        in_specs=[tok(d), _mod_spec(layer, 3, n_seq, d), _mod_spec(layer, 4, n_seq, d),
                  pl.BlockSpec((None, None, 1, d), lambda b, i: (layer, 1, 0, 0)),
                  _weight(win),
                  pl.BlockSpec((None, 2, 1, diff_qk), lambda b, i: (layer, 0, 0, 0)),
                  full(gmat),
                  pos_tab(rope[0]), pos_tab(rope[1]), pos_tab(rope[2]), pos_tab(rope[3]),
                  full(ret_tabs[0]), full(ret_tabs[1]), full(ret_tabs[2]), full(ret_tabs[3]),
                  s0_spec] + prev_specs,
        out_specs=att_specs + [tok(ret_v)],
        out_shape=att_shapes + [jax.ShapeDtypeStruct((n_b, seq, ret_v), BF16)],
        scratch_shapes=[pltpu.VMEM(state_blk, F32)],
        compiler_params=_cparams(2),
        name="mix_in",
    )(x, mods, mods, g_norm, win, gqk, gmat, *rope, *ret_tabs, s0, *(prev_kv or ()))


def _lambda(wl_ref, lam_init):
    wl = wl_ref[...]
    a = jnp.sum(wl[0:1] * wl[1:2], axis=-1, keepdims=True)
    b = jnp.sum(wl[2:3] * wl[3:4], axis=-1, keepdims=True)
    return jnp.exp(a) - jnp.exp(b) + lam_init


def _softmax_step_t(s_t, v_t, m_ref, acc_ref):
    m_prev = m_ref[...]
    m_new = jnp.maximum(m_prev, jnp.max(s_t, axis=0, keepdims=True))
    p = jnp.exp2(s_t - m_new)
    acc_ref[...] = jnp.exp2(m_prev - m_new) * acc_ref[...] + _dot(v_t, p.astype(BF16))
    m_ref[...] = m_new


def _attn_prompt_kernel(wl_ref, gsub_ref, qt_ref, k_ref, vt_ref, o_ref, *stats, lam_init):
    hw, seq = qt_ref.shape
    n_kt, _, tk = vt_ref.shape
    tq = tk
    gw = min(tq, ATTN_GROUP)
    n_g = tq // gw
    feat = lax.broadcasted_iota(jnp.int32, (hw, 1), 0)
    lam = _lambda(wl_ref, lam_init)
    k_chunk = lax.broadcasted_iota(jnp.int32, (gw, gw), 0) // CHUNK
    q_chunk = lax.broadcasted_iota(jnp.int32, (gw, gw), 1) // CHUNK
    diag_mask = jnp.where(k_chunk <= q_chunk, 0.0, -jnp.inf).astype(F32)

    hv = o_ref.shape[-1]

    def refs(sub, g):
        return stats[2 * (sub * n_g + g):2 * (sub * n_g + g) + 2]

    def ratio(sub, g):
        acc = refs(sub, g)[1][...]
        return acc[:hv] / acc[hv:hv + 1]

    units = [(i, j, sub, g, tk if j < i else min(tk, (g + 1) * gw))
             for i in range(seq // tq) for j in range(i + 1) for sub in range(2) for g in range(n_g)]

    def scores(unit):
        i, j, sub, g, n_keys = unit
        q = qt_ref[:, i * tq + g * gw:i * tq + (g + 1) * gw]
        q = jnp.where(feat < hw // 2 if sub == 0 else feat >= hw // 2, q, jnp.zeros_like(q))
        s_t = _dot(k_ref[j * tk:j * tk + n_keys, :], q)
        if j == i:
            tail = s_t[n_keys - gw:] + diag_mask
            s_t = tail if n_keys == gw else jnp.concatenate([s_t[:n_keys - gw], tail], axis=0)
        return s_t

    pending = [scores(u) for u in units[:ATTN_LOOKAHEAD]]
    for n, (i, j, sub, g, n_keys) in enumerate(units):
        if n + ATTN_LOOKAHEAD < len(units):
            pending.append(scores(units[n + ATTN_LOOKAHEAD]))
        m_ref, acc_ref = refs(sub, g)
        if j == 0:
            m_ref[...] = jnp.full(m_ref.shape, -jnp.inf, F32)
            acc_ref[...] = jnp.zeros(acc_ref.shape, F32)
        _softmax_step_t(pending.pop(0), vt_ref[j, :, :n_keys], m_ref, acc_ref)
        if j == i and sub == 1 and g == n_g - 1:
            o_t = jnp.concatenate([ratio(0, c) - lam * ratio(1, c) for c in range(n_g)], axis=1)
            o_t = (o_t * lax.rsqrt(jnp.mean(o_t * o_t, axis=0, keepdims=True) + EPS)
                   * gsub_ref[...] * (1.0 - lam_init))
            o_ref[i * tq:(i + 1) * tq, :] = jnp.transpose(o_t).astype(BF16)


def _attn_prompt(qt, k, vt, wl, gsub_col, lam_init, layer):
    n_b, width, seq = qt.shape
    n_kt, v_rows, tk = vt.shape[1:]
    hw, hva = width // DIFF_HEADS, v_rows // DIFF_HEADS
    hv = hva - ONES_ROWS
    diff_v = hv * DIFF_HEADS
    gw = min(tk, ATTN_GROUP)
    assert seq == n_kt * tk and tk % CHUNK == 0 and tk % gw == 0
    return pl.pallas_call(
        functools.partial(_attn_prompt_kernel, lam_init=lam_init),
        grid=(n_b, DIFF_HEADS),
        in_specs=[pl.BlockSpec((None,) + wl.shape[1:], lambda b, h: (layer, 0, 0)),
                  pl.BlockSpec((None,) + gsub_col.shape[1:], lambda b, h: (layer, 0, 0)),
                  pl.BlockSpec((None, hw, seq), lambda b, h: (b, h, 0)),
                  pl.BlockSpec((None, seq, hw), lambda b, h: (b, 0, h)),
                  pl.BlockSpec((None, n_kt, hva, tk), lambda b, h: (b, 0, h, 0))],
        out_specs=pl.BlockSpec((None, seq, hv), lambda b, h: (b, 0, h)),
        out_shape=jax.ShapeDtypeStruct((n_b, seq, diff_v), BF16),
        scratch_shapes=[pltpu.VMEM((1, gw), F32), pltpu.VMEM((hva, gw), F32)] * (2 * (tk // gw)),
        compiler_params=_cparams(2),
        name="attn_prompt",
    )(wl, gsub_col, qt, k, vt)


def _stack_subheads(q):
    w = q.shape[-1]
    lane = lax.broadcasted_iota(jnp.int32, (1, w), 1)
    zero = jnp.zeros_like(q)
    return jnp.concatenate([jnp.where(lane < w // 2, q, zero), jnp.where(lane >= w // 2, q, zero)], axis=0)


def _softmax_step(s, v, m_ref, l_ref, acc_ref):
    m_prev = m_ref[...]
    m_new = jnp.maximum(m_prev, jnp.max(s, axis=-1, keepdims=True))
    alpha = jnp.exp2(m_prev - m_new)
    p = jnp.exp2(s - m_new)
    l_ref[...] = alpha * l_ref[...] + jnp.sum(p, axis=-1, keepdims=True)
    acc_ref[...] = alpha * acc_ref[...] + _dot(p.astype(BF16), v)
    m_ref[...] = m_new


class _SampleAttn:
    def __init__(self, wl_ref, gsub_ref, q_ref, kct_ref, vc_ref, kn_ref, vn_ref, o_ref, lam_init):
        self.refs = (wl_ref, gsub_ref, q_ref, kct_ref, vc_ref, kn_ref, vn_ref, o_ref)
        self.lam_init = lam_init
        self.hw = q_ref.shape[-1] // DIFF_HEADS
        self.hv = vc_ref.shape[-1]
        self.tk = kct_ref.shape[-1]

    def head_q(self, h):
        return _stack_subheads(self.refs[2][:, h * self.hw:(h + 1) * self.hw])

    def cache_scores(self, h):
        return _dot(self.head_q(h), self.refs[3][h * self.hw:(h + 1) * self.hw, :].astype(BF16))

    def cache_v(self, h):
        return self.refs[4][pl.ds(h, self.tk, stride=DIFF_HEADS), :].astype(BF16)

    def new_scores(self, h):
        return _dot_nt(self.head_q(h), self.refs[5][:, h * self.hw:(h + 1) * self.hw])

    def new_v(self, h):
        return self.refs[6][:, h * self.hv:(h + 1) * self.hv]

    def finish(self, h, acc, l):
        rows = acc.shape[0] // 2
        o = acc[:rows] / l[:rows] - _lambda(self.refs[0], self.lam_init) * (acc[rows:] / l[rows:])
        self.refs[7][:, h * self.hv:(h + 1) * self.hv] = (
            _rms(o) * self.refs[1][...] * (1.0 - self.lam_init)).astype(BF16)

    def whole_head(self, h, s_cache=None):
        s_cache = self.cache_scores(h) if s_cache is None else s_cache
        s_new = self.new_scores(h)
        m = jnp.maximum(jnp.max(s_cache, axis=-1, keepdims=True), jnp.max(s_new, axis=-1, keepdims=True))
        p_c = jnp.exp2(s_cache - m)
        p_n = jnp.exp2(s_new - m)
        l = jnp.sum(p_c, axis=-1, keepdims=True) + jnp.sum(p_n, axis=-1, keepdims=True)
        acc = _dot(p_c.astype(BF16), self.cache_v(h)) + _dot(p_n.astype(BF16), self.new_v(h))
        self.finish(h, acc, l)


def _attn_sample_kernel(wl_ref, gsub_ref, q_ref, kct_ref, vc_ref, kn_ref, vn_ref, o_ref,
                        m_ref, l_ref, acc_ref, *, lam_init, single_tile):
    j = pl.program_id(1)
    att = _SampleAttn(wl_ref, gsub_ref, q_ref, kct_ref, vc_ref, kn_ref, vn_ref, o_ref, lam_init)
    scores = [att.cache_scores(h) for h in range(DIFF_HEADS)]

    if single_tile:
        for h in range(DIFF_HEADS):
            att.whole_head(h, scores[h])
        return

    @pl.when(j == 0)
    def _():
        m_ref[...] = jnp.full(m_ref.shape, -jnp.inf, F32)
        l_ref[...] = jnp.zeros(l_ref.shape, F32)
        acc_ref[...] = jnp.zeros(acc_ref.shape, F32)

    for h in range(DIFF_HEADS):
        _softmax_step(scores[h], att.cache_v(h), m_ref.at[h], l_ref.at[h], acc_ref.at[h])

    @pl.when(j == pl.num_programs(1) - 1)
    def _():
        for h in range(DIFF_HEADS):
            _softmax_step(att.new_scores(h), att.new_v(h), m_ref.at[h], l_ref.at[h], acc_ref.at[h])
            att.finish(h, acc_ref[h], l_ref[h])


def _attn_sample(q, k_new, v_new, cache_kt, cache_vi, wl, gsub_row, lam_init, layer):
    n_b, seq, width = q.shape
    past = cache_kt.shape[-1]
    hv = cache_vi.shape[-1]
    vw = v_new.shape[-1]
    assert past % CHUNK == 0 and seq <= CHUNK
    tk = min(past, CACHE_TK)
    assert past % tk == 0
    return pl.pallas_call(
        functools.partial(_attn_sample_kernel, lam_init=lam_init, single_tile=past == tk),
        grid=(n_b, past // tk),
        in_specs=[pl.BlockSpec((None,) + wl.shape[1:], lambda b, j: (layer, 0, 0)),
                  pl.BlockSpec((None,) + gsub_row.shape[1:], lambda b, j: (layer, 0, 0)),
                  pl.BlockSpec((None, seq, width), lambda b, j: (b, 0, 0)),
                  pl.BlockSpec((None, None, width, tk), lambda b, j: (layer, b, 0, j)),
                  pl.BlockSpec((None, None, tk * DIFF_HEADS, hv), lambda b, j: (layer, b, j, 0)),
                  pl.BlockSpec((None, seq, width), lambda b, j: (b, 0, 0)),
                  pl.BlockSpec((None, seq, vw), lambda b, j: (b, 0, 0))],
        out_specs=pl.BlockSpec((None, seq, vw), lambda b, j: (b, 0, 0)),
        out_shape=jax.ShapeDtypeStruct((n_b, seq, vw), BF16),
        scratch_shapes=[pltpu.VMEM((DIFF_HEADS, 2 * seq, 1), F32), pltpu.VMEM((DIFF_HEADS, 2 * seq, 1), F32),
                        pltpu.VMEM((DIFF_HEADS, 2 * seq, hv), F32)],
        compiler_params=_cparams(2),
        name="attn_sample",
    )(wl, gsub_row, q, cache_kt, cache_vi, k_new, v_new)


def _attn_both_kernel(wl_ref, gcol_ref, qt_ref, k_ref, vt_ref, grow_ref, q_ref, kct_ref, vc_ref, kn_ref, vn_ref,
                      op_ref, os_ref, *stats, lam_init):
    att = _SampleAttn(wl_ref, grow_ref, q_ref, kct_ref, vc_ref, kn_ref, vn_ref, os_ref, lam_init)
    s_cache = att.cache_scores(0)
    for h in range(DIFF_HEADS):
        s_next = att.cache_scores(h + 1) if h + 1 < DIFF_HEADS else None
        att.whole_head(h, s_cache)
        s_cache = s_next
    _attn_prompt_kernel(wl_ref, gcol_ref, qt_ref, k_ref, vt_ref, op_ref, *stats, lam_init=lam_init)


def _attn_both(qt, k, vt, q_s, k_new, v_new, cache_kt, cache_vi, wl, gsub_col, gsub_row, lam_init, layer):
    n_b, width, seq = qt.shape
    n_kt, v_rows, tk = vt.shape[1:]
    n_s, seq_s, _ = q_s.shape
    past = cache_kt.shape[-1]
    hw, hva = width // DIFF_HEADS, v_rows // DIFF_HEADS
    hv = hva - ONES_ROWS
    diff_v = hv * DIFF_HEADS
    gw = min(tk, ATTN_GROUP)
    assert seq == n_kt * tk and tk % CHUNK == 0 and tk % gw == 0
    assert n_s == n_b * DIFF_HEADS and past % CHUNK == 0 and seq_s <= CHUNK

    def sample(blk):
        return pl.BlockSpec((None,) + blk, lambda b, h: (b * DIFF_HEADS + h, 0, 0))

    return pl.pallas_call(
        functools.partial(_attn_both_kernel, lam_init=lam_init),
        grid=(n_b, DIFF_HEADS),
        in_specs=[pl.BlockSpec((None,) + wl.shape[1:], lambda b, h: (layer, 0, 0)),
                  pl.BlockSpec((None,) + gsub_col.shape[1:], lambda b, h: (layer, 0, 0)),
                  pl.BlockSpec((None, hw, seq), lambda b, h: (b, h, 0)),
                  pl.BlockSpec((None, seq, hw), lambda b, h: (b, 0, h)),
                  pl.BlockSpec((None, n_kt, hva, tk), lambda b, h: (b, 0, h, 0)),
                  pl.BlockSpec((None,) + gsub_row.shape[1:], lambda b, h: (layer, 0, 0)),
                  sample((seq_s, width)),
                  pl.BlockSpec((None, None, width, past), lambda b, h: (layer, b * DIFF_HEADS + h, 0, 0)),
                  pl.BlockSpec((None, None, past * DIFF_HEADS, hv), lambda b, h: (layer, b * DIFF_HEADS + h, 0, 0)),
                  sample((seq_s, width)), sample((seq_s, diff_v))],
        out_specs=[pl.BlockSpec((None, seq, hv), lambda b, h: (b, 0, h)), sample((seq_s, diff_v))],
        out_shape=[jax.ShapeDtypeStruct((n_b, seq, diff_v), BF16),
                   jax.ShapeDtypeStruct((n_s, seq_s, diff_v), BF16)],
        scratch_shapes=[pltpu.VMEM((1, gw), F32), pltpu.VMEM((hva, gw), F32)] * (2 * (tk // gw)),
        compiler_params=_cparams(2),
        name="attn_both",
    )(wl, gsub_col, qt, k, vt, gsub_row, q_s, cache_kt, cache_vi, k_new, v_new)


def _mix_out_kernel(x_ref, or_ref, od_ref, gate1_ref, shift_ref, scale_ref, gate2_ref, g_ref,
                    wout_ref, wup_ref, wdn_ref, *rest):
    o_ref = _run_riders(rest)
    n_seq, lt, d = x_ref.shape
    m = n_seq * lt
    rw = or_ref.shape[-1]
    mix = (_dot(or_ref[...].reshape(m, rw), wout_ref[:rw, :])
           + _dot(od_ref[...].reshape(m, od_ref.shape[-1]), wout_ref[rw:, :]))
    x3 = x_ref[...] + gate1_ref[...] * mix.reshape(n_seq, lt, d)
    o_ref[...] = _ffn_apply(x3, g_ref[...], shift_ref[...], scale_ref[...], gate2_ref[...], wup_ref, wdn_ref)


def _mix_out(x, o_r, o_d, mods, g_norm, wout, wup, wdn, layer, cast_srcs=()):
    n_b, seq, d = x.shape
    n_seq, lt = _token_tiles(n_b, seq, OUT_TILE)
    grid = (n_b // n_seq, seq // lt)

    def tok(width):
        return pl.BlockSpec((n_seq, lt, width), lambda b, i: (b, i, 0))

    r_in, r_ops, r_out, r_shapes = _cast_riders(cast_srcs, *grid)
    return pl.pallas_call(
        _mix_out_kernel,
        grid=grid,
        in_specs=[tok(d), tok(o_r.shape[-1]), tok(o_d.shape[-1]),
                  _mod_spec(layer, 5, n_seq, d), _mod_spec(layer, 6, n_seq, d),
                  _mod_spec(layer, 7, n_seq, d), _mod_spec(layer, 8, n_seq, d),
                  pl.BlockSpec((None, None, 1, d), lambda b, i: (layer, 2, 0, 0)),
                  _weight(wout), _weight(wup), _weight(wdn)] + r_in,
        out_specs=[tok(d)] + r_out,
        out_shape=[jax.ShapeDtypeStruct(x.shape, F32)] + r_shapes,
        compiler_params=_cparams(2),
        name="mix_out",
    )(x, o_r, o_d, mods, mods, mods, mods, g_norm, wout, wup, wdn, *r_ops)


def kernel(x_prompt, x_sample, cache_k, cache_v, state_ret, c_prompt, c_sample, w_ada, b_ada, g_norm,
           w_ff_up, w_ff_down, w_in, w_out, g_qk, w_lambda, g_sub):
    depth = w_in.shape[0]
    n_p, seq_p, d = x_prompt.shape
    n_s, seq_s, _ = x_sample.shape
    past = cache_k.shape[2]
    dk_h, dv_h = state_ret.shape[-2], state_ret.shape[-1]
    sub = cache_k.shape[-1]
    hv = cache_v.shape[-1]
    ret_qk, ret_v = RET_HEADS * dk_h, RET_HEADS * dv_h
    diff_qk, diff_v = DIFF_HEADS * 2 * sub, DIFF_HEADS * hv
    sizes = (ret_qk, ret_v, diff_qk, diff_v)

    mods = _ada_mods(jnp.concatenate([c_prompt, c_sample], axis=0), w_ada, b_ada)
    mods = mods[:, :, :, None, :]
    mods_p, mods_s = mods[:, :, :n_p], mods[:, :, n_p:]

    rope_p = _rope_tables(np.arange(seq_p), dk_h, sub)
    rope_s = _rope_tables(past + np.arange(seq_s), dk_h, sub)
    rc_p, rc_s = min(RET_CHUNK, seq_p), min(RET_CHUNK, seq_s)
    tabs_p = _retention_tables(rc_p, dk_h, dv_h)
    tabs_s = _retention_tables(rc_s, dk_h, dv_h)
    group = np.arange(diff_qk) // sub
    gmat = jnp.asarray(group[:, None] == group[None, :], BF16)
    s0_p = jnp.zeros((n_p, RET_HEADS, dk_h, dv_h), F32)
    cache_kt = jnp.transpose(cache_k, (0, 1, 3, 4, 5, 2)).reshape(depth, n_s, diff_qk, past)
    cache_vi = cache_v.reshape(depth, n_s, past * DIFF_HEADS, hv)

    def first_ffn(l):
        return [(w_ff_up, (l, 0)), (w_ff_down, (l, 0))]

    def rest_of_layer(l):
        return [(w_ff_up, (l, 1)), (w_ff_down, (l, 1)), (w_in, (l,)), (w_out, (l,))]

    w_names = ("up0", "dn0", "up1", "dn1", "win", "wout")
    weights = {0: {"up0": w_ff_up[0, 0].astype(BF16), "dn0": w_ff_down[0, 0].astype(BF16)}}
    g_norm4 = g_norm[:, :, None, :]
    gqk = jnp.tile(g_qk, (1, 1, diff_qk // sub))[:, :, None, :]
    wl = w_lambda.astype(F32)
    gsub_row = g_sub[:, None, :]
    gsub_col = g_sub[:, :, None]

    yp, ys = x_prompt, x_sample
    kv_p, kv_s = None, None
    for l in range(depth):
        lam_init = 0.8 - 0.6 * math.exp(-0.3 * l)

        w = weights[l]
        yp, *casts = _ffn1(yp, mods_p, g_norm4, w["up0"], w["dn0"], l,
                           rest_of_layer(l) if len(w) < len(w_names) else ())
        w.update(zip(w_names[2:], casts))
        ys, = _ffn1(ys, mods_s, g_norm4, w["up0"], w["dn0"], l)
        *kv_p, q_t, k_p, v_t, or_p = _mix_in(yp, mods_p, g_norm4, w["win"], gqk, gmat, rope_p, tabs_p,
                                              s0_p, None, sizes, rc_p, l, True, kv_p)
        *kv_s, q_a, k_a, v_a, or_s = _mix_in(ys, mods_s, g_norm4, w["win"], gqk, gmat, rope_s, tabs_s,
                                              state_ret, l, sizes, rc_s, l, False, kv_s)
        if n_s == n_p * DIFF_HEADS and past <= CACHE_TK:
            od_p, od_s = _attn_both(q_t, k_p, v_t, q_a, k_a, v_a, cache_kt, cache_vi, wl, gsub_col, gsub_row,
                                    lam_init, l)
        else:
            od_p = _attn_prompt(q_t, k_p, v_t, wl, gsub_col, lam_init, l)
            od_s = _attn_sample(q_a, k_a, v_a, cache_kt, cache_vi, wl, gsub_row, lam_init, l)
        nxt = first_ffn(l + 1) + rest_of_layer(l + 1) if l + 1 < depth else ()
        yp, *casts = _mix_out(yp, or_p, od_p, mods_p, g_norm4, w["wout"], w["up1"], w["dn1"], l, nxt)
        if nxt:
            weights[l + 1] = dict(zip(w_names, casts))
        ys, = _mix_out(ys, or_s, od_s, mods_s, g_norm4, w["wout"], w["up1"], w["dn1"], l)

    k_prompt = jnp.transpose(kv_p[0].reshape(depth, n_p, DIFF_HEADS, 2, sub, seq_p), (0, 1, 5, 2, 3, 4))
    v_prompt = kv_p[1].reshape(depth, n_p, seq_p, DIFF_HEADS, hv)
    k_sample = kv_s[0].reshape(depth, n_s, seq_s, DIFF_HEADS, 2, sub)
    v_sample = kv_s[1].reshape(depth, n_s, seq_s, DIFF_HEADS, hv)
    return (yp, ys, k_prompt, v_prompt, kv_p[2], k_sample, v_sample, kv_s[2])
```

```python
import functools
import math

import numpy as np
import jax
import jax.numpy as jnp
from jax import lax
from jax.experimental import pallas as pl
from jax.experimental.pallas import tpu as pltpu

F32 = jnp.float32
BF16 = jnp.bfloat16

CHUNK = 64
RET_HEADS = 4
DIFF_HEADS = 4
ROPE_THETA = 10000.0
EPS = 1e-6
LOG2E = math.log2(math.e)
MIX_TILE = 512
FFN_TILE = 1024
OUT_TILE = 512
ADA_VECTORS = 3
FF_CHUNK = 256
RET_CHUNK = 256
ATTN_GROUP = 256
ONES_ROWS = 16
ATTN_LOOKAHEAD = 3
CACHE_TK = 4096
VMEM_LIMIT = 56 * 1024 * 1024


def _cparams(n_axes):
    return pltpu.CompilerParams(dimension_semantics=("arbitrary",) * n_axes,
                                vmem_limit_bytes=VMEM_LIMIT)


def _resident(block_shape, index_map):
    return pl.BlockSpec(block_shape, index_map, pipeline_mode=pl.Buffered(1))


def _rms(x):
    return x * lax.rsqrt(jnp.mean(x * x, axis=-1, keepdims=True) + EPS)


def _dot(a, b):
    return jnp.dot(a, b, preferred_element_type=F32)


def _dot_nt(a, b):
    return lax.dot_general(a, b, (((1,), (1,)), ((), ())), preferred_element_type=F32)


def _split_bf16(x):
    hi = x.astype(BF16)
    return hi, (x - hi.astype(F32)).astype(BF16)


def _ada_kernel(c_ref, w_ref, b_ref, o_ref):
    n = c_ref.shape[0]
    c = c_ref[...]
    a_hi, a_lo = _split_bf16(c * jax.nn.sigmoid(c))
    w_hi, w_lo = _split_bf16(w_ref[...])
    a_both = jnp.concatenate([a_hi.astype(F32), a_lo.astype(F32)], axis=0).astype(BF16)
    first = _dot(a_both, w_hi)
    res = first[:n] + first[n:] + _dot(a_hi, w_lo)
    d = o_ref.shape[-1]
    for k in range(o_ref.shape[0]):
        o_ref[k] = res[:, k * d:(k + 1) * d] + b_ref[k]


def _ada_mods(c_all, w_ada, b_ada):
    depth, d, n_out = w_ada.shape
    n_seq = c_all.shape[0]
    n_blk = n_out // d
    per = ADA_VECTORS if n_blk % ADA_VECTORS == 0 else 1
    return pl.pallas_call(
        _ada_kernel,
        grid=(depth, n_blk // per),
        in_specs=[
            pl.BlockSpec((n_seq, d), lambda l, j: (0, 0)),
            pl.BlockSpec((None, d, per * d), lambda l, j: (l, 0, j)),
            pl.BlockSpec((None, per, 1, d), lambda l, j: (l, j, 0, 0)),
        ],
        out_specs=pl.BlockSpec((None, per, n_seq, d), lambda l, j: (l, j, 0, 0)),
        out_shape=jax.ShapeDtypeStruct((depth, n_blk, n_seq, d), F32),
        compiler_params=_cparams(2),
        name="ada_mods",
    )(c_all, w_ada, b_ada.reshape(depth, n_blk, 1, d))


def _pre(x3, g, shift, scale):
    return _rms(x3) * g * (1.0 + scale) + shift


def _ffn_apply(x3, g, shift, scale, gate, wup_ref, wdn_ref):
    n_seq, lt, d = x3.shape
    d_ff = wdn_ref.shape[0]
    hb = _pre(x3, g, shift, scale).reshape(n_seq * lt, d).astype(BF16)
    acc = jnp.zeros((n_seq * lt, d), F32)
    for c in range(d_ff // FF_CHUNK):
        lo = c * FF_CHUNK
        a = _dot(hb, wup_ref[:, lo:lo + FF_CHUNK])
        b = _dot(hb, wup_ref[:, d_ff + lo:d_ff + lo + FF_CHUNK])
        act = (a * jax.nn.sigmoid(a) * b).astype(BF16)
        acc = acc + _dot(act, wdn_ref[lo:lo + FF_CHUNK, :])
    return x3 + (0.5 * gate) * acc.reshape(n_seq, lt, d)


def _cast_riders(srcs, n_outer, n_inner):
    n_steps = n_outer * n_inner
    in_specs, operands, out_specs, out_shapes = [], [], [], []
    for w, lead in srcs:
        rows, cols = w.shape[-2:]
        n_blk = n_steps
        while rows % n_blk or (rows // n_blk) % 16:
            n_blk //= 2
        rep, blk = n_steps // n_blk, rows // n_blk
        in_specs.append(pl.BlockSpec((None,) * len(lead) + (blk, cols),
                                     lambda b, i, lead=lead, rep=rep: lead + ((b * n_inner + i) // rep, 0)))
        out_specs.append(pl.BlockSpec((blk, cols), lambda b, i, rep=rep: ((b * n_inner + i) // rep, 0)))
        out_shapes.append(jax.ShapeDtypeStruct((rows, cols), BF16))
        operands.append(w)
    return in_specs, operands, out_specs, out_shapes


def _run_riders(rest):
    n_r = (len(rest) - 1) // 2
    for src, dst in zip(rest[:n_r], rest[n_r + 1:]):
        dst[...] = src[...].astype(BF16)
    return rest[n_r]


def _ffn_kernel(x_ref, shift_ref, scale_ref, gate_ref, g_ref, wup_ref, wdn_ref, *rest):
    o_ref = _run_riders(rest)
    o_ref[...] = _ffn_apply(x_ref[...], g_ref[...], shift_ref[...], scale_ref[...], gate_ref[...],
                            wup_ref, wdn_ref)


def _token_tiles(n_b, seq, tile):
    lt = min(seq, tile)
    n_seq = max(1, min(n_b, tile // lt))
    assert seq % lt == 0 and n_b % n_seq == 0 and lt % 8 == 0
    return n_seq, lt


def _mod_spec(layer, idx, n_seq, d):
    return pl.BlockSpec((None, None, n_seq, 1, d), lambda b, i: (layer, idx, b, 0, 0))


def _weight(w):
    return _resident(w.shape, lambda b, i: (0, 0))


def _ffn1(x, mods, g_norm, wup, wdn, layer, cast_srcs=()):
    n_b, seq, d = x.shape
    n_seq, lt = _token_tiles(n_b, seq, FFN_TILE)
    grid = (n_b // n_seq, seq // lt)
    x_spec = pl.BlockSpec((n_seq, lt, d), lambda b, i: (b, i, 0))
    r_in, r_ops, r_out, r_shapes = _cast_riders(cast_srcs, *grid)
    return pl.pallas_call(
        _ffn_kernel,
        grid=grid,
        in_specs=[x_spec, _mod_spec(layer, 0, n_seq, d), _mod_spec(layer, 1, n_seq, d),
                  _mod_spec(layer, 2, n_seq, d),
                  pl.BlockSpec((None, None, 1, d), lambda b, i: (layer, 0, 0, 0)),
                  _weight(wup), _weight(wdn)] + r_in,
        out_specs=[x_spec] + r_out,
        out_shape=[jax.ShapeDtypeStruct(x.shape, F32)] + r_shapes,
        compiler_params=_cparams(2),
        name="ffn1",
    )(x, mods, mods, mods, g_norm, wup, wdn, *r_ops)


def _swap_pairs(x):
    w = x.shape[-1]
    lane = lax.broadcasted_iota(jnp.int32, (1, w), 1)
    return jnp.where((lane & 1) == 0, pltpu.roll(x, w - 1, 1), pltpu.roll(x, 1, 1))


def _swap_halves(x, half):
    w = x.shape[-1]
    lane = lax.broadcasted_iota(jnp.int32, (1, w), 1)
    return jnp.where((lane & (2 * half - 1)) < half, pltpu.roll(x, w - half, 1), pltpu.roll(x, half, 1))


def _mix_in_kernel(x_ref, shift_ref, scale_ref, g_ref, win_ref, gqk_ref, gmat_ref,
                   cosr_ref, sinr_ref, cosd_ref, sind_ref, dmask_ref, qdec_ref, kdec_ref, cdec_ref, s0_ref,
                   *rest, rc, sizes, prompt, n_prev):
    if n_prev:
        dkp_ref, dvp_ref, sp_ref = rest[:3]
        rest = rest[3:]
    dk_ref, dv_ref, snew_ref, qa_ref, ka_ref, va_ref, or_ref, s_scr = rest
    @pl.when(pl.program_id(1) == 0)
    def _():
        s_scr[...] = s0_ref[...]

    if n_prev:
        dk_ref[:n_prev] = dkp_ref[...]
        dv_ref[:n_prev] = dvp_ref[...]
        snew_ref[:n_prev] = sp_ref[...]
    ret_qk, ret_v, diff_qk, diff_v = sizes
    n_seq, lt, d = x_ref.shape
    m = n_seq * lt
    dk_h = ret_qk // RET_HEADS
    dv_h = ret_v // RET_HEADS
    hv = diff_v // DIFF_HEADS
    hb = _pre(x_ref[...], g_ref[...], shift_ref[...], scale_ref[...]).reshape(m, d).astype(BF16)

    widths = (ret_qk, ret_qk, ret_v, ret_v, diff_qk, diff_qk, diff_v)

    def proj(n):
        lo = sum(widths[:n])
        return _dot(hb, win_ref[:, lo:lo + widths[n]])

    def rope(x, partner, cos_ref, sin_ref):
        w = x.shape[-1]
        y = x.reshape(n_seq, lt, w) * cos_ref[...] + partner.reshape(n_seq, lt, w) * sin_ref[...]
        return y.reshape(m, w)

    rq, rk = proj(0), proj(1)
    dq, dk = proj(4), proj(5)
    dv = proj(6)

    rq = rope(rq, _swap_pairs(rq), cosr_ref, sinr_ref)
    rk = rope(rk, _swap_pairs(rk), cosr_ref, sinr_ref) * (dk_h ** -0.5)
    chunks = [(s, c, s * lt + c * rc) for s in range(n_seq) for c in range(lt // rc)]
    kd_t = {(s, c): jnp.transpose(rk[r0:r0 + rc] * kdec_ref[...]).astype(BF16)
            for s, c, r0 in chunks}

    sub = diff_qk // (2 * DIFF_HEADS)
    qk = jnp.concatenate([dq, dk], axis=0)
    gmean = _dot((qk * qk).astype(BF16), gmat_ref[...]) * (1.0 / sub)
    rv, rg = proj(2), proj(3)
    qk = qk * lax.rsqrt(gmean + EPS)
    dq = qk[:m] * gqk_ref[0]
    dk = qk[m:] * gqk_ref[1]
    dq = rope(dq, _swap_halves(dq, sub // 2), cosd_ref, sind_ref) * (sub ** -0.5 * LOG2E)
    dk = rope(dk, _swap_halves(dk, sub // 2), cosd_ref, sind_ref)

    for s in range(n_seq):
        rows = slice(s * lt, (s + 1) * lt)
        for h in range(DIFF_HEADS):
            dv_ref[n_prev, s, pl.ds(h, lt, stride=DIFF_HEADS), :] = dv[rows, h * hv:(h + 1) * hv]
        if prompt:
            dk_t = jnp.transpose(dk[rows])
            dk_ref[n_prev, s] = dk_t
            ka_ref[s] = dk[rows].astype(BF16)
            qa_ref[s] = jnp.transpose(dq[rows]).astype(BF16)
            v_t = jnp.transpose(dv[rows]).astype(BF16)
            ones = jnp.ones((ONES_ROWS, lt), BF16)
            va_ref[s] = jnp.concatenate(
                [part for h in range(DIFF_HEADS) for part in (v_t[h * hv:(h + 1) * hv], ones)], axis=0)
        else:
            for j in range(diff_qk // sub):
                dk_ref[n_prev, s, :, j, :] = dk[rows, j * sub:(j + 1) * sub]
            ka_ref[s] = dk[rows].astype(BF16)
            qa_ref[s] = dq[rows].astype(BF16)
            va_ref[s] = dv[rows].astype(BF16)

    lane_q = lax.broadcasted_iota(jnp.int32, (1, ret_qk), 1)
    gate = rg * jax.nn.sigmoid(rg)
    rv_b = rv.astype(BF16)
    zero_blk = jnp.zeros((dk_h, dv_h), BF16)

    def head_cols(h, width):
        return slice(h * width, (h + 1) * width)

    att, upd = {}, {}
    for s, c, r0 in chunks:
        q = rq[r0:r0 + rc]
        q_heads = jnp.concatenate(
            [jnp.where((lane_q >= h * dk_h) & (lane_q < (h + 1) * dk_h), q, 0.0) for h in range(RET_HEADS)],
            axis=0).astype(BF16)
        att[s, c] = (_dot_nt(q_heads, rk[r0:r0 + rc].astype(BF16)) * dmask_ref[...]).astype(BF16)
    for s, c, r0 in chunks:
        upd[s, c] = [_dot(kd_t[s, c][head_cols(h, dk_h)], rv_b[r0:r0 + rc, head_cols(h, dv_h)])
                     for h in range(RET_HEADS)]
    for s in range(n_seq):
        state = [s_scr[s, h] for h in range(RET_HEADS)]
        for c in range(lt // rc):
            r0 = s * lt + c * rc
            s_diag = jnp.concatenate(
                [jnp.concatenate([state[h].astype(BF16) if g == h else zero_blk for g in range(RET_HEADS)], axis=1)
                 for h in range(RET_HEADS)], axis=0)
            inter = _dot((rq[r0:r0 + rc] * qdec_ref[...]).astype(BF16), s_diag)
            for h in range(RET_HEADS):
                cols = head_cols(h, dv_h)
                o = _dot(att[s, c][h * rc:(h + 1) * rc], rv_b[r0:r0 + rc, cols]) + inter[:, cols]
                or_ref[s, c * rc:(c + 1) * rc, cols] = (_rms(o) * gate[r0:r0 + rc, cols]).astype(BF16)
                state[h] = cdec_ref[h] * state[h] + upd[s, c][h]
        for h in range(RET_HEADS):
            s_scr[s, h] = state[h]
            snew_ref[n_prev, s, h] = state[h]


def _retention_tables(rc, dk_h, dv_h):
    n_h = RET_HEADS
    log_g = np.log1p(-np.exp2(-5.0 - np.arange(n_h, dtype=np.float64)))
    idx = np.arange(rc, dtype=np.float64)
    rel = idx[:, None] - idx[None, :]
    dmask = np.where(rel >= 0, np.exp(log_g[:, None, None] * np.maximum(rel, 0.0)), 0.0)
    q_dec = np.exp(log_g[None, :] * (idx[:, None] + 1.0))
    k_dec = np.exp(log_g[None, :] * (rc - 1.0 - idx[:, None]))
    c_dec = np.exp(log_g * rc)
    tabs = (dmask.reshape(n_h * rc, rc), np.repeat(q_dec, dk_h, axis=1), np.repeat(k_dec, dk_h, axis=1),
            np.broadcast_to(c_dec[:, None, None], (n_h, dk_h, dv_h)))
    return tuple(jnp.asarray(t, F32) for t in tabs)


def _rope_tables(pos, dk_h, sub):
    posf = np.asarray(pos, np.float64)[:, None]
    ret_freq = 1.0 / (ROPE_THETA ** np.linspace(0.0, 1.0, dk_h // 2))
    ang = posf * ret_freq[None, :]
    sign = np.tile(np.array([-1.0, 1.0]), dk_h // 2)
    cos_r = np.tile(np.repeat(np.cos(ang), 2, axis=1), (1, RET_HEADS))
    sin_r = np.tile(np.repeat(np.sin(ang), 2, axis=1) * sign, (1, RET_HEADS))
    rope_freq = 1.0 / (ROPE_THETA ** (np.arange(0, sub, 2, dtype=np.float64) / sub))
    ang = posf * rope_freq[None, :]
    sign = np.concatenate([-np.ones(sub // 2), np.ones(sub // 2)])
    cos_d = np.tile(np.tile(np.cos(ang), (1, 2)), (1, 2 * DIFF_HEADS))
    sin_d = np.tile(np.tile(np.sin(ang), (1, 2)) * sign, (1, 2 * DIFF_HEADS))
    return tuple(jnp.asarray(t, F32) for t in (cos_r, sin_r, cos_d, sin_d))


def _mix_in(x, mods, g_norm, win, gqk, gmat, rope, ret_tabs, s0, s0_layer, sizes, rc, layer, prompt, prev_kv):
    n_b, seq, d = x.shape
    n_prev = 0 if prev_kv is None else prev_kv[0].shape[0]
    ret_qk, ret_v, diff_qk, diff_v = sizes
    n_seq, lt = _token_tiles(n_b, seq, MIX_TILE)
    assert lt % rc == 0
    dk_h, dv_h = ret_qk // RET_HEADS, ret_v // RET_HEADS
    hv = diff_v // DIFF_HEADS

    def tok(width):
        return pl.BlockSpec((n_seq, lt, width), lambda b, i: (b, i, 0))

    def feat(width):
        return pl.BlockSpec((n_seq, width, lt), lambda b, i: (b, 0, i))

    def full(a):
        return _resident(a.shape, lambda b, i, _n=a.ndim: (0,) * _n)

    def pos_tab(a):
        return pl.BlockSpec((lt, a.shape[1]), lambda b, i: (i, 0))

    state_blk = (n_seq, RET_HEADS, dk_h, dv_h)
    if s0_layer is None:
        s0_spec = pl.BlockSpec(state_blk, lambda b, i: (b, 0, 0, 0))
    else:
        s0_spec = pl.BlockSpec((None,) + state_blk, lambda b, i: (s0_layer, b, 0, 0, 0))
    dv_blk, dv_idx = (n_seq, lt * DIFF_HEADS, hv), (lambda b, i: (b, i, 0))
    dv_full = (n_b, seq * DIFF_HEADS, hv)
    if prompt:
        dk_blk, dk_idx, dk_full = (n_seq, diff_qk, lt), (lambda b, i: (b, 0, i)), (n_b, diff_qk, seq)
    else:
        sub = diff_qk // (2 * DIFF_HEADS)
        dk_blk, dk_idx = (n_seq, lt, diff_qk // sub, sub), (lambda b, i: (b, i, 0, 0))
        dk_full = (n_b, seq, diff_qk // sub, sub)

    def stacked_spec(n, blk, idx):
        return pl.BlockSpec((n,) + blk, lambda b, i: (0,) + idx(b, i))

    st_idx = lambda b, i: (b, 0, 0, 0)
    kv_specs = [stacked_spec(n_prev + 1, dk_blk, dk_idx), stacked_spec(n_prev + 1, dv_blk, dv_idx),
                stacked_spec(n_prev + 1, state_blk, st_idx)]
    kv_shapes = [jax.ShapeDtypeStruct((n_prev + 1,) + dk_full, F32),
                 jax.ShapeDtypeStruct((n_prev + 1,) + dv_full, F32),
                 jax.ShapeDtypeStruct((n_prev + 1, n_b) + state_blk[1:], F32)]
    prev_specs = [stacked_spec(n_prev, dk_blk, dk_idx), stacked_spec(n_prev, dv_blk, dv_idx),
                  stacked_spec(n_prev, state_blk, st_idx)] if n_prev else []
    if prompt:
        v_rows = diff_v + DIFF_HEADS * ONES_ROWS
        att_specs = kv_specs + [feat(diff_qk), tok(diff_qk),
                                pl.BlockSpec((n_seq, None, v_rows, lt), lambda b, i: (b, i, 0, 0))]
        att_shapes = kv_shapes + [jax.ShapeDtypeStruct((n_b, diff_qk, seq), BF16),
                                  jax.ShapeDtypeStruct((n_b, seq, diff_qk), BF16),
                                  jax.ShapeDtypeStruct((n_b, seq // lt, v_rows, lt), BF16)]
    else:
        att_specs = kv_specs + [tok(diff_qk), tok(diff_qk), tok(diff_v)]
        att_shapes = kv_shapes + [jax.ShapeDtypeStruct((n_b, seq, diff_qk), BF16),
                                  jax.ShapeDtypeStruct((n_b, seq, diff_qk), BF16),
                                  jax.ShapeDtypeStruct((n_b, seq, diff_v), BF16)]
    return pl.pallas_call(
        functools.partial(_mix_in_kernel, rc=rc, sizes=sizes, prompt=prompt, n_prev=n_prev),
        grid=(n_b // n_seq, seq // lt),
        in_specs=[tok(d), _mod_spec(layer, 3, n_seq, d), _mod_spec(layer, 4, n_seq, d),
                  pl.BlockSpec((None, None, 1, d), lambda b, i: (layer, 1, 0, 0)),
                  _weight(win),
                  pl.BlockSpec((None, 2, 1, diff_qk), lambda b, i: (layer, 0, 0, 0)),
                  full(gmat),
                  pos_tab(rope[0]), pos_tab(rope[1]), pos_tab(rope[2]), pos_tab(rope[3]),
                  full(ret_tabs[0]), full(ret_tabs[1]), full(ret_tabs[2]), full(ret_tabs[3]),
                  s0_spec] + prev_specs,
        out_specs=att_specs + [tok(ret_v)],
        out_shape=att_shapes + [jax.ShapeDtypeStruct((n_b, seq, ret_v), BF16)],
        scratch_shapes=[pltpu.VMEM(state_blk, F32)],
        compiler_params=_cparams(2),
        name="mix_in",
    )(x, mods, mods, g_norm, win, gqk, gmat, *rope, *ret_tabs, s0, *(prev_kv or ()))


def _lambda(wl_ref, lam_init):
    wl = wl_ref[...]
    a = jnp.sum(wl[0:1] * wl[1:2], axis=-1, keepdims=True)
    b = jnp.sum(wl[2:3] * wl[3:4], axis=-1, keepdims=True)
    return jnp.exp(a) - jnp.exp(b) + lam_init


def _softmax_step_t(s_t, v_t, m_ref, acc_ref):
    m_prev = m_ref[...]
    m_new = jnp.maximum(m_prev, jnp.max(s_t, axis=0, keepdims=True))
    p = jnp.exp2(s_t - m_new)
    acc_ref[...] = jnp.exp2(m_prev - m_new) * acc_ref[...] + _dot(v_t, p.astype(BF16))
    m_ref[...] = m_new


def _attn_prompt_kernel(wl_ref, gsub_ref, qt_ref, k_ref, vt_ref, o_ref, *stats, lam_init):
    hw, seq = qt_ref.shape
    n_kt, _, tk = vt_ref.shape
    tq = tk
    gw = min(tq, ATTN_GROUP)
    n_g = tq // gw
    feat = lax.broadcasted_iota(jnp.int32, (hw, 1), 0)
    lam = _lambda(wl_ref, lam_init)
    k_chunk = lax.broadcasted_iota(jnp.int32, (gw, gw), 0) // CHUNK
    q_chunk = lax.broadcasted_iota(jnp.int32, (gw, gw), 1) // CHUNK
    diag_mask = jnp.where(k_chunk <= q_chunk, 0.0, -jnp.inf).astype(F32)

    hv = o_ref.shape[-1]

    def refs(sub, g):
        return stats[2 * (sub * n_g + g):2 * (sub * n_g + g) + 2]

    def ratio(sub, g):
        acc = refs(sub, g)[1][...]
        return acc[:hv] / acc[hv:hv + 1]

    units = [(i, j, sub, g, tk if j < i else min(tk, (g + 1) * gw))
             for i in range(seq // tq) for j in range(i + 1) for sub in range(2) for g in range(n_g)]

    def scores(unit):
        i, j, sub, g, n_keys = unit
        q = qt_ref[:, i * tq + g * gw:i * tq + (g + 1) * gw]
        q = jnp.where(feat < hw // 2 if sub == 0 else feat >= hw // 2, q, jnp.zeros_like(q))
        s_t = _dot(k_ref[j * tk:j * tk + n_keys, :], q)
        if j == i:
            tail = s_t[n_keys - gw:] + diag_mask
            s_t = tail if n_keys == gw else jnp.concatenate([s_t[:n_keys - gw], tail], axis=0)
        return s_t

    pending = [scores(u) for u in units[:ATTN_LOOKAHEAD]]
    for n, (i, j, sub, g, n_keys) in enumerate(units):
        if n + ATTN_LOOKAHEAD < len(units):
            pending.append(scores(units[n + ATTN_LOOKAHEAD]))
        m_ref, acc_ref = refs(sub, g)
        if j == 0:
            m_ref[...] = jnp.full(m_ref.shape, -jnp.inf, F32)
            acc_ref[...] = jnp.zeros(acc_ref.shape, F32)
        _softmax_step_t(pending.pop(0), vt_ref[j, :, :n_keys], m_ref, acc_ref)
        if j == i and sub == 1 and g == n_g - 1:
            o_t = jnp.concatenate([ratio(0, c) - lam * ratio(1, c) for c in range(n_g)], axis=1)
            o_t = (o_t * lax.rsqrt(jnp.mean(o_t * o_t, axis=0, keepdims=True) + EPS)
                   * gsub_ref[...] * (1.0 - lam_init))
            o_ref[i * tq:(i + 1) * tq, :] = jnp.transpose(o_t).astype(BF16)


def _attn_prompt(qt, k, vt, wl, gsub_col, lam_init, layer):
    n_b, width, seq = qt.shape
    n_kt, v_rows, tk = vt.shape[1:]
    hw, hva = width // DIFF_HEADS, v_rows // DIFF_HEADS
    hv = hva - ONES_ROWS
    diff_v = hv * DIFF_HEADS
    gw = min(tk, ATTN_GROUP)
    assert seq == n_kt * tk and tk % CHUNK == 0 and tk % gw == 0
    return pl.pallas_call(
        functools.partial(_attn_prompt_kernel, lam_init=lam_init),
        grid=(n_b, DIFF_HEADS),
        in_specs=[pl.BlockSpec((None,) + wl.shape[1:], lambda b, h: (layer, 0, 0)),
                  pl.BlockSpec((None,) + gsub_col.shape[1:], lambda b, h: (layer, 0, 0)),
                  pl.BlockSpec((None, hw, seq), lambda b, h: (b, h, 0)),
                  pl.BlockSpec((None, seq, hw), lambda b, h: (b, 0, h)),
                  pl.BlockSpec((None, n_kt, hva, tk), lambda b, h: (b, 0, h, 0))],
        out_specs=pl.BlockSpec((None, seq, hv), lambda b, h: (b, 0, h)),
        out_shape=jax.ShapeDtypeStruct((n_b, seq, diff_v), BF16),
        scratch_shapes=[pltpu.VMEM((1, gw), F32), pltpu.VMEM((hva, gw), F32)] * (2 * (tk // gw)),
        compiler_params=_cparams(2),
        name="attn_prompt",
    )(wl, gsub_col, qt, k, vt)


def _stack_subheads(q):
    w = q.shape[-1]
    lane = lax.broadcasted_iota(jnp.int32, (1, w), 1)
    zero = jnp.zeros_like(q)
    return jnp.concatenate([jnp.where(lane < w // 2, q, zero), jnp.where(lane >= w // 2, q, zero)], axis=0)


def _softmax_step(s, v, m_ref, l_ref, acc_ref):
    m_prev = m_ref[...]
    m_new = jnp.maximum(m_prev, jnp.max(s, axis=-1, keepdims=True))
    alpha = jnp.exp2(m_prev - m_new)
    p = jnp.exp2(s - m_new)
    l_ref[...] = alpha * l_ref[...] + jnp.sum(p, axis=-1, keepdims=True)
    acc_ref[...] = alpha * acc_ref[...] + _dot(p.astype(BF16), v)
    m_ref[...] = m_new


class _SampleAttn:
    def __init__(self, wl_ref, gsub_ref, q_ref, kct_ref, vc_ref, kn_ref, vn_ref, o_ref, lam_init):
        self.refs = (wl_ref, gsub_ref, q_ref, kct_ref, vc_ref, kn_ref, vn_ref, o_ref)
        self.lam_init = lam_init
        self.hw = q_ref.shape[-1] // DIFF_HEADS
        self.hv = vc_ref.shape[-1]
        self.tk = kct_ref.shape[-1]

    def head_q(self, h):
        return _stack_subheads(self.refs[2][:, h * self.hw:(h + 1) * self.hw])

    def cache_scores(self, h):
        return _dot(self.head_q(h), self.refs[3][h * self.hw:(h + 1) * self.hw, :].astype(BF16))

    def cache_v(self, h):
        return self.refs[4][pl.ds(h, self.tk, stride=DIFF_HEADS), :].astype(BF16)

    def new_scores(self, h):
        return _dot_nt(self.head_q(h), self.refs[5][:, h * self.hw:(h + 1) * self.hw])

    def new_v(self, h):
        return self.refs[6][:, h * self.hv:(h + 1) * self.hv]

    def finish(self, h, acc, l):
        rows = acc.shape[0] // 2
        o = acc[:rows] / l[:rows] - _lambda(self.refs[0], self.lam_init) * (acc[rows:] / l[rows:])
        self.refs[7][:, h * self.hv:(h + 1) * self.hv] = (
            _rms(o) * self.refs[1][...] * (1.0 - self.lam_init)).astype(BF16)

    def whole_head(self, h, s_cache=None):
        s_cache = self.cache_scores(h) if s_cache is None else s_cache
        s_new = self.new_scores(h)
        m = jnp.maximum(jnp.max(s_cache, axis=-1, keepdims=True), jnp.max(s_new, axis=-1, keepdims=True))
        p_c = jnp.exp2(s_cache - m)
        p_n = jnp.exp2(s_new - m)
        l = jnp.sum(p_c, axis=-1, keepdims=True) + jnp.sum(p_n, axis=-1, keepdims=True)
        acc = _dot(p_c.astype(BF16), self.cache_v(h)) + _dot(p_n.astype(BF16), self.new_v(h))
        self.finish(h, acc, l)


def _attn_sample_kernel(wl_ref, gsub_ref, q_ref, kct_ref, vc_ref, kn_ref, vn_ref, o_ref,
                        m_ref, l_ref, acc_ref, *, lam_init, single_tile):
    j = pl.program_id(1)
    att = _SampleAttn(wl_ref, gsub_ref, q_ref, kct_ref, vc_ref, kn_ref, vn_ref, o_ref, lam_init)
    scores = [att.cache_scores(h) for h in range(DIFF_HEADS)]

    if single_tile:
        for h in range(DIFF_HEADS):
            att.whole_head(h, scores[h])
        return

    @pl.when(j == 0)
    def _():
        m_ref[...] = jnp.full(m_ref.shape, -jnp.inf, F32)
        l_ref[...] = jnp.zeros(l_ref.shape, F32)
        acc_ref[...] = jnp.zeros(acc_ref.shape, F32)

    for h in range(DIFF_HEADS):
        _softmax_step(scores[h], att.cache_v(h), m_ref.at[h], l_ref.at[h], acc_ref.at[h])

    @pl.when(j == pl.num_programs(1) - 1)
    def _():
        for h in range(DIFF_HEADS):
            _softmax_step(att.new_scores(h), att.new_v(h), m_ref.at[h], l_ref.at[h], acc_ref.at[h])
            att.finish(h, acc_ref[h], l_ref[h])


def _attn_sample(q, k_new, v_new, cache_kt, cache_vi, wl, gsub_row, lam_init, layer):
    n_b, seq, width = q.shape
    past = cache_kt.shape[-1]
    hv = cache_vi.shape[-1]
    vw = v_new.shape[-1]
    assert past % CHUNK == 0 and seq <= CHUNK
    tk = min(past, CACHE_TK)
    assert past % tk == 0
    return pl.pallas_call(
        functools.partial(_attn_sample_kernel, lam_init=lam_init, single_tile=past == tk),
        grid=(n_b, past // tk),
        in_specs=[pl.BlockSpec((None,) + wl.shape[1:], lambda b, j: (layer, 0, 0)),
                  pl.BlockSpec((None,) + gsub_row.shape[1:], lambda b, j: (layer, 0, 0)),
                  pl.BlockSpec((None, seq, width), lambda b, j: (b, 0, 0)),
                  pl.BlockSpec((None, None, width, tk), lambda b, j: (layer, b, 0, j)),
                  pl.BlockSpec((None, None, tk * DIFF_HEADS, hv), lambda b, j: (layer, b, j, 0)),
                  pl.BlockSpec((None, seq, width), lambda b, j: (b, 0, 0)),
                  pl.BlockSpec((None, seq, vw), lambda b, j: (b, 0, 0))],
        out_specs=pl.BlockSpec((None, seq, vw), lambda b, j: (b, 0, 0)),
        out_shape=jax.ShapeDtypeStruct((n_b, seq, vw), BF16),
        scratch_shapes=[pltpu.VMEM((DIFF_HEADS, 2 * seq, 1), F32), pltpu.VMEM((DIFF_HEADS, 2 * seq, 1), F32),
                        pltpu.VMEM((DIFF_HEADS, 2 * seq, hv), F32)],
        compiler_params=_cparams(2),
        name="attn_sample",
    )(wl, gsub_row, q, cache_kt, cache_vi, k_new, v_new)


def _attn_both_kernel(wl_ref, gcol_ref, qt_ref, k_ref, vt_ref, grow_ref, q_ref, kct_ref, vc_ref, kn_ref, vn_ref,
                      op_ref, os_ref, *stats, lam_init):
    att = _SampleAttn(wl_ref, grow_ref, q_ref, kct_ref, vc_ref, kn_ref, vn_ref, os_ref, lam_init)
    s_cache = att.cache_scores(0)
    for h in range(DIFF_HEADS):
        s_next = att.cache_scores(h + 1) if h + 1 < DIFF_HEADS else None
        att.whole_head(h, s_cache)
        s_cache = s_next
    _attn_prompt_kernel(wl_ref, gcol_ref, qt_ref, k_ref, vt_ref, op_ref, *stats, lam_init=lam_init)


def _attn_both(qt, k, vt, q_s, k_new, v_new, cache_kt, cache_vi, wl, gsub_col, gsub_row, lam_init, layer):
    n_b, width, seq = qt.shape
    n_kt, v_rows, tk = vt.shape[1:]
    n_s, seq_s, _ = q_s.shape
    past = cache_kt.shape[-1]
    hw, hva = width // DIFF_HEADS, v_rows // DIFF_HEADS
    hv = hva - ONES_ROWS
    diff_v = hv * DIFF_HEADS
    gw = min(tk, ATTN_GROUP)
    assert seq == n_kt * tk and tk % CHUNK == 0 and tk % gw == 0
    assert n_s == n_b * DIFF_HEADS and past % CHUNK == 0 and seq_s <= CHUNK

    def sample(blk):
        return pl.BlockSpec((None,) + blk, lambda b, h: (b * DIFF_HEADS + h, 0, 0))

    return pl.pallas_call(
        functools.partial(_attn_both_kernel, lam_init=lam_init),
        grid=(n_b, DIFF_HEADS),
        in_specs=[pl.BlockSpec((None,) + wl.shape[1:], lambda b, h: (layer, 0, 0)),
                  pl.BlockSpec((None,) + gsub_col.shape[1:], lambda b, h: (layer, 0, 0)),
                  pl.BlockSpec((None, hw, seq), lambda b, h: (b, h, 0)),
                  pl.BlockSpec((None, seq, hw), lambda b, h: (b, 0, h)),
                  pl.BlockSpec((None, n_kt, hva, tk), lambda b, h: (b, 0, h, 0)),
                  pl.BlockSpec((None,) + gsub_row.shape[1:], lambda b, h: (layer, 0, 0)),
                  sample((seq_s, width)),
                  pl.BlockSpec((None, None, width, past), lambda b, h: (layer, b * DIFF_HEADS + h, 0, 0)),
                  pl.BlockSpec((None, None, past * DIFF_HEADS, hv), lambda b, h: (layer, b * DIFF_HEADS + h, 0, 0)),
                  sample((seq_s, width)), sample((seq_s, diff_v))],
        out_specs=[pl.BlockSpec((None, seq, hv), lambda b, h: (b, 0, h)), sample((seq_s, diff_v))],
        out_shape=[jax.ShapeDtypeStruct((n_b, seq, diff_v), BF16),
                   jax.ShapeDtypeStruct((n_s, seq_s, diff_v), BF16)],
        scratch_shapes=[pltpu.VMEM((1, gw), F32), pltpu.VMEM((hva, gw), F32)] * (2 * (tk // gw)),
        compiler_params=_cparams(2),
        name="attn_both",
    )(wl, gsub_col, qt, k, vt, gsub_row, q_s, cache_kt, cache_vi, k_new, v_new)


def _mix_out_kernel(x_ref, or_ref, od_ref, gate1_ref, shift_ref, scale_ref, gate2_ref, g_ref,
                    wout_ref, wup_ref, wdn_ref, *rest):
    o_ref = _run_riders(rest)
    n_seq, lt, d = x_ref.shape
    m = n_seq * lt
    rw = or_ref.shape[-1]
    mix = (_dot(or_ref[...].reshape(m, rw), wout_ref[:rw, :])
           + _dot(od_ref[...].reshape(m, od_ref.shape[-1]), wout_ref[rw:, :]))
    x3 = x_ref[...] + gate1_ref[...] * mix.reshape(n_seq, lt, d)
    o_ref[...] = _ffn_apply(x3, g_ref[...], shift_ref[...], scale_ref[...], gate2_ref[...], wup_ref, wdn_ref)


def _mix_out(x, o_r, o_d, mods, g_norm, wout, wup, wdn, layer, cast_srcs=()):
    n_b, seq, d = x.shape
    n_seq, lt = _token_tiles(n_b, seq, OUT_TILE)
    grid = (n_b // n_seq, seq // lt)

    def tok(width):
        return pl.BlockSpec((n_seq, lt, width), lambda b, i: (b, i, 0))

    r_in, r_ops, r_out, r_shapes = _cast_riders(cast_srcs, *grid)
    return pl.pallas_call(
        _mix_out_kernel,
        grid=grid,
        in_specs=[tok(d), tok(o_r.shape[-1]), tok(o_d.shape[-1]),
                  _mod_spec(layer, 5, n_seq, d), _mod_spec(layer, 6, n_seq, d),
                  _mod_spec(layer, 7, n_seq, d), _mod_spec(layer, 8, n_seq, d),
                  pl.BlockSpec((None, None, 1, d), lambda b, i: (layer, 2, 0, 0)),
                  _weight(wout), _weight(wup), _weight(wdn)] + r_in,
        out_specs=[tok(d)] + r_out,
        out_shape=[jax.ShapeDtypeStruct(x.shape, F32)] + r_shapes,
        compiler_params=_cparams(2),
        name="mix_out",
    )(x, o_r, o_d, mods, mods, mods, mods, g_norm, wout, wup, wdn, *r_ops)


def kernel(x_prompt, x_sample, cache_k, cache_v, state_ret, c_prompt, c_sample, w_ada, b_ada, g_norm,
           w_ff_up, w_ff_down, w_in, w_out, g_qk, w_lambda, g_sub):
    depth = w_in.shape[0]
    n_p, seq_p, d = x_prompt.shape
    n_s, seq_s, _ = x_sample.shape
    past = cache_k.shape[2]
    dk_h, dv_h = state_ret.shape[-2], state_ret.shape[-1]
    sub = cache_k.shape[-1]
    hv = cache_v.shape[-1]
    ret_qk, ret_v = RET_HEADS * dk_h, RET_HEADS * dv_h
    diff_qk, diff_v = DIFF_HEADS * 2 * sub, DIFF_HEADS * hv
    sizes = (ret_qk, ret_v, diff_qk, diff_v)

    mods = _ada_mods(jnp.concatenate([c_prompt, c_sample], axis=0), w_ada, b_ada)
    mods = mods[:, :, :, None, :]
    mods_p, mods_s = mods[:, :, :n_p], mods[:, :, n_p:]

    rope_p = _rope_tables(np.arange(seq_p), dk_h, sub)
    rope_s = _rope_tables(past + np.arange(seq_s), dk_h, sub)
    rc_p, rc_s = min(RET_CHUNK, seq_p), min(RET_CHUNK, seq_s)
    tabs_p = _retention_tables(rc_p, dk_h, dv_h)
    tabs_s = _retention_tables(rc_s, dk_h, dv_h)
    group = np.arange(diff_qk) // sub
    gmat = jnp.asarray(group[:, None] == group[None, :], BF16)
    s0_p = jnp.zeros((n_p, RET_HEADS, dk_h, dv_h), F32)
    cache_kt = jnp.transpose(cache_k, (0, 1, 3, 4, 5, 2)).reshape(depth, n_s, diff_qk, past)
    cache_vi = cache_v.reshape(depth, n_s, past * DIFF_HEADS, hv)

    def first_ffn(l):
        return [(w_ff_up, (l, 0)), (w_ff_down, (l, 0))]

    def rest_of_layer(l):
        return [(w_ff_up, (l, 1)), (w_ff_down, (l, 1)), (w_in, (l,)), (w_out, (l,))]

    w_names = ("up0", "dn0", "up1", "dn1", "win", "wout")
    weights = {0: {"up0": w_ff_up[0, 0].astype(BF16), "dn0": w_ff_down[0, 0].astype(BF16)}}
    g_norm4 = g_norm[:, :, None, :]
    gqk = jnp.tile(g_qk, (1, 1, diff_qk // sub))[:, :, None, :]
    wl = w_lambda.astype(F32)
    gsub_row = g_sub[:, None, :]
    gsub_col = g_sub[:, :, None]

    yp, ys = x_prompt, x_sample
    kv_p, kv_s = None, None
    for l in range(depth):
        lam_init = 0.8 - 0.6 * math.exp(-0.3 * l)

        w = weights[l]
        yp, *casts = _ffn1(yp, mods_p, g_norm4, w["up0"], w["dn0"], l,
                           rest_of_layer(l) if len(w) < len(w_names) else ())
        w.update(zip(w_names[2:], casts))
        ys, = _ffn1(ys, mods_s, g_norm4, w["up0"], w["dn0"], l)
        *kv_p, q_t, k_p, v_t, or_p = _mix_in(yp, mods_p, g_norm4, w["win"], gqk, gmat, rope_p, tabs_p,
                                              s0_p, None, sizes, rc_p, l, True, kv_p)
        *kv_s, q_a, k_a, v_a, or_s = _mix_in(ys, mods_s, g_norm4, w["win"], gqk, gmat, rope_s, tabs_s,
                                              state_ret, l, sizes, rc_s, l, False, kv_s)
        if n_s == n_p * DIFF_HEADS and past <= CACHE_TK:
            od_p, od_s = _attn_both(q_t, k_p, v_t, q_a, k_a, v_a, cache_kt, cache_vi, wl, gsub_col, gsub_row,
                                    lam_init, l)
        else:
            od_p = _attn_prompt(q_t, k_p, v_t, wl, gsub_col, lam_init, l)
            od_s = _attn_sample(q_a, k_a, v_a, cache_kt, cache_vi, wl, gsub_row, lam_init, l)
        nxt = first_ffn(l + 1) + rest_of_layer(l + 1) if l + 1 < depth else ()
        yp, *casts = _mix_out(yp, or_p, od_p, mods_p, g_norm4, w["wout"], w["up1"], w["dn1"], l, nxt)
        if nxt:
            weights[l + 1] = dict(zip(w_names, casts))
        ys, = _mix_out(ys, or_s, od_s, mods_s, g_norm4, w["wout"], w["up1"], w["dn1"], l)

    k_prompt = jnp.transpose(kv_p[0].reshape(depth, n_p, DIFF_HEADS, 2, sub, seq_p), (0, 1, 5, 2, 3, 4))
    v_prompt = kv_p[1].reshape(depth, n_p, seq_p, DIFF_HEADS, hv)
    k_sample = kv_s[0].reshape(depth, n_s, seq_s, DIFF_HEADS, 2, sub)
    v_sample = kv_s[1].reshape(depth, n_s, seq_s, DIFF_HEADS, hv)
    return (yp, ys, k_prompt, v_prompt, kv_p[2], k_sample, v_sample, kv_s[2])
```
